```python
import jax, jax.numpy as jnp
from jax import lax
import numpy as np

D_MODEL = 1024
BATCH = 8
SEQ = 4096
DEPTH = 4

N_META = 16
D_ATTN = D_MODEL // 2
D_RNN = D_MODEL - D_ATTN
N_HEADS = 8
QK_NOPE = 64
QK_ROPE = 32
QK_HEAD = QK_NOPE + QK_ROPE
V_HEAD = D_ATTN // N_HEADS
Q_LORA = 384
KV_LORA = 256
ROPE_BASE = 10000.0
N_RNN_BLOCKS = 8
RNN_BLOCK = D_RNN // N_RNN_BLOCKS
CONV_W = 4
LRU_C = 8.0
D_FF = 2816
N_EXPERTS = 8
TOP_K = 2
D_FF_EXPERT = 1408
Q_BLOCK = 128
EPS = 1e-6
N_DENSE = (DEPTH + 1) // 2
N_MOE = DEPTH // 2
SPLITS = (Q_LORA, Q_LORA + KV_LORA, Q_LORA + KV_LORA + QK_ROPE, Q_LORA + KV_LORA + QK_ROPE + D_RNN)
D_IN = Q_LORA + KV_LORA + QK_ROPE + 2 * D_RNN

kernel_name = "hymba_mla_rglru_moe_trunk"


def rmsnorm(x, g):
    xf = x.astype(jnp.float32)
    y = xf * lax.rsqrt(jnp.mean(xf * xf, axis=-1, keepdims=True) + EPS)
    return (y * g.astype(jnp.float32)).astype(x.dtype)


def rope_tables(pos):
    inv_freq = ROPE_BASE ** (-jnp.arange(0, QK_ROPE, 2, dtype=jnp.float32) / QK_ROPE)
    ang = pos.astype(jnp.float32)[..., None] * inv_freq
    return jnp.cos(ang), jnp.sin(ang)


def apply_rope(x, cos, sin):
    xf = x.astype(jnp.float32)
    x1, x2 = xf[..., : QK_ROPE // 2], xf[..., QK_ROPE // 2:]
    c, s = cos[:, :, None, :], sin[:, :, None, :]
    return jnp.concatenate([x1 * c - x2 * s, x2 * c + x1 * s], axis=-1).astype(x.dtype)


def causal_block_attention(q, k, v):
    T = q.shape[1]
    bounds = sorted(set([0, N_META] + list(range(N_META + Q_BLOCK, T, Q_BLOCK)) + [T]))
    scale = QK_HEAD ** -0.5
    outs = []
    for s, e in zip(bounds[:-1], bounds[1:]):
        sc = jnp.einsum('bqhd,bkhd->bhqk', q[:, s:e], k[:, :e]).astype(jnp.float32) * scale
        mask = jnp.arange(e)[None, :] <= jnp.arange(s, e)[:, None]
        p = jax.nn.softmax(jnp.where(mask, sc, -1e30), axis=-1)
        outs.append(jnp.einsum('bhqk,bkhd->bqhd', p.astype(v.dtype), v[:, :e]))
    return jnp.concatenate(outs, axis=1)


def mla_mixer(cq, ckv, kr, cos, sin, q_norm_g, w_uq, kv_norm_g, w_ukv, q_head_g, k_head_g):
    B, T, _ = cq.shape
    q = (rmsnorm(cq, q_norm_g) @ w_uq).reshape(B, T, N_HEADS, QK_HEAD)
    kv = (rmsnorm(ckv, kv_norm_g) @ w_ukv).reshape(B, T, N_HEADS, QK_NOPE + V_HEAD)
    k_nope, v = kv[..., :QK_NOPE], kv[..., QK_NOPE:]
    k = jnp.concatenate([k_nope, jnp.broadcast_to(kr[:, :, None, :], (B, T, N_HEADS, QK_ROPE))], axis=-1)
    q = rmsnorm(q, q_head_g)
    k = rmsnorm(k, k_head_g)
    q = jnp.concatenate([q[..., :QK_NOPE], apply_rope(q[..., QK_NOPE:], cos, sin)], axis=-1)
    k = jnp.concatenate([k[..., :QK_NOPE], apply_rope(k[..., QK_NOPE:], cos, sin)], axis=-1)
    o = causal_block_attention(q, k, v)
    return o.reshape(B, T, D_ATTN)


def _lin_combine(left, right):
    a_l, b_l = left
    a_r, b_r = right
    return a_l * a_r, a_r * b_l + b_r


def rglru_mixer(xr, gate, conv_w, conv_b, w_a, b_a, w_i, b_i, lam):
    B, T, _ = xr.shape
    xp = jnp.pad(xr, ((0, 0), (CONV_W - 1, 0), (0, 0)))
    xc = conv_b + xp[:, 0:T] * conv_w[0]
    for j in range(1, CONV_W):
        xc = xc + xp[:, j:j + T] * conv_w[j]
    xb = xc.reshape(B, T, N_RNN_BLOCKS, RNN_BLOCK)
    r = jax.nn.sigmoid(jnp.einsum('btnc,ncd->btnd', xb, w_a).reshape(B, T, D_RNN) + b_a)
    i = jax.nn.sigmoid(jnp.einsum('btnc,ncd->btnd', xb, w_i).reshape(B, T, D_RNN) + b_i)
    log_a = -LRU_C * r.astype(jnp.float32) * jax.nn.softplus(-lam.astype(jnp.float32))
    a = jnp.exp(log_a)
    mult = jnp.sqrt(-jnp.expm1(2.0 * log_a))
    b = mult * i.astype(jnp.float32) * xc.astype(jnp.float32)
    _, h = lax.associative_scan(_lin_combine, (a, b), axis=1)
    return h.astype(xr.dtype) * jax.nn.gelu(gate, approximate=True)


def swiglu(x, w_g, w_u, w_d):
    return (jax.nn.silu(x @ w_g) * (x @ w_u)) @ w_d


def moe_swiglu(x, w_router, w_g, w_u, w_d):
    B, T, D = x.shape
    xt = x.reshape(-1, D)
    logits = (xt @ w_router).astype(jnp.float32)
    top_val, top_idx = lax.top_k(logits, TOP_K)
    top_w = jax.nn.softmax(top_val, axis=-1)
    gates = jnp.sum(jax.nn.one_hot(top_idx, N_EXPERTS, dtype=jnp.float32) * top_w[..., None], axis=1)
    gates = gates.astype(x.dtype)
    out = jnp.zeros_like(xt)
    for e in range(N_EXPERTS):
        h = jax.nn.silu(xt @ w_g[e]) * (xt @ w_u[e])
        out = out + (gates[:, e:e + 1] * h) @ w_d[e]
    return out.reshape(B, T, D)


def setup_inputs(seed: int = 0) -> dict:
    key = jax.random.key(seed)
    ks = iter(jax.random.split(key, 40))

    def nrm(shape, fan_in):
        return jax.random.normal(next(ks), shape, jnp.float32) * (fan_in ** -0.5)

    def gain(shape):
        return 1.0 + 0.02 * jax.random.normal(next(ks), shape, jnp.float32)

    def bias(shape):
        return 0.02 * jax.random.normal(next(ks), shape, jnp.float32)

    x = jax.random.normal(next(ks), (BATCH, SEQ, D_MODEL), jnp.float32)
    start = jax.random.randint(next(ks), (BATCH, 1), 0, 1024, dtype=jnp.int32)
    positions = (start + jnp.arange(SEQ, dtype=jnp.int32)[None, :]).astype(jnp.int32)
    meta_tokens = jax.random.normal(next(ks), (N_META, D_MODEL), jnp.float32)
    u = jax.random.uniform(next(ks), (DEPTH, D_RNN), jnp.float32, 0.9, 0.999)
    s = u ** (1.0 / LRU_C)
    lru_lambda = jnp.log(s) - jnp.log1p(-s)
    return {
        "x": x,
        "positions": positions,
        "meta_tokens": meta_tokens,
        "norm_mix_g": gain((DEPTH, D_MODEL)),
        "w_in": nrm((DEPTH, D_MODEL, D_IN), D_MODEL),
        "q_norm_g": gain((DEPTH, Q_LORA)),
        "w_uq": nrm((DEPTH, Q_LORA, N_HEADS * QK_HEAD), Q_LORA),
        "kv_norm_g": gain((DEPTH, KV_LORA)),
        "w_ukv": nrm((DEPTH, KV_LORA, N_HEADS * (QK_NOPE + V_HEAD)), KV_LORA),
        "q_head_g": gain((DEPTH, QK_HEAD)),
        "k_head_g": gain((DEPTH, QK_HEAD)),
        "conv_w": nrm((DEPTH, CONV_W, D_RNN), CONV_W),
        "conv_b": bias((DEPTH, D_RNN)),
        "w_a": nrm((DEPTH, N_RNN_BLOCKS, RNN_BLOCK, RNN_BLOCK), RNN_BLOCK),
        "b_a": bias((DEPTH, D_RNN)),
        "w_i": nrm((DEPTH, N_RNN_BLOCKS, RNN_BLOCK, RNN_BLOCK), RNN_BLOCK),
        "b_i": bias((DEPTH, D_RNN)),
        "lru_lambda": lru_lambda,
        "attn_out_g": gain((DEPTH, D_ATTN)),
        "rnn_out_g": gain((DEPTH, D_RNN)),
        "w_out": nrm((DEPTH, D_MODEL, D_MODEL), D_MODEL),
        "norm_ffn_g": gain((DEPTH, D_MODEL)),
        "ffn_w_gate": nrm((N_DENSE, D_MODEL, D_FF), D_MODEL),
        "ffn_w_up": nrm((N_DENSE, D_MODEL, D_FF), D_MODEL),
        "ffn_w_down": nrm((N_DENSE, D_FF, D_MODEL), D_FF),
        "router_w": nrm((N_MOE, D_MODEL, N_EXPERTS), D_MODEL),
        "moe_w_gate": nrm((N_MOE, N_EXPERTS, D_MODEL, D_FF_EXPERT), D_MODEL),
        "moe_w_up": nrm((N_MOE, N_EXPERTS, D_MODEL, D_FF_EXPERT), D_MODEL),
        "moe_w_down": nrm((N_MOE, N_EXPERTS, D_FF_EXPERT, D_MODEL), D_FF_EXPERT),
    }


def reference(x, positions, meta_tokens, norm_mix_g, w_in, q_norm_g, w_uq, kv_norm_g, w_ukv,
              q_head_g, k_head_g, conv_w, conv_b, w_a, b_a, w_i, b_i, lru_lambda,
              attn_out_g, rnn_out_g, w_out, norm_ffn_g, ffn_w_gate, ffn_w_up, ffn_w_down,
              router_w, moe_w_gate, moe_w_up, moe_w_down):
    B = x.shape[0]
    meta = jnp.broadcast_to(meta_tokens[None].astype(x.dtype), (B, N_META, D_MODEL))
    h = jnp.concatenate([meta, x], axis=1)
    meta_pos = jnp.broadcast_to(jnp.arange(N_META, dtype=jnp.int32)[None, :], (B, N_META))
    pos_full = jnp.concatenate([meta_pos, positions.astype(jnp.int32) + N_META], axis=1)
    cos, sin = rope_tables(pos_full)
    for l in range(DEPTH):
        u = rmsnorm(h, norm_mix_g[l])
        proj = u @ w_in[l]
        cq, ckv, kr, xr, gt = jnp.split(proj, SPLITS, axis=-1)
        att = mla_mixer(cq, ckv, kr, cos, sin, q_norm_g[l], w_uq[l], kv_norm_g[l], w_ukv[l],
                        q_head_g[l], k_head_g[l])
        rnn = rglru_mixer(xr, gt, conv_w[l], conv_b[l], w_a[l], b_a[l], w_i[l], b_i[l], lru_lambda[l])
        mix = jnp.concatenate([rmsnorm(att, attn_out_g[l]), rmsnorm(rnn, rnn_out_g[l])], axis=-1)
        h = h + mix @ w_out[l]
        u = rmsnorm(h, norm_ffn_g[l])
        if l % 2 == 0:
            j = l // 2
            h = h + swiglu(u, ffn_w_gate[j], ffn_w_up[j], ffn_w_down[j])
        else:
            j = l // 2
            h = h + moe_swiglu(u, router_w[j], moe_w_gate[j], moe_w_up[j], moe_w_down[j])
    return h[:, N_META:]
```

```python
import functools
import math

import jax
import jax.numpy as jnp
from jax import lax
from jax.experimental import pallas as pl
from jax.experimental.pallas import tpu as pltpu

F32 = jnp.float32
BF16 = jnp.bfloat16

D_MODEL = 1024
N_META = 16
D_ATTN = D_MODEL // 2
D_RNN = D_MODEL - D_ATTN
N_HEADS = 8
QK_NOPE = 64
QK_ROPE = 32
QK_HEAD = QK_NOPE + QK_ROPE
V_HEAD = D_ATTN // N_HEADS
Q_LORA = 384
KV_LORA = 256
ROPE_BASE = 10000.0
N_RNN_BLOCKS = 8
RNN_BLOCK = D_RNN // N_RNN_BLOCKS
CONV_W = 4
LRU_C = 8.0
N_EXPERTS = 8
TOP_K = 2
EPS = 1e-6

LANES = 128
SUBLANES = 8
D_PROJ = Q_LORA + KV_LORA + 2 * D_RNN + LANES
VMEM_LIMIT = 56 * 1024 * 1024
NEG_INF = -1e30


def _rms(x):
    return x * lax.rsqrt(jnp.mean(x * x, axis=-1, keepdims=True) + EPS)


def _full(shape):
    return pl.BlockSpec(shape, lambda *_: (0,) * len(shape))


def _rope(x, c, s1, s2):
    return x * c + pltpu.roll(x, LANES - QK_ROPE // 2, 1) * s1 + pltpu.roll(x, QK_ROPE // 2, 1) * s2


def _pre_kernel(h_ref, rc_ref, rs1_ref, rs2_ref, gmix_ref, win_ref, gq_ref, wuq_ref, gkv_ref,
                wukv_ref, gqh_ref, gkh_ref, q_ref, k_ref, v_ref, xr_ref, gt_ref):
    u = (_rms(h_ref[0]) * gmix_ref[...]).astype(BF16)
    proj = jnp.dot(u, win_ref[...], preferred_element_type=F32)
    o1 = Q_LORA
    o2 = o1 + KV_LORA
    o3 = o2 + D_RNN
    o4 = o3 + D_RNN
    cq, ckv, krs = proj[:, :o1], proj[:, o1:o2], proj[:, o4:]
    xr_ref[0] = proj[:, o2:o3].astype(BF16)
    gt_ref[0] = proj[:, o3:o4].astype(BF16)

    c, s1, s2 = rc_ref[0], rs1_ref[0], rs2_ref[0]
    inv_dh = 1.0 / QK_HEAD
    qf = jnp.dot((_rms(cq) * gq_ref[...]).astype(BF16), wuq_ref[...], preferred_element_type=F32)
    gqh = gqh_ref[...] * (QK_HEAD ** -0.5)
    for hh in range(N_HEADS):
        qh = qf[:, hh * LANES:(hh + 1) * LANES]
        inv = lax.rsqrt(jnp.sum(qh * qh, axis=-1, keepdims=True) * inv_dh + EPS)
        q_ref[0, hh] = _rope(qh * inv * gqh, c, s1, s2).astype(BF16)

    kvf = jnp.dot((_rms(ckv) * gkv_ref[...]).astype(BF16), wukv_ref[...], preferred_element_type=F32)
    gkh = gkh_ref[...]
    kr_ss = jnp.sum(krs * krs, axis=-1, keepdims=True)
    kr_rot = _rope(krs * gkh, c, s1, s2)
    v_off = N_HEADS * LANES
    for hh in range(N_HEADS):
        kh = kvf[:, hh * LANES:(hh + 1) * LANES]
        inv = lax.rsqrt((jnp.sum(kh * kh, axis=-1, keepdims=True) + kr_ss) * inv_dh + EPS)
        k_ref[0, hh] = ((kh * gkh + kr_rot) * inv).astype(BF16)
        v_ref[0, hh] = kvf[:, v_off + hh * V_HEAD:v_off + (hh + 1) * V_HEAD].astype(BF16)


def _pre_mixer(h, rope_c, rope_s1, rope_s2, lw, ta):
    b, tp, d = h.shape
    row = lambda n: pl.BlockSpec((1, ta, n), lambda bi, ti: (bi, ti, 0))
    head = lambda n: pl.BlockSpec((1, N_HEADS, ta, n), lambda bi, ti: (bi, 0, ti, 0))
    return pl.pallas_call(
        _pre_kernel,
        grid=(b, tp // ta),
        in_specs=[row(d), row(LANES), row(LANES), row(LANES),
                  _full((1, d)), _full((d, D_PROJ)),
                  _full((1, Q_LORA)), _full((Q_LORA, N_HEADS * LANES)),
                  _full((1, KV_LORA)), _full((KV_LORA, N_HEADS * (LANES + V_HEAD))),
                  _full((1, LANES)), _full((1, LANES))],
        out_specs=[head(LANES), head(LANES), head(V_HEAD), row(D_RNN), row(D_RNN)],
        out_shape=[jax.ShapeDtypeStruct((b, N_HEADS, tp, LANES), BF16),
                   jax.ShapeDtypeStruct((b, N_HEADS, tp, LANES), BF16),
                   jax.ShapeDtypeStruct((b, N_HEADS, tp, V_HEAD), BF16),
                   jax.ShapeDtypeStruct((b, tp, D_RNN), BF16),
                   jax.ShapeDtypeStruct((b, tp, D_RNN), BF16)],
        compiler_params=pltpu.CompilerParams(
            dimension_semantics=("parallel", "parallel"), vmem_limit_bytes=VMEM_LIMIT),
        name="pre_mixer",
    )(h, rope_c, rope_s1, rope_s2, lw["g_mix"], lw["w_in"], lw["g_q"], lw["w_uq"], lw["g_kv"],
      lw["w_ukv"], lw["g_qh"], lw["g_kh"])


def _attn_kernel(q_ref, k_ref, v_ref, o_ref, m_sc, l_sc, acc_sc, *, tq):
    qi = pl.program_id(1)
    kj = pl.program_id(2)

    @pl.when(kj == 0)
    def _():
        m_sc[...] = jnp.full(m_sc.shape, NEG_INF, F32)
        l_sc[...] = jnp.zeros(l_sc.shape, F32)
        acc_sc[...] = jnp.zeros(acc_sc.shape, F32)

    def step(diagonal):
        if diagonal:
            keep = (lax.broadcasted_iota(jnp.int32, (tq, tq), 1)
                    <= lax.broadcasted_iota(jnp.int32, (tq, tq), 0))
        for hh in range(N_HEADS):
            s = lax.dot_general(q_ref[0, hh], k_ref[0, hh], (((1,), (1,)), ((), ())),
                                preferred_element_type=F32)
            if diagonal:
                s = jnp.where(keep, s, NEG_INF)
            m_prev = m_sc[hh]
            m_new = jnp.maximum(m_prev, jnp.max(s, axis=-1, keepdims=True))
            alpha = jnp.exp(m_prev - m_new)
            p = jnp.exp(s - m_new)
            l_sc[hh] = alpha * l_sc[hh] + jnp.sum(p, axis=-1, keepdims=True)
            acc_sc[hh] = alpha * acc_sc[hh] + jnp.dot(p.astype(BF16), v_ref[0, hh],
                                                      preferred_element_type=F32)
            m_sc[hh] = m_new

    @pl.when(kj < qi)
    def _():
        step(False)

    @pl.when(kj == qi)
    def _():
        step(True)
        outs = [acc_sc[hh] / l_sc[hh] for hh in range(N_HEADS)]
        o_ref[0] = jnp.concatenate(outs, axis=-1).astype(BF16)


def _attention(q, k, v, tq):
    b, nh, tp, _ = q.shape
    nt = tp // tq
    kv_idx = lambda bi, qi, kj: (bi, 0, jnp.minimum(kj, qi), 0)
    return pl.pallas_call(
        functools.partial(_attn_kernel, tq=tq),
        grid=(b, nt, nt),
        in_specs=[pl.BlockSpec((1, nh, tq, LANES), lambda bi, qi, kj: (bi, 0, qi, 0)),
                  pl.BlockSpec((1, nh, tq, LANES), kv_idx),
                  pl.BlockSpec((1, nh, tq, V_HEAD), kv_idx)],
        out_specs=pl.BlockSpec((1, tq, D_ATTN), lambda bi, qi, kj: (bi, qi, 0)),
        out_shape=jax.ShapeDtypeStruct((b, tp, D_ATTN), BF16),
        scratch_shapes=[pltpu.VMEM((nh, tq, 1), F32), pltpu.VMEM((nh, tq, 1), F32),
                        pltpu.VMEM((nh, tq, V_HEAD), F32)],
        compiler_params=pltpu.CompilerParams(
            dimension_semantics=("parallel", "parallel", "arbitrary"),
            vmem_limit_bytes=VMEM_LIMIT),
        name="attention",
    )(q, k, v)


def _rnn_kernel(xr_ref, gt_ref, cw_ref, cb_ref, wai_ref, bai_ref, lam_ref, o_ref,
                a_sc, b_sc, h_sc, tail_sc, *, tt, nb):
    ng = a_sc.shape[0]

    @pl.when(pl.program_id(0) == 0)
    def _():
        h_sc[...] = jnp.zeros(h_sc.shape, F32)
        tail_sc[...] = jnp.zeros(tail_sc.shape, F32)

    lam = lam_ref[...]
    log_a_unit = -LRU_C * (jnp.maximum(-lam, 0.0) + jnp.log1p(jnp.exp(-jnp.abs(lam))))
    cw = cw_ref[...]
    for bi in range(nb):
        x = xr_ref[bi].astype(F32)
        xe = jnp.concatenate([tail_sc[bi], x], axis=0)
        xc = cb_ref[...] + x * cw[CONV_W - 1:CONV_W]
        for j in range(CONV_W - 1):
            sh = CONV_W - 1 - j
            xc = xc + xe[SUBLANES - sh:SUBLANES - sh + tt] * cw[j:j + 1]
        tail_sc[bi] = x[tt - SUBLANES:]
        gates = jnp.dot(xc.astype(BF16), wai_ref[...], preferred_element_type=F32) + bai_ref[...]
        r = jax.nn.sigmoid(gates[:, :D_RNN])
        ig = jax.nn.sigmoid(gates[:, D_RNN:])
        log_a = r * log_a_unit
        a = jnp.exp(log_a)
        mult = jnp.sqrt(1.0 - a * a)
        bv = mult * ig * xc
        for g in range(ng):
            a_sc[g, pl.ds(bi * tt, tt), :] = a[:, g * LANES:(g + 1) * LANES]
            b_sc[g, pl.ds(bi * tt, tt), :] = bv[:, g * LANES:(g + 1) * LANES]

    def scan_step(t, hs):
        rows = pl.ds(t, nb, stride=tt)
        new = []
        for g in range(ng):
            hg = a_sc[g, rows, :] * hs[g] + b_sc[g, rows, :]
            b_sc[g, rows, :] = hg
            new.append(hg)
        return tuple(new)

    h_last = lax.fori_loop(0, tt, scan_step, tuple(h_sc[g] for g in range(ng)), unroll=8)
    for g in range(ng):
        h_sc[g] = h_last[g]

    for bi in range(nb):
        hs = jnp.concatenate([b_sc[g, pl.ds(bi * tt, tt), :] for g in range(ng)], axis=-1)
        o_ref[bi] = (hs * jax.nn.gelu(gt_ref[bi].astype(F32), approximate=True)).astype(BF16)


def _rglru(xr, gt, lw, tt):
    b, tp, c = xr.shape
    blk = pl.BlockSpec((b, tt, c), lambda ti: (0, ti, 0))
    return pl.pallas_call(
        functools.partial(_rnn_kernel, tt=tt, nb=b),
        grid=(tp // tt,),
        in_specs=[blk, blk, _full((CONV_W, c)), _full((1, c)), _full((c, 2 * c)),
                  _full((1, 2 * c)), _full((1, c))],
        out_specs=blk,
        out_shape=jax.ShapeDtypeStruct((b, tp, c), BF16),
        scratch_shapes=[pltpu.VMEM((c // LANES, b * tt, LANES), F32),
                        pltpu.VMEM((c // LANES, b * tt, LANES), F32),
                        pltpu.VMEM((c // LANES, b, LANES), F32),
                        pltpu.VMEM((b, SUBLANES, c), F32)],
        compiler_params=pltpu.CompilerParams(
            dimension_semantics=("arbitrary",), vmem_limit_bytes=VMEM_LIMIT),
        name="rglru",
    )(xr, gt, lw["conv_w"], lw["conv_b"], lw["w_ai"], lw["b_ai"], lw["lam"])


def _post_kernel(att_ref, rnn_ref, h_ref, ga_ref, gr_ref, wo_ref, gf_ref, *rest, with_router):
    if with_router:
        wr_ref, ho_ref, u_ref, lg_ref = rest
    else:
        ho_ref, u_ref = rest
    an = (_rms(att_ref[...].astype(F32)) * ga_ref[...]).astype(BF16)
    rn = (_rms(rnn_ref[...].astype(F32)) * gr_ref[...]).astype(BF16)
    mix = jnp.concatenate([an, rn], axis=-1)
    hn = h_ref[...] + jnp.dot(mix, wo_ref[...], preferred_element_type=F32)
    ho_ref[...] = hn
    u = _rms(hn) * gf_ref[...]
    u_ref[...] = u.astype(BF16)
    if with_router:
        lg_ref[...] = jnp.dot(u, wr_ref[...], preferred_element_type=F32,
                              precision=lax.Precision.HIGHEST)


def _post_mixer(att, rnn, h, lw, tm, w_router=None):
    n, d = h.shape
    with_router = w_router is not None
    row = lambda c: pl.BlockSpec((tm, c), lambda i: (i, 0))
    in_specs = [row(D_ATTN), row(D_RNN), row(d), _full((1, D_ATTN)), _full((1, D_RNN)),
                _full((d, d)), _full((1, d))]
    out_specs = [row(d), row(d)]
    out_shape = [jax.ShapeDtypeStruct((n, d), F32), jax.ShapeDtypeStruct((n, d), BF16)]
    args = [att, rnn, h, lw["g_att"], lw["g_rnn"], lw["w_out"], lw["g_ffn"]]
    if with_router:
        in_specs.append(_full((d, LANES)))
        out_specs.append(row(LANES))
        out_shape.append(jax.ShapeDtypeStruct((n, LANES), F32))
        args.append(w_router)
    return pl.pallas_call(
        functools.partial(_post_kernel, with_router=with_router),
        grid=(n // tm,),
        in_specs=in_specs, out_specs=out_specs, out_shape=out_shape,
        compiler_params=pltpu.CompilerParams(
            dimension_semantics=("parallel",), vmem_limit_bytes=VMEM_LIMIT),
        name="post_mixer",
    )(*args)


def _swiglu_tile(u, wg_ref, wu_ref, wd_ref, act_sc, fc):
    f = act_sc.shape[-1]
    for c0 in range(0, f, fc):
        g = jnp.dot(u, wg_ref[:, c0:c0 + fc], preferred_element_type=F32)
        up = jnp.dot(u, wu_ref[:, c0:c0 + fc], preferred_element_type=F32)
        act_sc[:, c0:c0 + fc] = (jax.nn.silu(g) * up).astype(BF16)
    return jnp.dot(act_sc[...], wd_ref[...], preferred_element_type=F32)


def _ffn_kernel(u_ref, h_ref, wg_ref, wu_ref, wd_ref, o_ref, act_sc, *, fc):
    o_ref[...] = h_ref[...] + _swiglu_tile(u_ref[...], wg_ref, wu_ref, wd_ref, act_sc, fc)


def _chunk(f):
    for fc in (512, 256, 128):
        if f % fc == 0:
            return fc
    return f


def _dense_ffn(u, h, wg, wu, wd, tm):
    n, d = h.shape
    f = wg.shape[1]
    row = lambda dt: pl.BlockSpec((tm, d), lambda i: (i, 0))
    once = pl.Buffered(1)
    return pl.pallas_call(
        functools.partial(_ffn_kernel, fc=_chunk(f)),
        grid=(n // tm,),
        in_specs=[row(BF16), row(F32),
                  pl.BlockSpec((d, f), lambda i: (0, 0), pipeline_mode=once),
                  pl.BlockSpec((d, f), lambda i: (0, 0), pipeline_mode=once),
                  pl.BlockSpec((f, d), lambda i: (0, 0), pipeline_mode=once)],
        out_specs=row(F32),
        out_shape=jax.ShapeDtypeStruct((n, d), F32),
        scratch_shapes=[pltpu.VMEM((tm, f), BF16)],
        compiler_params=pltpu.CompilerParams(
            dimension_semantics=("parallel",), vmem_limit_bytes=VMEM_LIMIT),
        name="dense_ffn",
    )(u, h, wg, wu, wd)


def _moe_kernel(te_ref, nt_ref, x_ref, gate_ref, wg_ref, wu_ref, wd_ref, o_ref, act_sc, *, fc):
    @pl.when(pl.program_id(0) < nt_ref[0])
    def _():
        y = _swiglu_tile(x_ref[...], wg_ref.at[0], wu_ref.at[0], wd_ref.at[0], act_sc, fc)
        o_ref[...] = y * gate_ref[...]

    @pl.when(pl.program_id(0) >= nt_ref[0])
    def _():
        o_ref[...] = jnp.zeros(o_ref.shape, F32)


def _grouped_ffn(tile_expert, n_tiles, xs, gates, wg, wu, wd, tm):
    a, d = xs.shape
    f = wg.shape[2]
    grid_spec = pltpu.PrefetchScalarGridSpec(
        num_scalar_prefetch=2,
        grid=(a // tm,),
        in_specs=[pl.BlockSpec((tm, d), lambda i, te, nt: (i, 0)),
                  pl.BlockSpec((tm, 1), lambda i, te, nt: (i, 0)),
                  pl.BlockSpec((1, d, f), lambda i, te, nt: (te[i], 0, 0)),
                  pl.BlockSpec((1, d, f), lambda i, te, nt: (te[i], 0, 0)),
                  pl.BlockSpec((1, f, d), lambda i, te, nt: (te[i], 0, 0))],
        out_specs=pl.BlockSpec((tm, d), lambda i, te, nt: (i, 0)),
        scratch_shapes=[pltpu.VMEM((tm, f), BF16)],
    )
    return pl.pallas_call(
        functools.partial(_moe_kernel, fc=_chunk(f)),
        grid_spec=grid_spec,
        out_shape=jax.ShapeDtypeStruct((a, d), F32),
        compiler_params=pltpu.CompilerParams(
            dimension_semantics=("arbitrary",), vmem_limit_bytes=VMEM_LIMIT),
        name="moe_ffn",
    )(tile_expert, n_tiles, xs, gates, wg, wu, wd)


def _moe_layer(u, h, logits, wg, wu, wd, tm):
    n, d = h.shape
    top_val, top_idx = lax.top_k(logits[:, :N_EXPERTS], TOP_K)
    top_w = jax.nn.softmax(top_val, axis=-1)
    e_flat = top_idx.reshape(-1)
    onehot = (e_flat[:, None] == jnp.arange(N_EXPERTS)[None, :]).astype(jnp.int32)
    rank = jnp.sum((jnp.cumsum(onehot, axis=0) - onehot) * onehot, axis=-1)
    counts = jnp.sum(onehot, axis=0)
    padded = ((counts + tm - 1) // tm) * tm
    ends = jnp.cumsum(padded)
    starts = ends - padded
    pos = starts[e_flat] + rank
    a_pad = (TOP_K * n // tm + N_EXPERTS) * tm
    tok = jnp.repeat(jnp.arange(n, dtype=jnp.int32), TOP_K)
    slot_tok = jnp.zeros((a_pad,), jnp.int32).at[pos].set(tok)
    slot_gate = jnp.zeros((a_pad,), F32).at[pos].set(top_w.reshape(-1))
    tile_start = jnp.arange(a_pad // tm, dtype=jnp.int32) * tm
    tile_expert = jnp.minimum(jnp.searchsorted(ends, tile_start, side="right"),
                              N_EXPERTS - 1).astype(jnp.int32)
    n_tiles = (ends[-1] // tm).astype(jnp.int32).reshape(1)
    xs = jnp.take(u, slot_tok, axis=0)
    y = _grouped_ffn(tile_expert, n_tiles, xs, slot_gate[:, None], wg, wu, wd, tm)
    pos2 = pos.reshape(n, TOP_K)
    return h + jnp.take(y, pos2[:, 0], axis=0) + jnp.take(y, pos2[:, 1], axis=0)


def _pad_head_cols(w, width):
    kdim = w.shape[0]
    w = w.reshape(kdim, N_HEADS, width)
    return jnp.pad(w, ((0, 0), (0, 0), (0, LANES - width))).reshape(kdim, N_HEADS * LANES)


def _block_diag(w):
    nb, c, _ = w.shape
    eye = jnp.eye(nb, dtype=w.dtype)
    return (eye[:, None, :, None] * w[:, :, None, :]).reshape(nb * c, nb * c)


def _layer_weights(l, p):
    w_in = p["w_in"][l]
    s1 = Q_LORA
    s2 = s1 + KV_LORA
    s3 = s2 + QK_ROPE
    s4 = s3 + D_RNN
    kr_slab = jnp.pad(w_in[:, s2:s3], ((0, 0), (QK_NOPE, LANES - QK_HEAD)))
    w_cat = jnp.concatenate([w_in[:, :s2], w_in[:, s3:s4], w_in[:, s4:], kr_slab], axis=1)
    w_ukv = p["w_ukv"][l].reshape(KV_LORA, N_HEADS, QK_NOPE + V_HEAD)
    w_uk = _pad_head_cols(w_ukv[:, :, :QK_NOPE].reshape(KV_LORA, N_HEADS * QK_NOPE), QK_NOPE)
    w_uv = w_ukv[:, :, QK_NOPE:].reshape(KV_LORA, N_HEADS * V_HEAD)
    pad_gain = lambda g: jnp.pad(g, (0, LANES - QK_HEAD)).reshape(1, LANES)
    return {
        "g_mix": p["norm_mix_g"][l].reshape(1, -1),
        "w_in": w_cat.astype(BF16),
        "g_q": p["q_norm_g"][l].reshape(1, -1),
        "w_uq": _pad_head_cols(p["w_uq"][l], QK_HEAD).astype(BF16),
        "g_kv": p["kv_norm_g"][l].reshape(1, -1),
        "w_ukv": jnp.concatenate([w_uk, w_uv], axis=1).astype(BF16),
        "g_qh": pad_gain(p["q_head_g"][l]),
        "g_kh": pad_gain(p["k_head_g"][l]),
        "conv_w": p["conv_w"][l],
        "conv_b": p["conv_b"][l].reshape(1, -1),
        "w_ai": jnp.concatenate([_block_diag(p["w_a"][l]), _block_diag(p["w_i"][l])],
                                axis=1).astype(BF16),
        "b_ai": jnp.concatenate([p["b_a"][l], p["b_i"][l]]).reshape(1, -1),
        "lam": p["lru_lambda"][l].reshape(1, -1),
        "g_att": p["attn_out_g"][l].reshape(1, -1),
        "g_rnn": p["rnn_out_g"][l].reshape(1, -1),
        "w_out": p["w_out"][l].astype(BF16),
        "g_ffn": p["norm_ffn_g"][l].reshape(1, -1),
    }


def _rope_tables(pos_full):
    inv_freq = ROPE_BASE ** (-jnp.arange(0, QK_ROPE, 2, dtype=F32) / QK_ROPE)
    ang = pos_full.astype(F32)[..., None] * inv_freq
    cos, sin = jnp.cos(ang), jnp.sin(ang)
    shape = cos.shape[:-1]
    ones = jnp.ones(shape + (QK_NOPE,), F32)
    z16 = jnp.zeros(shape + (QK_ROPE // 2,), F32)
    z32 = jnp.zeros(shape + (LANES - QK_HEAD,), F32)
    z64 = jnp.zeros(shape + (QK_NOPE,), F32)
    rope_c = jnp.concatenate([ones, cos, cos, z32], axis=-1)
    rope_s1 = jnp.concatenate([z64, -sin, z16, z32], axis=-1)
    rope_s2 = jnp.concatenate([z64, z16, sin, z32], axis=-1)
    return rope_c, rope_s1, rope_s2


def _pick_tiles(t, b):
    if t > 1024:
        tq = 384
    else:
        tq = 128
    tp = -(-t // tq) * tq
    n = b * tp
    tm = tq
    for cand in (1024, 768, 512):
        if n % cand == 0:
            tm = cand
            break
    return tp, tq, tm


def kernel(x, positions, meta_tokens, norm_mix_g, w_in, q_norm_g, w_uq, kv_norm_g, w_ukv, q_head_g, k_head_g, conv_w, conv_b, w_a, b_a, w_i, b_i, lru_lambda, attn_out_g, rnn_out_g, w_out, norm_ffn_g, ffn_w_gate, ffn_w_up, ffn_w_down, router_w, moe_w_gate, moe_w_up, moe_w_down):
    p = dict(norm_mix_g=norm_mix_g, w_in=w_in, q_norm_g=q_norm_g, w_uq=w_uq, kv_norm_g=kv_norm_g,
             w_ukv=w_ukv, q_head_g=q_head_g, k_head_g=k_head_g, conv_w=conv_w, conv_b=conv_b,
             w_a=w_a, b_a=b_a, w_i=w_i, b_i=b_i, lru_lambda=lru_lambda, attn_out_g=attn_out_g,
             rnn_out_g=rnn_out_g, w_out=w_out, norm_ffn_g=norm_ffn_g)
    b, seq, d = x.shape
    depth = w_in.shape[0]
    t = N_META + seq
    tp, tq, tm = _pick_tiles(t, b)
    n = b * tp

    meta = jnp.broadcast_to(meta_tokens[None].astype(x.dtype), (b, N_META, d))
    h = jnp.concatenate([meta, x, jnp.zeros((b, tp - t, d), x.dtype)], axis=1)
    meta_pos = jnp.broadcast_to(jnp.arange(N_META, dtype=jnp.int32)[None, :], (b, N_META))
    pos_full = jnp.concatenate([meta_pos, positions.astype(jnp.int32) + N_META,
                                jnp.zeros((b, tp - t), jnp.int32)], axis=1)
    rope_c, rope_s1, rope_s2 = _rope_tables(pos_full)

    for l in range(depth):
        lw = _layer_weights(l, p)
        q, k, v, xr, gt = _pre_mixer(h, rope_c, rope_s1, rope_s2, lw, tq)
        att = _attention(q, k, v, tq)
        rnn = _rglru(xr, gt, lw, tq)
        j = l // 2
        h2 = h.reshape(n, d)
        if l % 2 == 0:
            h2, u = _post_mixer(att.reshape(n, D_ATTN), rnn.reshape(n, D_RNN), h2, lw, tm)
            h2 = _dense_ffn(u, h2, ffn_w_gate[j].astype(BF16), ffn_w_up[j].astype(BF16),
                            ffn_w_down[j].astype(BF16), tm)
        else:
            w_r = jnp.pad(router_w[j], ((0, 0), (0, LANES - N_EXPERTS)))
            h2, u, logits = _post_mixer(att.reshape(n, D_ATTN), rnn.reshape(n, D_RNN), h2, lw,
                                        tm, w_router=w_r)
            h2 = _moe_layer(u, h2, logits, moe_w_gate[j].astype(BF16), moe_w_up[j].astype(BF16),
                            moe_w_down[j].astype(BF16), min(tm, 512))
        h = h2.reshape(b, tp, d)
    return h[:, N_META:t]
```

```python
import functools
import math

import jax
import jax.numpy as jnp
from jax import lax
from jax.experimental import pallas as pl
from jax.experimental.pallas import tpu as pltpu

F32 = jnp.float32
BF16 = jnp.bfloat16

D_MODEL = 1024
N_META = 16
D_ATTN = D_MODEL // 2
D_RNN = D_MODEL - D_ATTN
N_HEADS = 8
QK_NOPE = 64
QK_ROPE = 32
QK_HEAD = QK_NOPE + QK_ROPE
V_HEAD = D_ATTN // N_HEADS
Q_LORA = 384
KV_LORA = 256
ROPE_BASE = 10000.0
N_RNN_BLOCKS = 8
RNN_BLOCK = D_RNN // N_RNN_BLOCKS
CONV_W = 4
LRU_C = 8.0
N_EXPERTS = 8
TOP_K = 2
EPS = 1e-6

LANES = 128
SUBLANES = 8
BF16_ROWS = 16
V_ROWS = V_HEAD + BF16_ROWS
D_PROJ = Q_LORA + KV_LORA + 2 * D_RNN + LANES
VMEM_LIMIT = 56 * 1024 * 1024
NEG_INF = -1e30


def _rms(x):
    return x * lax.rsqrt(jnp.mean(x * x, axis=-1, keepdims=True) + EPS)


def _full(shape):
    return pl.BlockSpec(shape, lambda *_: (0,) * len(shape))


def _rope(x, c, s1, s2):
    return x * c + pltpu.roll(x, LANES - QK_ROPE // 2, 1) * s1 + pltpu.roll(x, QK_ROPE // 2, 1) * s2


def _pre_kernel(h_ref, rc_ref, rs1_ref, rs2_ref, gmix_ref, win_ref, gq_ref, wuq_ref, gkv_ref,
                wuk_ref, wuvt_ref, gqh_ref, gkh_ref, q_ref, k_ref, v_ref, xr_ref, gt_ref):
    u = (_rms(h_ref[0]) * gmix_ref[...]).astype(BF16)
    proj = jnp.dot(u, win_ref[...], preferred_element_type=F32)
    o1 = Q_LORA
    o2 = o1 + KV_LORA
    o3 = o2 + D_RNN
    o4 = o3 + D_RNN
    cq, ckv, krs = proj[:, :o1], proj[:, o1:o2], proj[:, o4:]
    xr_ref[0] = proj[:, o2:o3].astype(BF16)
    gt_ref[0] = proj[:, o3:o4].astype(BF16)

    c, s1, s2 = rc_ref[0], rs1_ref[0], rs2_ref[0]
    inv_dh = 1.0 / QK_HEAD
    qf = jnp.dot((_rms(cq) * gq_ref[...]).astype(BF16), wuq_ref[...], preferred_element_type=F32)
    gqh = gqh_ref[...] * (QK_HEAD ** -0.5 * math.log2(math.e))
    for hh in range(N_HEADS):
        qh = qf[:, hh * LANES:(hh + 1) * LANES]
        inv = lax.rsqrt(jnp.sum(qh * qh, axis=-1, keepdims=True) * inv_dh + EPS)
        q_ref[0, hh] = _rope(qh * inv * gqh, c, s1, s2).astype(BF16)

    ckvn = (_rms(ckv) * gkv_ref[...]).astype(BF16)
    kf = jnp.dot(ckvn, wuk_ref[...], preferred_element_type=F32)
    vt = lax.dot_general(wuvt_ref[...], ckvn, (((1,), (1,)), ((), ())), preferred_element_type=F32)
    gkh = gkh_ref[...]
    kr_ss = jnp.sum(krs * krs, axis=-1, keepdims=True)
    kr_rot = _rope(krs * gkh, c, s1, s2)
    ones = jnp.ones((V_ROWS - V_HEAD, vt.shape[1]), BF16)
    for hh in range(N_HEADS):
        kh = kf[:, hh * LANES:(hh + 1) * LANES]
        inv = lax.rsqrt((jnp.sum(kh * kh, axis=-1, keepdims=True) + kr_ss) * inv_dh + EPS)
        k_ref[0, hh] = ((kh * gkh + kr_rot) * inv).astype(BF16)
        v_ref[0, hh, :V_HEAD, :] = vt[hh * V_HEAD:(hh + 1) * V_HEAD].astype(BF16)
        v_ref[0, hh, V_HEAD:, :] = ones


def _pre_mixer(h, rope_c, rope_s1, rope_s2, lw, ta):
    b, tp, d = h.shape
    row = lambda n: pl.BlockSpec((1, ta, n), lambda bi, ti: (bi, ti, 0))
    head = lambda n: pl.BlockSpec((1, N_HEADS, ta, n), lambda bi, ti: (bi, 0, ti, 0))
    return pl.pallas_call(
        _pre_kernel,
        grid=(b, tp // ta),
        in_specs=[row(d), row(LANES), row(LANES), row(LANES),
                  _full((1, d)), _full((d, D_PROJ)),
                  _full((1, Q_LORA)), _full((Q_LORA, N_HEADS * LANES)),
                  _full((1, KV_LORA)), _full((KV_LORA, N_HEADS * LANES)),
                  _full((N_HEADS * V_HEAD, KV_LORA)),
                  _full((1, LANES)), _full((1, LANES))],
        out_specs=[head(LANES), head(LANES),
                   pl.BlockSpec((1, N_HEADS, V_ROWS, ta), lambda bi, ti: (bi, 0, 0, ti)),
                   row(D_RNN), row(D_RNN)],
        out_shape=[jax.ShapeDtypeStruct((b, N_HEADS, tp, LANES), BF16),
                   jax.ShapeDtypeStruct((b, N_HEADS, tp, LANES), BF16),
                   jax.ShapeDtypeStruct((b, N_HEADS, V_ROWS, tp), BF16),
                   jax.ShapeDtypeStruct((b, tp, D_RNN), BF16),
                   jax.ShapeDtypeStruct((b, tp, D_RNN), BF16)],
        compiler_params=pltpu.CompilerParams(
            dimension_semantics=("parallel", "parallel"), vmem_limit_bytes=VMEM_LIMIT),
        name="pre_mixer",
    )(h, rope_c, rope_s1, rope_s2, lw["g_mix"], lw["w_in"], lw["g_q"], lw["w_uq"], lw["g_kv"],
      lw["w_uk"], lw["w_uvt"], lw["g_qh"], lw["g_kh"])


def _attn_kernel(q_ref, k_ref, vt_ref, o_ref, m_sc, acc_sc, *, tq):
    qi = pl.program_id(1)
    kj = pl.program_id(2)

    @pl.when(kj == 0)
    def _():
        m_sc[...] = jnp.full(m_sc.shape, NEG_INF, F32)
        acc_sc[...] = jnp.zeros(acc_sc.shape, F32)

    def step(diagonal):
        if diagonal:
            keep = (lax.broadcasted_iota(jnp.int32, (tq, tq), 0)
                    <= lax.broadcasted_iota(jnp.int32, (tq, tq), 1))
        sts = [lax.dot_general(k_ref[0, hh], q_ref[0, hh], (((1,), (1,)), ((), ())),
                               preferred_element_type=F32)
               for hh in range(N_HEADS)]
        ps, alphas = [], []
        for hh in range(N_HEADS):
            st = jnp.where(keep, sts[hh], NEG_INF) if diagonal else sts[hh]
            m_prev = m_sc[hh]
            m_new = jnp.maximum(m_prev, jnp.max(st, axis=0, keepdims=True))
            alphas.append(jnp.exp2(m_prev - m_new))
            ps.append(jnp.exp2(st - m_new).astype(BF16))
            m_sc[hh] = m_new
        for hh in range(N_HEADS):
            acc_sc[hh] = alphas[hh] * acc_sc[hh] + jnp.dot(vt_ref[0, hh], ps[hh],
                                                           preferred_element_type=F32)

    @pl.when(kj < qi)
    def _():
        step(False)

    @pl.when(kj == qi)
    def _():
        step(True)
        outs = []
        for hh in range(N_HEADS):
            acc = acc_sc[hh]
            outs.append(acc[:V_HEAD] / acc[V_HEAD:V_HEAD + 1])
        o_ref[0] = jnp.concatenate(outs, axis=0).T.astype(BF16)


def _attention(q, k, vt, tq):
    b, nh, tp, _ = q.shape
    nt = tp // tq
    vrows = vt.shape[2]
    return pl.pallas_call(
        functools.partial(_attn_kernel, tq=tq),
        grid=(b, nt, nt),
        in_specs=[pl.BlockSpec((1, nh, tq, LANES), lambda bi, qi, kj: (bi, 0, qi, 0)),
                  pl.BlockSpec((1, nh, tq, LANES),
                               lambda bi, qi, kj: (bi, 0, jnp.minimum(kj, qi), 0)),
                  pl.BlockSpec((1, nh, vrows, tq),
                               lambda bi, qi, kj: (bi, 0, 0, jnp.minimum(kj, qi)))],
        out_specs=pl.BlockSpec((1, tq, D_ATTN), lambda bi, qi, kj: (bi, qi, 0)),
        out_shape=jax.ShapeDtypeStruct((b, tp, D_ATTN), BF16),
        scratch_shapes=[pltpu.VMEM((nh, 1, tq), F32), pltpu.VMEM((nh, vrows, tq), F32)],
        compiler_params=pltpu.CompilerParams(
            dimension_semantics=("parallel", "parallel", "arbitrary"),
            vmem_limit_bytes=VMEM_LIMIT),
        name="attention",
    )(q, k, vt)


def _rnn_kernel(xr_ref, gt_ref, cw_ref, cb_ref, wai_ref, bai_ref, lam_ref, o_ref,
                a_sc, b_sc, h_sc, tail_sc, *, tt, nb):
    ng = a_sc.shape[0]

    @pl.when(pl.program_id(0) == 0)
    def _():
        h_sc[...] = jnp.zeros(h_sc.shape, F32)
        tail_sc[...] = jnp.zeros(tail_sc.shape, F32)

    lam = lam_ref[...]
    log_a_unit = -LRU_C * (jnp.maximum(-lam, 0.0) + jnp.log1p(jnp.exp(-jnp.abs(lam))))
    cw = cw_ref[...]
    for bi in range(nb):
        x = xr_ref[bi].astype(F32)
        xe = jnp.concatenate([tail_sc[bi], x], axis=0)
        xc = cb_ref[...] + x * cw[CONV_W - 1:CONV_W]
        for j in range(CONV_W - 1):
            sh = CONV_W - 1 - j
            xc = xc + xe[SUBLANES - sh:SUBLANES - sh + tt] * cw[j:j + 1]
        tail_sc[bi] = x[tt - SUBLANES:]
        gates = jnp.dot(xc.astype(BF16), wai_ref[...], preferred_element_type=F32) + bai_ref[...]
        r = jax.nn.sigmoid(gates[:, :D_RNN])
        ig = jax.nn.sigmoid(gates[:, D_RNN:])
        log_a = r * log_a_unit
        a = jnp.exp(log_a)
        mult = jnp.sqrt(1.0 - a * a)
        bv = mult * ig * xc
        for g in range(ng):
            a_sc[g, pl.ds(bi * tt, tt), :] = a[:, g * LANES:(g + 1) * LANES]
            b_sc[g, pl.ds(bi * tt, tt), :] = bv[:, g * LANES:(g + 1) * LANES]

    def scan_step(t, hs):
        rows = pl.ds(t, nb, stride=tt)
        new = []
        for g in range(ng):
            hg = a_sc[g, rows, :] * hs[g] + b_sc[g, rows, :]
            b_sc[g, rows, :] = hg
            new.append(hg)
        return tuple(new)

    h_last = lax.fori_loop(0, tt, scan_step, tuple(h_sc[g] for g in range(ng)), unroll=8)
    for g in range(ng):
        h_sc[g] = h_last[g]

    for bi in range(nb):
        hs = jnp.concatenate([b_sc[g, pl.ds(bi * tt, tt), :] for g in range(ng)], axis=-1)
        o_ref[bi] = (hs * jax.nn.gelu(gt_ref[bi].astype(F32), approximate=True)).astype(BF16)


def _rglru(xr, gt, lw, tt):
    b, tp, c = xr.shape
    blk = pl.BlockSpec((b, tt, c), lambda ti: (0, ti, 0))
    return pl.pallas_call(
        functools.partial(_rnn_kernel, tt=tt, nb=b),
        grid=(tp // tt,),
        in_specs=[blk, blk, _full((CONV_W, c)), _full((1, c)), _full((c, 2 * c)),
                  _full((1, 2 * c)), _full((1, c))],
        out_specs=blk,
        out_shape=jax.ShapeDtypeStruct((b, tp, c), BF16),
        scratch_shapes=[pltpu.VMEM((c // LANES, b * tt, LANES), F32),
                        pltpu.VMEM((c // LANES, b * tt, LANES), F32),
                        pltpu.VMEM((c // LANES, b, LANES), F32),
                        pltpu.VMEM((b, SUBLANES, c), F32)],
        compiler_params=pltpu.CompilerParams(
            dimension_semantics=("arbitrary",), vmem_limit_bytes=VMEM_LIMIT),
        name="rglru",
    )(xr, gt, lw["conv_w"], lw["conv_b"], lw["w_ai"], lw["b_ai"], lw["lam"])


def _post_kernel(att_ref, rnn_ref, h_ref, ga_ref, gr_ref, wo_ref, gf_ref, *rest, with_router):
    if with_router:
        wr_ref, ho_ref, u_ref, lg_ref = rest
    else:
        ho_ref, u_ref = rest
    an = (_rms(att_ref[...].astype(F32)) * ga_ref[...]).astype(BF16)
    rn = (_rms(rnn_ref[...].astype(F32)) * gr_ref[...]).astype(BF16)
    mix = jnp.concatenate([an, rn], axis=-1)
    hn = h_ref[...] + jnp.dot(mix, wo_ref[...], preferred_element_type=F32)
    ho_ref[...] = hn
    u = _rms(hn) * gf_ref[...]
    u_ref[...] = u.astype(BF16)
    if with_router:
        w = wr_ref[...]
        u_hi = u.astype(BF16)
        w_hi = w.astype(BF16)
        u_lo = (u - u_hi.astype(F32)).astype(BF16)
        w_lo = (w - w_hi.astype(F32)).astype(BF16)
        lg_ref[...] = (jnp.dot(u_hi, w_hi, preferred_element_type=F32)
                       + jnp.dot(u_lo, w_hi, preferred_element_type=F32)
                       + jnp.dot(u_hi, w_lo, preferred_element_type=F32))


def _post_mixer(att, rnn, h, lw, tm, w_router=None):
    n, d = h.shape
    with_router = w_router is not None
    row = lambda c: pl.BlockSpec((tm, c), lambda i: (i, 0))
    in_specs = [row(D_ATTN), row(D_RNN), row(d), _full((1, D_ATTN)), _full((1, D_RNN)),
                _full((d, d)), _full((1, d))]
    out_specs = [row(d), row(d)]
    out_shape = [jax.ShapeDtypeStruct((n, d), F32), jax.ShapeDtypeStruct((n, d), BF16)]
    args = [att, rnn, h, lw["g_att"], lw["g_rnn"], lw["w_out"], lw["g_ffn"]]
    if with_router:
        in_specs.append(_full((d, LANES)))
        out_specs.append(row(LANES))
        out_shape.append(jax.ShapeDtypeStruct((n, LANES), F32))
        args.append(w_router)
    return pl.pallas_call(
        functools.partial(_post_kernel, with_router=with_router),
        grid=(n // tm,),
        in_specs=in_specs, out_specs=out_specs, out_shape=out_shape,
        compiler_params=pltpu.CompilerParams(
            dimension_semantics=("parallel",), vmem_limit_bytes=VMEM_LIMIT),
        name="post_mixer",
    )(*args)


def _swiglu_tile(u, wg_ref, wu_ref, wd_ref, act_sc, fc):
    f = act_sc.shape[-1]
    for c0 in range(0, f, fc):
        g = jnp.dot(u, wg_ref[:, c0:c0 + fc], preferred_element_type=F32)
        up = jnp.dot(u, wu_ref[:, c0:c0 + fc], preferred_element_type=F32)
        act_sc[:, c0:c0 + fc] = (jax.nn.silu(g) * up).astype(BF16)
    return jnp.dot(act_sc[...], wd_ref[...], preferred_element_type=F32)


def _ffn_kernel(u_ref, h_ref, wg_ref, wu_ref, wd_ref, o_ref, act_sc, *, fc):
    o_ref[...] = h_ref[...] + _swiglu_tile(u_ref[...], wg_ref, wu_ref, wd_ref, act_sc, fc)


def _chunk(f):
    for fc in (512, 256, 128):
        if f % fc == 0:
            return fc
    return f


def _dense_ffn(u, h, wg, wu, wd, tm):
    n, d = h.shape
    f = wg.shape[1]
    row = lambda dt: pl.BlockSpec((tm, d), lambda i: (i, 0))
    once = pl.Buffered(1)
    return pl.pallas_call(
        functools.partial(_ffn_kernel, fc=_chunk(f)),
        grid=(n // tm,),
        in_specs=[row(BF16), row(F32),
                  pl.BlockSpec((d, f), lambda i: (0, 0), pipeline_mode=once),
                  pl.BlockSpec((d, f), lambda i: (0, 0), pipeline_mode=once),
                  pl.BlockSpec((f, d), lambda i: (0, 0), pipeline_mode=once)],
        out_specs=row(F32),
        out_shape=jax.ShapeDtypeStruct((n, d), F32),
        scratch_shapes=[pltpu.VMEM((tm, f), BF16)],
        compiler_params=pltpu.CompilerParams(
            dimension_semantics=("parallel",), vmem_limit_bytes=VMEM_LIMIT),
        name="dense_ffn",
    )(u, h, wg, wu, wd)


def _moe_kernel(te_ref, nt_ref, x_ref, gate_ref, wg_ref, wu_ref, wd_ref, o_ref, act_sc, *, fc):
    @pl.when(pl.program_id(0) < nt_ref[0])
    def _():
        y = _swiglu_tile(x_ref[...], wg_ref.at[0], wu_ref.at[0], wd_ref.at[0], act_sc, fc)
        o_ref[...] = y * gate_ref[...]

    @pl.when(pl.program_id(0) >= nt_ref[0])
    def _():
        o_ref[...] = jnp.zeros(o_ref.shape, F32)


def _grouped_ffn(tile_expert, n_tiles, xs, gates, wg, wu, wd, tm):
    a, d = xs.shape
    f = wg.shape[2]
    grid_spec = pltpu.PrefetchScalarGridSpec(
        num_scalar_prefetch=2,
        grid=(a // tm,),
        in_specs=[pl.BlockSpec((tm, d), lambda i, te, nt: (i, 0)),
                  pl.BlockSpec((tm, 1), lambda i, te, nt: (i, 0)),
                  pl.BlockSpec((1, d, f), lambda i, te, nt: (te[i], 0, 0)),
                  pl.BlockSpec((1, d, f), lambda i, te, nt: (te[i], 0, 0)),
                  pl.BlockSpec((1, f, d), lambda i, te, nt: (te[i], 0, 0))],
        out_specs=pl.BlockSpec((tm, d), lambda i, te, nt: (i, 0)),
        scratch_shapes=[pltpu.VMEM((tm, f), BF16)],
    )
    return pl.pallas_call(
        functools.partial(_moe_kernel, fc=_chunk(f)),
        grid_spec=grid_spec,
        out_shape=jax.ShapeDtypeStruct((a, d), F32),
        compiler_params=pltpu.CompilerParams(
            dimension_semantics=("arbitrary",), vmem_limit_bytes=VMEM_LIMIT),
        name="moe_ffn",
    )(tile_expert, n_tiles, xs, gates, wg, wu, wd)


def _moe_layer(u, h, logits, wg, wu, wd, tm):
    n, d = h.shape
    top_val, top_idx = lax.top_k(logits[:, :N_EXPERTS], TOP_K)
    top_w = jax.nn.softmax(top_val, axis=-1)
    e_flat = top_idx.reshape(-1)
    onehot = (e_flat[:, None] == jnp.arange(N_EXPERTS)[None, :]).astype(jnp.int32)
    rank = jnp.sum((jnp.cumsum(onehot, axis=0) - onehot) * onehot, axis=-1)
    counts = jnp.sum(onehot, axis=0)
    padded = ((counts + tm - 1) // tm) * tm
    ends = jnp.cumsum(padded)
    starts = ends - padded
    pos = starts[e_flat] + rank
    a_pad = (TOP_K * n // tm + N_EXPERTS) * tm
    tok = jnp.repeat(jnp.arange(n, dtype=jnp.int32), TOP_K)
    slot_tok = jnp.zeros((a_pad,), jnp.int32).at[pos].set(tok)
    slot_gate = jnp.zeros((a_pad,), F32).at[pos].set(top_w.reshape(-1))
    tile_start = jnp.arange(a_pad // tm, dtype=jnp.int32) * tm
    tile_expert = jnp.minimum(jnp.searchsorted(ends, tile_start, side="right"),
                              N_EXPERTS - 1).astype(jnp.int32)
    n_tiles = (ends[-1] // tm).astype(jnp.int32).reshape(1)
    xs = jnp.take(u, slot_tok, axis=0)
    y = _grouped_ffn(tile_expert, n_tiles, xs, slot_gate[:, None], wg, wu, wd, tm)
    pos2 = pos.reshape(n, TOP_K)
    return h + jnp.take(y, pos2[:, 0], axis=0) + jnp.take(y, pos2[:, 1], axis=0)


def _pad_head_cols(w, width):
    kdim = w.shape[0]
    w = w.reshape(kdim, N_HEADS, width)
    return jnp.pad(w, ((0, 0), (0, 0), (0, LANES - width))).reshape(kdim, N_HEADS * LANES)


def _block_diag(w):
    nb, c, _ = w.shape
    eye = jnp.eye(nb, dtype=w.dtype)
    return (eye[:, None, :, None] * w[:, :, None, :]).reshape(nb * c, nb * c)


def _layer_weights(l, p):
    w_in = p["w_in"][l]
    s1 = Q_LORA
    s2 = s1 + KV_LORA
    s3 = s2 + QK_ROPE
    s4 = s3 + D_RNN
    kr_slab = jnp.pad(w_in[:, s2:s3], ((0, 0), (QK_NOPE, LANES - QK_HEAD)))
    w_cat = jnp.concatenate([w_in[:, :s2], w_in[:, s3:s4], w_in[:, s4:], kr_slab], axis=1)
    w_ukv = p["w_ukv"][l].reshape(KV_LORA, N_HEADS, QK_NOPE + V_HEAD)
    w_uk = _pad_head_cols(w_ukv[:, :, :QK_NOPE].reshape(KV_LORA, N_HEADS * QK_NOPE), QK_NOPE)
    w_uv = w_ukv[:, :, QK_NOPE:].reshape(KV_LORA, N_HEADS * V_HEAD)
    pad_gain = lambda g: jnp.pad(g, (0, LANES - QK_HEAD)).reshape(1, LANES)
    return {
        "g_mix": p["norm_mix_g"][l].reshape(1, -1),
        "w_in": w_cat.astype(BF16),
        "g_q": p["q_norm_g"][l].reshape(1, -1),
        "w_uq": _pad_head_cols(p["w_uq"][l], QK_HEAD).astype(BF16),
        "g_kv": p["kv_norm_g"][l].reshape(1, -1),
        "w_uk": w_uk.astype(BF16),
        "w_uvt": w_uv.T.astype(BF16),
        "g_qh": pad_gain(p["q_head_g"][l]),
        "g_kh": pad_gain(p["k_head_g"][l]),
        "conv_w": p["conv_w"][l],
        "conv_b": p["conv_b"][l].reshape(1, -1),
        "w_ai": jnp.concatenate([_block_diag(p["w_a"][l]), _block_diag(p["w_i"][l])],
                                axis=1).astype(BF16),
        "b_ai": jnp.concatenate([p["b_a"][l], p["b_i"][l]]).reshape(1, -1),
        "lam": p["lru_lambda"][l].reshape(1, -1),
        "g_att": p["attn_out_g"][l].reshape(1, -1),
        "g_rnn": p["rnn_out_g"][l].reshape(1, -1),
        "w_out": p["w_out"][l].astype(BF16),
        "g_ffn": p["norm_ffn_g"][l].reshape(1, -1),
    }


def _rope_tables(pos_full):
    inv_freq = ROPE_BASE ** (-jnp.arange(0, QK_ROPE, 2, dtype=F32) / QK_ROPE)
    ang = pos_full.astype(F32)[..., None] * inv_freq
    cos, sin = jnp.cos(ang), jnp.sin(ang)
    shape = cos.shape[:-1]
    ones = jnp.ones(shape + (QK_NOPE,), F32)
    z16 = jnp.zeros(shape + (QK_ROPE // 2,), F32)
    z32 = jnp.zeros(shape + (LANES - QK_HEAD,), F32)
    z64 = jnp.zeros(shape + (QK_NOPE,), F32)
    rope_c = jnp.concatenate([ones, cos, cos, z32], axis=-1)
    rope_s1 = jnp.concatenate([z64, -sin, z16, z32], axis=-1)
    rope_s2 = jnp.concatenate([z64, z16, sin, z32], axis=-1)
    return rope_c, rope_s1, rope_s2


def _pick_tiles(t, b):
    if t > 1024:
        tq = 384
    else:
        tq = 128
    tp = -(-t // tq) * tq
    n = b * tp
    tm = tq
    for cand in (1024, 768, 512):
        if n % cand == 0:
            tm = cand
            break
    return tp, tq, tm


def kernel(x, positions, meta_tokens, norm_mix_g, w_in, q_norm_g, w_uq, kv_norm_g, w_ukv, q_head_g, k_head_g, conv_w, conv_b, w_a, b_a, w_i, b_i, lru_lambda, attn_out_g, rnn_out_g, w_out, norm_ffn_g, ffn_w_gate, ffn_w_up, ffn_w_down, router_w, moe_w_gate, moe_w_up, moe_w_down):
    p = dict(norm_mix_g=norm_mix_g, w_in=w_in, q_norm_g=q_norm_g, w_uq=w_uq, kv_norm_g=kv_norm_g,
             w_ukv=w_ukv, q_head_g=q_head_g, k_head_g=k_head_g, conv_w=conv_w, conv_b=conv_b,
             w_a=w_a, b_a=b_a, w_i=w_i, b_i=b_i, lru_lambda=lru_lambda, attn_out_g=attn_out_g,
             rnn_out_g=rnn_out_g, w_out=w_out, norm_ffn_g=norm_ffn_g)
    b, seq, d = x.shape
    depth = w_in.shape[0]
    t = N_META + seq
    tp, tq, tm = _pick_tiles(t, b)
    n = b * tp

    meta = jnp.broadcast_to(meta_tokens[None].astype(x.dtype), (b, N_META, d))
    h = jnp.concatenate([meta, x, jnp.zeros((b, tp - t, d), x.dtype)], axis=1)
    meta_pos = jnp.broadcast_to(jnp.arange(N_META, dtype=jnp.int32)[None, :], (b, N_META))
    pos_full = jnp.concatenate([meta_pos, positions.astype(jnp.int32) + N_META,
                                jnp.zeros((b, tp - t), jnp.int32)], axis=1)
    rope_c, rope_s1, rope_s2 = _rope_tables(pos_full)

    for l in range(depth):
        lw = _layer_weights(l, p)
        q, k, v, xr, gt = _pre_mixer(h, rope_c, rope_s1, rope_s2, lw, tq)
        att = _attention(q, k, v, tq)
        rnn = _rglru(xr, gt, lw, tq)
        j = l // 2
        h2 = h.reshape(n, d)
        if l % 2 == 0:
            h2, u = _post_mixer(att.reshape(n, D_ATTN), rnn.reshape(n, D_RNN), h2, lw, tm)
            h2 = _dense_ffn(u, h2, ffn_w_gate[j].astype(BF16), ffn_w_up[j].astype(BF16),
                            ffn_w_down[j].astype(BF16), tm)
        else:
            w_r = jnp.pad(router_w[j], ((0, 0), (0, LANES - N_EXPERTS)))
            h2, u, logits = _post_mixer(att.reshape(n, D_ATTN), rnn.reshape(n, D_RNN), h2, lw,
                                        tm, w_router=w_r)
            h2 = _moe_layer(u, h2, logits, moe_w_gate[j].astype(BF16), moe_w_up[j].astype(BF16),
                            moe_w_down[j].astype(BF16), min(tm, 512))
        h = h2.reshape(b, tp, d)
    return h[:, N_META:t]
```

```python
import functools
import math

import jax
import jax.numpy as jnp
from jax import lax
from jax.experimental import pallas as pl
from jax.experimental.pallas import tpu as pltpu

F32 = jnp.float32
BF16 = jnp.bfloat16

D_MODEL = 1024
N_META = 16
D_ATTN = D_MODEL // 2
D_RNN = D_MODEL - D_ATTN
N_HEADS = 8
QK_NOPE = 64
QK_ROPE = 32
QK_HEAD = QK_NOPE + QK_ROPE
V_HEAD = D_ATTN // N_HEADS
Q_LORA = 384
KV_LORA = 256
ROPE_BASE = 10000.0
N_RNN_BLOCKS = 8
RNN_BLOCK = D_RNN // N_RNN_BLOCKS
CONV_W = 4
LRU_C = 8.0
N_EXPERTS = 8
TOP_K = 2
EPS = 1e-6

LANES = 128
SUBLANES = 8
BF16_ROWS = 16
V_ROWS = V_HEAD + BF16_ROWS
D_PROJ = Q_LORA + KV_LORA + 2 * D_RNN + LANES
VMEM_LIMIT = 56 * 1024 * 1024
NEG_INF = -1e30


def _rms(x):
    return x * lax.rsqrt(jnp.mean(x * x, axis=-1, keepdims=True) + EPS)


def _full(shape):
    return pl.BlockSpec(shape, lambda *_: (0,) * len(shape))


def _rope(x, c, s1, s2):
    return x * c + pltpu.roll(x, LANES - QK_ROPE // 2, 1) * s1 + pltpu.roll(x, QK_ROPE // 2, 1) * s2


def _pre_kernel(h_ref, rc_ref, rs1_ref, rs2_ref, gmix_ref, win_ref, gq_ref, wuq_ref, gkv_ref,
                wuk_ref, wuvt_ref, gqh_ref, gkh_ref, q_ref, k_ref, v_ref, xr_ref, gt_ref):
    u = (_rms(h_ref[0]) * gmix_ref[...]).astype(BF16)
    proj = jnp.dot(u, win_ref[...], preferred_element_type=F32)
    o1 = Q_LORA
    o2 = o1 + KV_LORA
    o3 = o2 + D_RNN
    o4 = o3 + D_RNN
    cq, ckv, krs = proj[:, :o1], proj[:, o1:o2], proj[:, o4:]
    xr_ref[0] = proj[:, o2:o3].astype(BF16)
    gt_ref[0] = proj[:, o3:o4].astype(BF16)

    cqn = (_rms(cq) * gq_ref[...]).astype(BF16)
    ckvn = (_rms(ckv) * gkv_ref[...]).astype(BF16)
    qf = jnp.dot(cqn, wuq_ref[...], preferred_element_type=F32)
    kf = jnp.dot(ckvn, wuk_ref[...], preferred_element_type=F32)
    vt = lax.dot_general(wuvt_ref[...], ckvn, (((1,), (1,)), ((), ())), preferred_element_type=F32)
    ones = jnp.ones((V_ROWS - V_HEAD, vt.shape[1]), BF16)
    for hh in range(N_HEADS):
        v_ref[0, hh, :V_HEAD, :] = vt[hh * V_HEAD:(hh + 1) * V_HEAD].astype(BF16)
        v_ref[0, hh, V_HEAD:, :] = ones

    c, s1, s2 = rc_ref[0], rs1_ref[0], rs2_ref[0]
    inv_dh = 1.0 / QK_HEAD
    gqh = gqh_ref[...] * (QK_HEAD ** -0.5 * math.log2(math.e))
    gkh = gkh_ref[...]
    for hh in range(N_HEADS):
        qh = qf[:, hh * LANES:(hh + 1) * LANES]
        inv = lax.rsqrt(jnp.sum(qh * qh, axis=-1, keepdims=True) * inv_dh + EPS)
        q_ref[0, hh] = _rope(qh * inv * gqh, c, s1, s2).astype(BF16)
    kr_ss = jnp.sum(krs * krs, axis=-1, keepdims=True)
    kr_rot = _rope(krs * gkh, c, s1, s2)
    for hh in range(N_HEADS):
        kh = kf[:, hh * LANES:(hh + 1) * LANES]
        inv = lax.rsqrt((jnp.sum(kh * kh, axis=-1, keepdims=True) + kr_ss) * inv_dh + EPS)
        k_ref[0, hh] = ((kh * gkh + kr_rot) * inv).astype(BF16)


def _pre_mixer(h, rope_c, rope_s1, rope_s2, lw, ta):
    b, tp, d = h.shape
    row = lambda n: pl.BlockSpec((1, ta, n), lambda bi, ti: (bi, ti, 0))
    head = lambda n: pl.BlockSpec((1, N_HEADS, ta, n), lambda bi, ti: (bi, 0, ti, 0))
    return pl.pallas_call(
        _pre_kernel,
        grid=(b, tp // ta),
        in_specs=[row(d), row(LANES), row(LANES), row(LANES),
                  _full((1, d)), _full((d, D_PROJ)),
                  _full((1, Q_LORA)), _full((Q_LORA, N_HEADS * LANES)),
                  _full((1, KV_LORA)), _full((KV_LORA, N_HEADS * LANES)),
                  _full((N_HEADS * V_HEAD, KV_LORA)),
                  _full((1, LANES)), _full((1, LANES))],
        out_specs=[head(LANES), head(LANES),
                   pl.BlockSpec((1, N_HEADS, V_ROWS, ta), lambda bi, ti: (bi, 0, 0, ti)),
                   row(D_RNN), row(D_RNN)],
        out_shape=[jax.ShapeDtypeStruct((b, N_HEADS, tp, LANES), BF16),
                   jax.ShapeDtypeStruct((b, N_HEADS, tp, LANES), BF16),
                   jax.ShapeDtypeStruct((b, N_HEADS, V_ROWS, tp), BF16),
                   jax.ShapeDtypeStruct((b, tp, D_RNN), BF16),
                   jax.ShapeDtypeStruct((b, tp, D_RNN), BF16)],
        compiler_params=pltpu.CompilerParams(
            dimension_semantics=("parallel", "parallel"), vmem_limit_bytes=VMEM_LIMIT),
        name="pre_mixer",
    )(h, rope_c, rope_s1, rope_s2, lw["g_mix"], lw["w_in"], lw["g_q"], lw["w_uq"], lw["g_kv"],
      lw["w_uk"], lw["w_uvt"], lw["g_qh"], lw["g_kh"])


def _attn_kernel(qi_ref, kj_ref, q_ref, k_ref, vt_ref, o_ref, m_sc, acc_sc, *, tq):
    qi = qi_ref[pl.program_id(1)]
    kj = kj_ref[pl.program_id(1)]

    @pl.when(kj == 0)
    def _():
        m_sc[...] = jnp.full(m_sc.shape, NEG_INF, F32)
        acc_sc[...] = jnp.zeros(acc_sc.shape, F32)

    def step(diagonal):
        if diagonal:
            keep = (lax.broadcasted_iota(jnp.int32, (tq, tq), 0)
                    <= lax.broadcasted_iota(jnp.int32, (tq, tq), 1))
        sts = [lax.dot_general(k_ref[0, hh], q_ref[0, hh], (((1,), (1,)), ((), ())),
                               preferred_element_type=F32)
               for hh in range(N_HEADS)]
        ps, alphas = [], []
        for hh in range(N_HEADS):
            st = jnp.where(keep, sts[hh], NEG_INF) if diagonal else sts[hh]
            m_prev = m_sc[hh]
            m_new = jnp.maximum(m_prev, jnp.max(st, axis=0, keepdims=True))
            alphas.append(jnp.exp2(m_prev - m_new))
            ps.append(jnp.exp2(st - m_new).astype(BF16))
            m_sc[hh] = m_new
        for hh in range(N_HEADS):
            acc_sc[hh] = alphas[hh] * acc_sc[hh] + jnp.dot(vt_ref[0, hh], ps[hh],
                                                           preferred_element_type=F32)

    @pl.when(kj < qi)
    def _():
        step(False)

    @pl.when(kj == qi)
    def _():
        step(True)
        outs = []
        for hh in range(N_HEADS):
            acc = acc_sc[hh]
            outs.append(acc[:V_HEAD] / acc[V_HEAD:V_HEAD + 1])
        o_ref[0] = jnp.concatenate(outs, axis=0).T.astype(BF16)


def _attention(q, k, vt, tq):
    b, nh, tp, _ = q.shape
    nt = tp // tq
    vrows = vt.shape[2]
    pairs = [(qi, kj) for qi in range(nt) for kj in range(qi + 1)]
    qi_tab = jnp.asarray([p[0] for p in pairs], jnp.int32)
    kj_tab = jnp.asarray([p[1] for p in pairs], jnp.int32)
    grid_spec = pltpu.PrefetchScalarGridSpec(
        num_scalar_prefetch=2,
        grid=(b, len(pairs)),
        in_specs=[pl.BlockSpec((1, nh, tq, LANES), lambda bi, p, qt, kt: (bi, 0, qt[p], 0)),
                  pl.BlockSpec((1, nh, tq, LANES), lambda bi, p, qt, kt: (bi, 0, kt[p], 0)),
                  pl.BlockSpec((1, nh, vrows, tq), lambda bi, p, qt, kt: (bi, 0, 0, kt[p]))],
        out_specs=pl.BlockSpec((1, tq, D_ATTN), lambda bi, p, qt, kt: (bi, qt[p], 0)),
        scratch_shapes=[pltpu.VMEM((nh, 1, tq), F32), pltpu.VMEM((nh, vrows, tq), F32)],
    )
    return pl.pallas_call(
        functools.partial(_attn_kernel, tq=tq),
        grid_spec=grid_spec,
        out_shape=jax.ShapeDtypeStruct((b, tp, D_ATTN), BF16),
        compiler_params=pltpu.CompilerParams(
            dimension_semantics=("parallel", "arbitrary"), vmem_limit_bytes=VMEM_LIMIT),
        name="attention",
    )(qi_tab, kj_tab, q, k, vt)


def _rnn_kernel(xr_ref, gt_ref, cw_ref, cb_ref, wai_ref, bai_ref, lam_ref, o_ref,
                a_sc, b_sc, h_sc, tail_sc, *, tt, nb):
    ng = a_sc.shape[0]

    @pl.when(pl.program_id(0) == 0)
    def _():
        h_sc[...] = jnp.zeros(h_sc.shape, F32)
        tail_sc[...] = jnp.zeros(tail_sc.shape, F32)

    lam = lam_ref[...]
    log_a_unit = -LRU_C * (jnp.maximum(-lam, 0.0) + jnp.log1p(jnp.exp(-jnp.abs(lam))))
    cw = cw_ref[...]
    for bi in range(nb):
        x = xr_ref[bi].astype(F32)
        xe = jnp.concatenate([tail_sc[bi], x], axis=0)
        xc = cb_ref[...] + x * cw[CONV_W - 1:CONV_W]
        for j in range(CONV_W - 1):
            sh = CONV_W - 1 - j
            xc = xc + xe[SUBLANES - sh:SUBLANES - sh + tt] * cw[j:j + 1]
        tail_sc[bi] = x[tt - SUBLANES:]
        gates = jnp.dot(xc.astype(BF16), wai_ref[...], preferred_element_type=F32) + bai_ref[...]
        r = jax.nn.sigmoid(gates[:, :D_RNN])
        ig = jax.nn.sigmoid(gates[:, D_RNN:])
        log_a = r * log_a_unit
        a = jnp.exp(log_a)
        mult = jnp.sqrt(1.0 - a * a)
        bv = mult * ig * xc
        for g in range(ng):
            a_sc[g, pl.ds(bi * tt, tt), :] = a[:, g * LANES:(g + 1) * LANES]
            b_sc[g, pl.ds(bi * tt, tt), :] = bv[:, g * LANES:(g + 1) * LANES]

    def scan_step(t, hs):
        rows = pl.ds(t, nb, stride=tt)
        new = []
        for g in range(ng):
            hg = a_sc[g, rows, :] * hs[g] + b_sc[g, rows, :]
            b_sc[g, rows, :] = hg
            new.append(hg)
        return tuple(new)

    h_last = lax.fori_loop(0, tt, scan_step, tuple(h_sc[g] for g in range(ng)), unroll=8)
    for g in range(ng):
        h_sc[g] = h_last[g]

    for bi in range(nb):
        hs = jnp.concatenate([b_sc[g, pl.ds(bi * tt, tt), :] for g in range(ng)], axis=-1)
        o_ref[bi] = (hs * jax.nn.gelu(gt_ref[bi].astype(F32), approximate=True)).astype(BF16)


def _rglru(xr, gt, lw, tt):
    b, tp, c = xr.shape
    blk = pl.BlockSpec((b, tt, c), lambda ti: (0, ti, 0))
    return pl.pallas_call(
        functools.partial(_rnn_kernel, tt=tt, nb=b),
        grid=(tp // tt,),
        in_specs=[blk, blk, _full((CONV_W, c)), _full((1, c)), _full((c, 2 * c)),
                  _full((1, 2 * c)), _full((1, c))],
        out_specs=blk,
        out_shape=jax.ShapeDtypeStruct((b, tp, c), BF16),
        scratch_shapes=[pltpu.VMEM((c // LANES, b * tt, LANES), F32),
                        pltpu.VMEM((c // LANES, b * tt, LANES), F32),
                        pltpu.VMEM((c // LANES, b, LANES), F32),
                        pltpu.VMEM((b, SUBLANES, c), F32)],
        compiler_params=pltpu.CompilerParams(
            dimension_semantics=("arbitrary",), vmem_limit_bytes=VMEM_LIMIT),
        name="rglru",
    )(xr, gt, lw["conv_w"], lw["conv_b"], lw["w_ai"], lw["b_ai"], lw["lam"])


def _post_kernel(att_ref, rnn_ref, h_ref, ga_ref, gr_ref, wo_ref, gf_ref, *rest, with_router):
    if with_router:
        wr_ref, ho_ref, u_ref, lg_ref = rest
    else:
        ho_ref, u_ref = rest
    an = (_rms(att_ref[...].astype(F32)) * ga_ref[...]).astype(BF16)
    rn = (_rms(rnn_ref[...].astype(F32)) * gr_ref[...]).astype(BF16)
    mix = jnp.concatenate([an, rn], axis=-1)
    hn = h_ref[...] + jnp.dot(mix, wo_ref[...], preferred_element_type=F32)
    ho_ref[...] = hn
    u = _rms(hn) * gf_ref[...]
    u_ref[...] = u.astype(BF16)
    if with_router:
        w = wr_ref[...]
        u_hi = u.astype(BF16)
        w_hi = w.astype(BF16)
        u_lo = (u - u_hi.astype(F32)).astype(BF16)
        w_lo = (w - w_hi.astype(F32)).astype(BF16)
        lg_ref[...] = (jnp.dot(u_hi, w_hi, preferred_element_type=F32)
                       + jnp.dot(u_lo, w_hi, preferred_element_type=F32)
                       + jnp.dot(u_hi, w_lo, preferred_element_type=F32))


def _post_mixer(att, rnn, h, lw, tm, w_router=None):
    n, d = h.shape
    with_router = w_router is not None
    row = lambda c: pl.BlockSpec((tm, c), lambda i: (i, 0))
    in_specs = [row(D_ATTN), row(D_RNN), row(d), _full((1, D_ATTN)), _full((1, D_RNN)),
                _full((d, d)), _full((1, d))]
    out_specs = [row(d), row(d)]
    out_shape = [jax.ShapeDtypeStruct((n, d), F32), jax.ShapeDtypeStruct((n, d), BF16)]
    args = [att, rnn, h, lw["g_att"], lw["g_rnn"], lw["w_out"], lw["g_ffn"]]
    if with_router:
        in_specs.append(_full((d, LANES)))
        out_specs.append(row(LANES))
        out_shape.append(jax.ShapeDtypeStruct((n, LANES), F32))
        args.append(w_router)
    return pl.pallas_call(
        functools.partial(_post_kernel, with_router=with_router),
        grid=(n // tm,),
        in_specs=in_specs, out_specs=out_specs, out_shape=out_shape,
        compiler_params=pltpu.CompilerParams(
            dimension_semantics=("parallel",), vmem_limit_bytes=VMEM_LIMIT),
        name="post_mixer",
    )(*args)


def _swiglu_tile(u, wgu_ref, wd_ref, act_sc):
    f = act_sc.shape[-1]
    for c in range(f // LANES):
        gu = jnp.dot(u, wgu_ref[:, 2 * LANES * c:2 * LANES * (c + 1)], preferred_element_type=F32)
        act_sc[:, c * LANES:(c + 1) * LANES] = (jax.nn.silu(gu[:, :LANES]) * gu[:, LANES:]).astype(BF16)
    return jnp.dot(act_sc[...], wd_ref[...], preferred_element_type=F32)


def _interleave_gate_up(wg, wu):
    lead, f = wg.shape[:-1], wg.shape[-1]
    blocks = lambda w: w.reshape(lead + (f // LANES, LANES))
    return jnp.stack([blocks(wg), blocks(wu)], axis=-2).reshape(lead + (2 * f,))


def _ffn_kernel(u_ref, h_ref, wgu_ref, wd_ref, o_ref, act_sc):
    o_ref[...] = h_ref[...] + _swiglu_tile(u_ref[...], wgu_ref, wd_ref, act_sc)


def _dense_ffn(u, h, wgu, wd, tm):
    n, d = h.shape
    f = wd.shape[0]
    row = pl.BlockSpec((tm, d), lambda i: (i, 0))
    once = pl.Buffered(1)
    return pl.pallas_call(
        _ffn_kernel,
        grid=(n // tm,),
        in_specs=[row, row,
                  pl.BlockSpec((d, 2 * f), lambda i: (0, 0), pipeline_mode=once),
                  pl.BlockSpec((f, d), lambda i: (0, 0), pipeline_mode=once)],
        out_specs=row,
        out_shape=jax.ShapeDtypeStruct((n, d), F32),
        scratch_shapes=[pltpu.VMEM((tm, f), BF16)],
        compiler_params=pltpu.CompilerParams(
            dimension_semantics=("parallel",), vmem_limit_bytes=VMEM_LIMIT),
        name="dense_ffn",
    )(u, h, wgu, wd)


def _moe_kernel(te_ref, nt_ref, x_ref, gate_ref, wgu_ref, wd_ref, o_ref, act_sc):
    @pl.when(pl.program_id(0) < nt_ref[0])
    def _():
        y = _swiglu_tile(x_ref[...], wgu_ref.at[0], wd_ref.at[0], act_sc)
        o_ref[...] = y * gate_ref[...]

    @pl.when(pl.program_id(0) >= nt_ref[0])
    def _():
        o_ref[...] = jnp.zeros(o_ref.shape, F32)


def _grouped_ffn(tile_expert, n_tiles, xs, gates, wgu, wd, tm):
    a, d = xs.shape
    f = wd.shape[1]
    grid_spec = pltpu.PrefetchScalarGridSpec(
        num_scalar_prefetch=2,
        grid=(a // tm,),
        in_specs=[pl.BlockSpec((tm, d), lambda i, te, nt: (i, 0)),
                  pl.BlockSpec((tm, 1), lambda i, te, nt: (i, 0)),
                  pl.BlockSpec((1, d, 2 * f), lambda i, te, nt: (te[i], 0, 0)),
                  pl.BlockSpec((1, f, d), lambda i, te, nt: (te[i], 0, 0))],
        out_specs=pl.BlockSpec((tm, d), lambda i, te, nt: (i, 0)),
        scratch_shapes=[pltpu.VMEM((tm, f), BF16)],
    )
    return pl.pallas_call(
        _moe_kernel,
        grid_spec=grid_spec,
        out_shape=jax.ShapeDtypeStruct((a, d), F32),
        compiler_params=pltpu.CompilerParams(
            dimension_semantics=("arbitrary",), vmem_limit_bytes=VMEM_LIMIT),
        name="moe_ffn",
    )(tile_expert, n_tiles, xs, gates, wgu, wd)


def _moe_layer(u, h, logits, wgu, wd, tm):
    n, d = h.shape
    top_val, top_idx = lax.top_k(logits[:, :N_EXPERTS], TOP_K)
    top_w = jax.nn.softmax(top_val, axis=-1)
    e_flat = top_idx.reshape(-1)
    onehot = (e_flat[:, None] == jnp.arange(N_EXPERTS)[None, :]).astype(jnp.int32)
    rank = jnp.sum((jnp.cumsum(onehot, axis=0) - onehot) * onehot, axis=-1)
    counts = jnp.sum(onehot, axis=0)
    padded = ((counts + tm - 1) // tm) * tm
    ends = jnp.cumsum(padded)
    starts = ends - padded
    pos = starts[e_flat] + rank
    a_pad = (TOP_K * n // tm + N_EXPERTS) * tm
    tok = jnp.repeat(jnp.arange(n, dtype=jnp.int32), TOP_K)
    slot_tok = jnp.zeros((a_pad,), jnp.int32).at[pos].set(tok)
    slot_gate = jnp.zeros((a_pad,), F32).at[pos].set(top_w.reshape(-1))
    tile_start = jnp.arange(a_pad // tm, dtype=jnp.int32) * tm
    tile_expert = jnp.minimum(jnp.searchsorted(ends, tile_start, side="right"),
                              N_EXPERTS - 1).astype(jnp.int32)
    n_tiles = (ends[-1] // tm).astype(jnp.int32).reshape(1)
    xs = jnp.take(u, slot_tok, axis=0)
    y = _grouped_ffn(tile_expert, n_tiles, xs, slot_gate[:, None], wgu, wd, tm)
    pos2 = pos.reshape(n, TOP_K)
    return h + jnp.take(y, pos2[:, 0], axis=0) + jnp.take(y, pos2[:, 1], axis=0)


def _pad_head_cols(w, width):
    kdim = w.shape[0]
    w = w.reshape(kdim, N_HEADS, width)
    return jnp.pad(w, ((0, 0), (0, 0), (0, LANES - width))).reshape(kdim, N_HEADS * LANES)


def _block_diag(w):
    nb, c, _ = w.shape
    eye = jnp.eye(nb, dtype=w.dtype)
    return (eye[:, None, :, None] * w[:, :, None, :]).reshape(nb * c, nb * c)


def _layer_weights(l, p):
    w_in = p["w_in"][l]
    s1 = Q_LORA
    s2 = s1 + KV_LORA
    s3 = s2 + QK_ROPE
    s4 = s3 + D_RNN
    kr_slab = jnp.pad(w_in[:, s2:s3], ((0, 0), (QK_NOPE, LANES - QK_HEAD)))
    w_cat = jnp.concatenate([w_in[:, :s2], w_in[:, s3:s4], w_in[:, s4:], kr_slab], axis=1)
    w_ukv = p["w_ukv"][l].reshape(KV_LORA, N_HEADS, QK_NOPE + V_HEAD)
    w_uk = _pad_head_cols(w_ukv[:, :, :QK_NOPE].reshape(KV_LORA, N_HEADS * QK_NOPE), QK_NOPE)
    w_uv = w_ukv[:, :, QK_NOPE:].reshape(KV_LORA, N_HEADS * V_HEAD)
    pad_gain = lambda g: jnp.pad(g, (0, LANES - QK_HEAD)).reshape(1, LANES)
    return {
        "g_mix": p["norm_mix_g"][l].reshape(1, -1),
        "w_in": w_cat.astype(BF16),
        "g_q": p["q_norm_g"][l].reshape(1, -1),
        "w_uq": _pad_head_cols(p["w_uq"][l], QK_HEAD).astype(BF16),
        "g_kv": p["kv_norm_g"][l].reshape(1, -1),
        "w_uk": w_uk.astype(BF16),
        "w_uvt": w_uv.T.astype(BF16),
        "g_qh": pad_gain(p["q_head_g"][l]),
        "g_kh": pad_gain(p["k_head_g"][l]),
        "conv_w": p["conv_w"][l],
        "conv_b": p["conv_b"][l].reshape(1, -1),
        "w_ai": jnp.concatenate([_block_diag(p["w_a"][l]), _block_diag(p["w_i"][l])],
                                axis=1).astype(BF16),
        "b_ai": jnp.concatenate([p["b_a"][l], p["b_i"][l]]).reshape(1, -1),
        "lam": p["lru_lambda"][l].reshape(1, -1),
        "g_att": p["attn_out_g"][l].reshape(1, -1),
        "g_rnn": p["rnn_out_g"][l].reshape(1, -1),
        "w_out": p["w_out"][l].astype(BF16),
        "g_ffn": p["norm_ffn_g"][l].reshape(1, -1),
    }


def _rope_tables(pos_full):
    inv_freq = ROPE_BASE ** (-jnp.arange(0, QK_ROPE, 2, dtype=F32) / QK_ROPE)
    ang = pos_full.astype(F32)[..., None] * inv_freq
    cos, sin = jnp.cos(ang), jnp.sin(ang)
    shape = cos.shape[:-1]
    ones = jnp.ones(shape + (QK_NOPE,), F32)
    z16 = jnp.zeros(shape + (QK_ROPE // 2,), F32)
    z32 = jnp.zeros(shape + (LANES - QK_HEAD,), F32)
    z64 = jnp.zeros(shape + (QK_NOPE,), F32)
    rope_c = jnp.concatenate([ones, cos, cos, z32], axis=-1)
    rope_s1 = jnp.concatenate([z64, -sin, z16, z32], axis=-1)
    rope_s2 = jnp.concatenate([z64, z16, sin, z32], axis=-1)
    return rope_c, rope_s1, rope_s2


def _pick_tiles(t, b):
    if t > 1024:
        tq = 384
    else:
        tq = 128
    tp = -(-t // tq) * tq
    n = b * tp
    tm = tq
    for cand in (1024, 768, 512):
        if n % cand == 0:
            tm = cand
            break
    return tp, tq, tm


def kernel(x, positions, meta_tokens, norm_mix_g, w_in, q_norm_g, w_uq, kv_norm_g, w_ukv, q_head_g, k_head_g, conv_w, conv_b, w_a, b_a, w_i, b_i, lru_lambda, attn_out_g, rnn_out_g, w_out, norm_ffn_g, ffn_w_gate, ffn_w_up, ffn_w_down, router_w, moe_w_gate, moe_w_up, moe_w_down):
    p = dict(norm_mix_g=norm_mix_g, w_in=w_in, q_norm_g=q_norm_g, w_uq=w_uq, kv_norm_g=kv_norm_g,
             w_ukv=w_ukv, q_head_g=q_head_g, k_head_g=k_head_g, conv_w=conv_w, conv_b=conv_b,
             w_a=w_a, b_a=b_a, w_i=w_i, b_i=b_i, lru_lambda=lru_lambda, attn_out_g=attn_out_g,
             rnn_out_g=rnn_out_g, w_out=w_out, norm_ffn_g=norm_ffn_g)
    b, seq, d = x.shape
    depth = w_in.shape[0]
    t = N_META + seq
    tp, tq, tm = _pick_tiles(t, b)
    n = b * tp

    meta = jnp.broadcast_to(meta_tokens[None].astype(x.dtype), (b, N_META, d))
    h = jnp.concatenate([meta, x, jnp.zeros((b, tp - t, d), x.dtype)], axis=1)
    meta_pos = jnp.broadcast_to(jnp.arange(N_META, dtype=jnp.int32)[None, :], (b, N_META))
    pos_full = jnp.concatenate([meta_pos, positions.astype(jnp.int32) + N_META,
                                jnp.zeros((b, tp - t), jnp.int32)], axis=1)
    rope_c, rope_s1, rope_s2 = _rope_tables(pos_full)

    for l in range(depth):
        lw = _layer_weights(l, p)
        q, k, v, xr, gt = _pre_mixer(h, rope_c, rope_s1, rope_s2, lw, tq)
        att = _attention(q, k, v, tq)
        rnn = _rglru(xr, gt, lw, tq)
        j = l // 2
        h2 = h.reshape(n, d)
        if l % 2 == 0:
            h2, u = _post_mixer(att.reshape(n, D_ATTN), rnn.reshape(n, D_RNN), h2, lw, tm)
            wgu = _interleave_gate_up(ffn_w_gate[j], ffn_w_up[j]).astype(BF16)
            h2 = _dense_ffn(u, h2, wgu, ffn_w_down[j].astype(BF16), tm)
        else:
            w_r = jnp.pad(router_w[j], ((0, 0), (0, LANES - N_EXPERTS)))
            h2, u, logits = _post_mixer(att.reshape(n, D_ATTN), rnn.reshape(n, D_RNN), h2, lw,
                                        tm, w_router=w_r)
            wgu = _interleave_gate_up(moe_w_gate[j], moe_w_up[j]).astype(BF16)
            h2 = _moe_layer(u, h2, logits, wgu, moe_w_down[j].astype(BF16), min(tm, 512))
        h = h2.reshape(b, tp, d)
    return h[:, N_META:t]
```

```python
import functools
import math

import jax
import jax.numpy as jnp
from jax import lax
from jax.experimental import pallas as pl
from jax.experimental.pallas import tpu as pltpu

F32 = jnp.float32
BF16 = jnp.bfloat16

D_MODEL = 1024
N_META = 16
D_ATTN = D_MODEL // 2
D_RNN = D_MODEL - D_ATTN
N_HEADS = 8
QK_NOPE = 64
QK_ROPE = 32
QK_HEAD = QK_NOPE + QK_ROPE
V_HEAD = D_ATTN // N_HEADS
Q_LORA = 384
KV_LORA = 256
ROPE_BASE = 10000.0
N_RNN_BLOCKS = 8
RNN_BLOCK = D_RNN // N_RNN_BLOCKS
CONV_W = 4
LRU_C = 8.0
N_EXPERTS = 8
TOP_K = 2
EPS = 1e-6

LANES = 128
SUBLANES = 8
MXU_WIDTH = 256
BF16_ROWS = 16
HALF_ROPE = QK_ROPE // 2
V_ROWS = V_HEAD + BF16_ROWS
D_LATENT = Q_LORA + KV_LORA + QK_ROPE
VMEM_LIMIT = 56 * 1024 * 1024
NEG_INF = -1e30


def _rms(x):
    return x * lax.rsqrt(jnp.mean(x * x, axis=-1, keepdims=True) + EPS)


def _rms_rows(x):
    return x * lax.rsqrt(jnp.mean(x * x, axis=0, keepdims=True) + EPS)


def _full(shape):
    return pl.BlockSpec(shape, lambda *_: (0,) * len(shape))


def _pre_kernel(h_ref, cos_ref, sin_ref, gmix_ref, wrnn_ref, wlat_ref, gq_ref, gkv_ref, wuq_ref,
                wuk_ref, wuv_ref, gqh_ref, gkn_ref, gkr_ref, q_ref, k_ref, v_ref, xr_ref, gt_ref):
    u = (_rms(h_ref[0]) * gmix_ref[...]).astype(BF16)
    rnn_in = jnp.dot(u, wrnn_ref[...], preferred_element_type=F32)
    xr_ref[...] = rnn_in[:, :D_RNN].astype(BF16)
    gt_ref[...] = rnn_in[:, D_RNN:].astype(BF16)

    lat = lax.dot_general(wlat_ref[...], u, (((1,), (1,)), ((), ())), preferred_element_type=F32)
    cqn = (_rms_rows(lat[:Q_LORA]) * gq_ref[...]).astype(BF16)
    ckvn = (_rms_rows(lat[Q_LORA:Q_LORA + KV_LORA]) * gkv_ref[...]).astype(BF16)
    kr = lat[Q_LORA + KV_LORA:]
    qt = jnp.dot(wuq_ref[...], cqn, preferred_element_type=F32)
    kt = jnp.dot(wuk_ref[...], ckvn, preferred_element_type=F32)
    vt = jnp.dot(wuv_ref[...], ckvn, preferred_element_type=F32)
    ta = vt.shape[1]
    ones = jnp.ones((V_ROWS - V_HEAD, ta), BF16)
    for hh in range(N_HEADS):
        v_ref[0, hh, :V_HEAD, :] = vt[hh * V_HEAD:(hh + 1) * V_HEAD].astype(BF16)
        v_ref[0, hh, V_HEAD:, :] = ones

    cos, sin = cos_ref[0], sin_ref[0]
    inv_dh = 1.0 / QK_HEAD
    r1, r2 = QK_NOPE, QK_NOPE + HALF_ROPE
    gqh = gqh_ref[...]
    zeros_q = jnp.zeros((LANES - QK_HEAD, ta), BF16)
    for hh in range(N_HEADS):
        x = qt[hh * LANES:(hh + 1) * LANES]
        inv = lax.rsqrt(jnp.sum(x * x, axis=0, keepdims=True) * inv_dh + EPS)
        xn = x[:QK_HEAD] * gqh * inv
        x1, x2 = xn[r1:r2], xn[r2:]
        q_ref[0, hh, :r1, :] = xn[:r1].astype(BF16)
        q_ref[0, hh, r1:r2, :] = (x1 * cos - x2 * sin).astype(BF16)
        q_ref[0, hh, r2:QK_HEAD, :] = (x2 * cos + x1 * sin).astype(BF16)
        q_ref[0, hh, QK_HEAD:, :] = zeros_q

    kr_ss = jnp.sum(kr * kr, axis=0, keepdims=True)
    krg = kr * gkr_ref[...]
    k1, k2 = krg[:HALF_ROPE], krg[HALF_ROPE:]
    kr_rot = jnp.concatenate([k1 * cos - k2 * sin, k2 * cos + k1 * sin,
                              jnp.zeros((LANES - QK_HEAD, ta), F32)], axis=0)
    gkn = gkn_ref[...]
    for hh in range(N_HEADS):
        x = kt[hh * QK_NOPE:(hh + 1) * QK_NOPE]
        inv = lax.rsqrt((jnp.sum(x * x, axis=0, keepdims=True) + kr_ss) * inv_dh + EPS)
        kh = jnp.concatenate([x * gkn, kr_rot], axis=0) * inv
        k_ref[0, hh] = kh.T.astype(BF16)


def _pre_mixer(h, cos_t, sin_t, lw, ta):
    b, tp, d = h.shape
    tok_rows = lambda r: pl.BlockSpec((1, r, ta), lambda bi, ti: (bi, 0, ti))
    head_t = lambda r: pl.BlockSpec((1, N_HEADS, r, ta), lambda bi, ti: (bi, 0, 0, ti))
    time_major = pl.BlockSpec((ta, D_RNN), lambda bi, ti: (ti, bi))
    return pl.pallas_call(
        _pre_kernel,
        grid=(b, tp // ta),
        in_specs=[pl.BlockSpec((1, ta, d), lambda bi, ti: (bi, ti, 0)),
                  tok_rows(HALF_ROPE), tok_rows(HALF_ROPE),
                  _full((1, d)), _full((d, 2 * D_RNN)), _full((D_LATENT, d)),
                  _full((Q_LORA, ta)), _full((KV_LORA, ta)),
                  _full((N_HEADS * LANES, Q_LORA)), _full((N_HEADS * QK_NOPE, KV_LORA)),
                  _full((N_HEADS * V_HEAD, KV_LORA)),
                  _full((QK_HEAD, ta)), _full((QK_NOPE, ta)), _full((QK_ROPE, ta))],
        out_specs=[head_t(LANES),
                   pl.BlockSpec((1, N_HEADS, ta, LANES), lambda bi, ti: (bi, 0, ti, 0)),
                   head_t(V_ROWS), time_major, time_major],
        out_shape=[jax.ShapeDtypeStruct((b, N_HEADS, LANES, tp), BF16),
                   jax.ShapeDtypeStruct((b, N_HEADS, tp, LANES), BF16),
                   jax.ShapeDtypeStruct((b, N_HEADS, V_ROWS, tp), BF16),
                   jax.ShapeDtypeStruct((tp, b * D_RNN), BF16),
                   jax.ShapeDtypeStruct((tp, b * D_RNN), BF16)],
        compiler_params=pltpu.CompilerParams(
            dimension_semantics=("parallel", "parallel"), vmem_limit_bytes=VMEM_LIMIT),
        name="pre_mixer",
    )(h, cos_t, sin_t, lw["g_mix"], lw["w_rnn"], lw["w_lat"], lw["g_q"], lw["g_kv"], lw["w_uq"],
      lw["w_uk"], lw["w_uv"], lw["g_qh"], lw["g_kn"], lw["g_kr"])


def _attn_kernel(qi_ref, kj_ref, qt_ref, k_ref, vt_ref, o_ref, m_sc, acc_sc, *, tq):
    qi = qi_ref[pl.program_id(1)]
    kj = kj_ref[pl.program_id(1)]

    @pl.when(kj == 0)
    def _():
        m_sc[...] = jnp.full(m_sc.shape, NEG_INF, F32)
        acc_sc[...] = jnp.zeros(acc_sc.shape, F32)

    def step(diagonal):
        if diagonal:
            keep = (lax.broadcasted_iota(jnp.int32, (tq, tq), 0)
                    <= lax.broadcasted_iota(jnp.int32, (tq, tq), 1))
        sts = [jnp.dot(k_ref[0, hh], qt_ref[0, hh], preferred_element_type=F32)
               for hh in range(N_HEADS)]
        ps, alphas = [], []
        for hh in range(N_HEADS):
            st = jnp.where(keep, sts[hh], NEG_INF) if diagonal else sts[hh]
            m_prev = m_sc[hh]
            m_new = jnp.maximum(m_prev, jnp.max(st, axis=0, keepdims=True))
            alphas.append(jnp.exp2(m_prev - m_new))
            ps.append(jnp.exp2(st - m_new).astype(BF16))
            m_sc[hh] = m_new
        for hh in range(N_HEADS):
            acc_sc[hh] = alphas[hh] * acc_sc[hh] + jnp.dot(vt_ref[0, hh], ps[hh],
                                                           preferred_element_type=F32)

    @pl.when(kj < qi)
    def _():
        step(False)

    @pl.when(kj == qi)
    def _():
        step(True)
        outs = []
        for hh in range(N_HEADS):
            acc = acc_sc[hh]
            outs.append(acc[:V_HEAD] / acc[V_HEAD:V_HEAD + 1])
        o_ref[0] = jnp.concatenate(outs, axis=0).T.astype(BF16)


def _attention(qt, k, vt, tq):
    b, nh, tp, _ = k.shape
    nt = tp // tq
    vrows = vt.shape[2]
    pairs = [(qi, kj) for qi in range(nt) for kj in range(qi + 1)]
    qi_tab = jnp.asarray([p[0] for p in pairs], jnp.int32)
    kj_tab = jnp.asarray([p[1] for p in pairs], jnp.int32)
    grid_spec = pltpu.PrefetchScalarGridSpec(
        num_scalar_prefetch=2,
        grid=(b, len(pairs)),
        in_specs=[pl.BlockSpec((1, nh, LANES, tq), lambda bi, p, qt_, kt_: (bi, 0, 0, qt_[p])),
                  pl.BlockSpec((1, nh, tq, LANES), lambda bi, p, qt_, kt_: (bi, 0, kt_[p], 0)),
                  pl.BlockSpec((1, nh, vrows, tq), lambda bi, p, qt_, kt_: (bi, 0, 0, kt_[p]))],
        out_specs=pl.BlockSpec((1, tq, D_ATTN), lambda bi, p, qt_, kt_: (bi, qt_[p], 0)),
        scratch_shapes=[pltpu.VMEM((nh, 1, tq), F32), pltpu.VMEM((nh, vrows, tq), F32)],
    )
    return pl.pallas_call(
        functools.partial(_attn_kernel, tq=tq),
        grid_spec=grid_spec,
        out_shape=jax.ShapeDtypeStruct((b, tp, D_ATTN), BF16),
        compiler_params=pltpu.CompilerParams(
            dimension_semantics=("parallel", "arbitrary"), vmem_limit_bytes=VMEM_LIMIT),
        name="attention",
    )(qi_tab, kj_tab, qt, k, vt)


def _sigmoid(x):
    return 0.5 * jnp.tanh(0.5 * x) + 0.5


def _rnn_kernel(xr_ref, gt_ref, cw_ref, cb_ref, wai_ref, bai_ref, lam_ref, o_ref,
                xe_sc, a_sc, b_sc, h_sc, *, tt, nb, rows):
    halo = (CONV_W - 1) * nb

    @pl.when(pl.program_id(0) == 0)
    def _():
        h_sc[...] = jnp.zeros(h_sc.shape, F32)
        xe_sc[pl.ds(0, halo), :] = jnp.zeros((halo, xe_sc.shape[1]), F32)

    xe_sc[pl.ds(halo, tt * nb), :] = xr_ref[...].astype(F32)
    lam = lam_ref[...]
    log_a_unit = -LRU_C * (jnp.maximum(-lam, 0.0) + jnp.log1p(jnp.exp(-jnp.abs(lam))))
    cw = cw_ref[...]
    for r0 in range(0, tt * nb, rows):
        xc = cb_ref[...] + xe_sc[pl.ds(r0, rows), :] * cw[0:1]
        for j in range(1, CONV_W):
            xc = xc + xe_sc[pl.ds(r0 + j * nb, rows), :] * cw[j:j + 1]
        gates = jnp.dot(xc.astype(BF16), wai_ref[...], preferred_element_type=F32) + bai_ref[...]
        r = _sigmoid(gates[:, :D_RNN])
        ig = _sigmoid(gates[:, D_RNN:])
        a = jnp.exp(r * log_a_unit)
        a_sc[pl.ds(r0, rows), :] = a
        b_sc[pl.ds(r0, rows), :] = jnp.sqrt(1.0 - a * a) * ig * xc
    xe_sc[pl.ds(0, halo), :] = xe_sc[pl.ds(tt * nb, halo), :]

    def scan_step(t, h):
        sl = pl.ds(pl.multiple_of(t * nb, nb), nb)
        h = a_sc[sl, :] * h + b_sc[sl, :]
        b_sc[sl, :] = h
        return h

    h_sc[...] = lax.fori_loop(0, tt, scan_step, h_sc[...], unroll=8)

    for r0 in range(0, tt * nb, rows):
        sl = pl.ds(r0, rows)
        o_ref[sl, :] = (b_sc[sl, :] * jax.nn.gelu(gt_ref[sl, :].astype(F32), approximate=True)
                        ).astype(BF16)


def _rglru(xr, gt, lw, tt, nb):
    n, c = xr.shape
    blk = pl.BlockSpec((tt * nb, c), lambda ti: (ti, 0))
    rows = math.gcd(tt * nb, 512)
    return pl.pallas_call(
        functools.partial(_rnn_kernel, tt=tt, nb=nb, rows=rows),
        grid=(n // (tt * nb),),
        in_specs=[blk, blk, _full((CONV_W, c)), _full((1, c)), _full((c, 2 * c)),
                  _full((1, 2 * c)), _full((1, c))],
        out_specs=blk,
        out_shape=jax.ShapeDtypeStruct((n, c), BF16),
        scratch_shapes=[pltpu.VMEM(((tt + CONV_W - 1) * nb, c), F32),
                        pltpu.VMEM((tt * nb, c), F32), pltpu.VMEM((tt * nb, c), F32),
                        pltpu.VMEM((nb, c), F32)],
        compiler_params=pltpu.CompilerParams(
            dimension_semantics=("arbitrary",), vmem_limit_bytes=VMEM_LIMIT),
        name="rglru",
    )(xr, gt, lw["conv_w"], lw["conv_b"], lw["w_ai"], lw["b_ai"], lw["lam"])


def _post_kernel(att_ref, rnn_ref, h_ref, ga_ref, gr_ref, wo_ref, gf_ref, *rest, with_router):
    if with_router:
        wr_ref, ho_ref, u_ref, lg_ref = rest
    else:
        ho_ref, u_ref = rest
    an = (_rms(att_ref[0].astype(F32)) * ga_ref[...]).astype(BF16)
    rn = (_rms(rnn_ref[...].astype(F32)) * gr_ref[...]).astype(BF16)
    mix = jnp.concatenate([an, rn], axis=-1)
    hn = h_ref[0] + jnp.dot(mix, wo_ref[...], preferred_element_type=F32)
    ho_ref[0] = hn
    u = _rms(hn) * gf_ref[...]
    u_ref[0] = u.astype(BF16)
    if with_router:
        w = wr_ref[...]
        u_hi = u.astype(BF16)
        w_hi = w.astype(BF16)
        u_lo = (u - u_hi.astype(F32)).astype(BF16)
        w_lo = (w - w_hi.astype(F32)).astype(BF16)
        lg_ref[0] = (jnp.dot(u_hi, w_hi, preferred_element_type=F32)
                     + jnp.dot(u_lo, w_hi, preferred_element_type=F32)
                     + jnp.dot(u_hi, w_lo, preferred_element_type=F32))


def _post_mixer(att, rnn_tm, h, lw, tm, w_router=None):
    b, tp, d = h.shape
    with_router = w_router is not None
    row = lambda c: pl.BlockSpec((1, tm, c), lambda bi, ti: (bi, ti, 0))
    in_specs = [row(D_ATTN), pl.BlockSpec((tm, D_RNN), lambda bi, ti: (ti, bi)), row(d),
                _full((1, D_ATTN)), _full((1, D_RNN)), _full((d, d)), _full((1, d))]
    out_specs = [row(d), row(d)]
    out_shape = [jax.ShapeDtypeStruct((b, tp, d), F32), jax.ShapeDtypeStruct((b, tp, d), BF16)]
    args = [att, rnn_tm, h, lw["g_att"], lw["g_rnn"], lw["w_out"], lw["g_ffn"]]
    if with_router:
        in_specs.append(_full((d, LANES)))
        out_specs.append(row(LANES))
        out_shape.append(jax.ShapeDtypeStruct((b, tp, LANES), F32))
        args.append(w_router)
    return pl.pallas_call(
        functools.partial(_post_kernel, with_router=with_router),
        grid=(b, tp // tm),
        in_specs=in_specs, out_specs=out_specs, out_shape=out_shape,
        compiler_params=pltpu.CompilerParams(
            dimension_semantics=("parallel", "parallel"), vmem_limit_bytes=VMEM_LIMIT),
        name="post_mixer",
    )(*args)


def _swiglu_tile(u, wg_ref, wu_ref, wd_ref, act_sc):
    f = act_sc.shape[-1]
    full = f - f % MXU_WIDTH
    for c0 in range(0, full, MXU_WIDTH):
        g = jnp.dot(u, wg_ref[:, c0:c0 + MXU_WIDTH], preferred_element_type=F32)
        up = jnp.dot(u, wu_ref[:, c0:c0 + MXU_WIDTH], preferred_element_type=F32)
        act_sc[:, c0:c0 + MXU_WIDTH] = (jax.nn.silu(g) * up).astype(BF16)
    if full < f:
        w = jnp.concatenate([wg_ref[:, full:], wu_ref[:, full:]], axis=1)
        gu = jnp.dot(u, w, preferred_element_type=F32)
        act_sc[:, full:] = (jax.nn.silu(gu[:, :f - full]) * gu[:, f - full:]).astype(BF16)
    return jnp.dot(act_sc[...], wd_ref[...], preferred_element_type=F32)


def _ffn_kernel(u_ref, h_ref, wg_ref, wu_ref, wd_ref, o_ref, act_sc):
    o_ref[...] = h_ref[...] + _swiglu_tile(u_ref[...], wg_ref, wu_ref, wd_ref, act_sc)


def _dense_ffn(u, h, wg, wu, wd, tm):
    n, d = h.shape
    f = wd.shape[0]
    row = pl.BlockSpec((tm, d), lambda i: (i, 0))
    once = pl.Buffered(1)
    return pl.pallas_call(
        _ffn_kernel,
        grid=(n // tm,),
        in_specs=[row, row,
                  pl.BlockSpec((d, f), lambda i: (0, 0), pipeline_mode=once),
                  pl.BlockSpec((d, f), lambda i: (0, 0), pipeline_mode=once),
                  pl.BlockSpec((f, d), lambda i: (0, 0), pipeline_mode=once)],
        out_specs=row,
        out_shape=jax.ShapeDtypeStruct((n, d), F32),
        scratch_shapes=[pltpu.VMEM((tm, f), BF16)],
        compiler_params=pltpu.CompilerParams(
            dimension_semantics=("parallel",), vmem_limit_bytes=VMEM_LIMIT),
        name="dense_ffn",
    )(u, h, wg, wu, wd)


def _moe_kernel(te_ref, nt_ref, x_ref, gate_ref, wg_ref, wu_ref, wd_ref, o_ref, act_sc):
    @pl.when(pl.program_id(0) < nt_ref[0])
    def _():
        y = _swiglu_tile(x_ref[...], wg_ref.at[0], wu_ref.at[0], wd_ref.at[0], act_sc)
        o_ref[...] = y * gate_ref[...]

    @pl.when(pl.program_id(0) >= nt_ref[0])
    def _():
        o_ref[...] = jnp.zeros(o_ref.shape, F32)


def _grouped_ffn(tile_expert, n_tiles, xs, gates, wg, wu, wd, tm):
    a, d = xs.shape
    f = wd.shape[1]
    by_expert = lambda r, c: pl.BlockSpec((1, r, c), lambda i, te, nt: (te[i], 0, 0))
    grid_spec = pltpu.PrefetchScalarGridSpec(
        num_scalar_prefetch=2,
        grid=(a // tm,),
        in_specs=[pl.BlockSpec((tm, d), lambda i, te, nt: (i, 0)),
                  pl.BlockSpec((tm, 1), lambda i, te, nt: (i, 0)),
                  by_expert(d, f), by_expert(d, f), by_expert(f, d)],
        out_specs=pl.BlockSpec((tm, d), lambda i, te, nt: (i, 0)),
        scratch_shapes=[pltpu.VMEM((tm, f), BF16)],
    )
    return pl.pallas_call(
        _moe_kernel,
        grid_spec=grid_spec,
        out_shape=jax.ShapeDtypeStruct((a, d), F32),
        compiler_params=pltpu.CompilerParams(
            dimension_semantics=("arbitrary",), vmem_limit_bytes=VMEM_LIMIT),
        name="moe_ffn",
    )(tile_expert, n_tiles, xs, gates, wg, wu, wd)


def _moe_layer(u, h, logits, wg, wu, wd, tm):
    n, d = h.shape
    top_val, top_idx = lax.top_k(logits[:, :N_EXPERTS], TOP_K)
    top_w = jax.nn.softmax(top_val, axis=-1)
    e_flat = top_idx.reshape(-1)
    onehot = (e_flat[:, None] == jnp.arange(N_EXPERTS)[None, :]).astype(jnp.int32)
    rank = jnp.sum((jnp.cumsum(onehot, axis=0) - onehot) * onehot, axis=-1)
    counts = jnp.sum(onehot, axis=0)
    padded = ((counts + tm - 1) // tm) * tm
    ends = jnp.cumsum(padded)
    starts = ends - padded
    pos = starts[e_flat] + rank
    a_pad = (TOP_K * n // tm + N_EXPERTS) * tm
    tok = jnp.repeat(jnp.arange(n, dtype=jnp.int32), TOP_K)
    slot_tok = jnp.zeros((a_pad,), jnp.int32).at[pos].set(tok)
    slot_gate = jnp.zeros((a_pad,), F32).at[pos].set(top_w.reshape(-1))
    tile_start = jnp.arange(a_pad // tm, dtype=jnp.int32) * tm
    tile_expert = jnp.minimum(jnp.searchsorted(ends, tile_start, side="right"),
                              N_EXPERTS - 1).astype(jnp.int32)
    n_tiles = (ends[-1] // tm).astype(jnp.int32).reshape(1)
    xs = jnp.take(u, slot_tok, axis=0)
    y = _grouped_ffn(tile_expert, n_tiles, xs, slot_gate[:, None], wg, wu, wd, tm)
    pos2 = pos.reshape(n, TOP_K)
    return h + jnp.take(y, pos2[:, 0], axis=0) + jnp.take(y, pos2[:, 1], axis=0)


def _block_diag(w):
    nb, c, _ = w.shape
    eye = jnp.eye(nb, dtype=w.dtype)
    return (eye[:, None, :, None] * w[:, :, None, :]).reshape(nb * c, nb * c)


def _lane_table(g, ta):
    return jnp.broadcast_to(g[:, None], (g.shape[0], ta))


def _layer_weights(l, p, ta):
    w_in = p["w_in"][l]
    w_uq = p["w_uq"][l].reshape(Q_LORA, N_HEADS, QK_HEAD)
    w_uq = jnp.pad(w_uq, ((0, 0), (0, 0), (0, LANES - QK_HEAD))).reshape(Q_LORA, N_HEADS * LANES)
    w_ukv = p["w_ukv"][l].reshape(KV_LORA, N_HEADS, QK_NOPE + V_HEAD)
    w_uk = w_ukv[:, :, :QK_NOPE].reshape(KV_LORA, N_HEADS * QK_NOPE)
    w_uv = w_ukv[:, :, QK_NOPE:].reshape(KV_LORA, N_HEADS * V_HEAD)
    q_gain = p["q_head_g"][l] * (QK_HEAD ** -0.5 * math.log2(math.e))
    return {
        "g_mix": p["norm_mix_g"][l].reshape(1, -1),
        "w_rnn": w_in[:, D_LATENT:].astype(BF16),
        "w_lat": w_in[:, :D_LATENT].T.astype(BF16),
        "g_q": _lane_table(p["q_norm_g"][l], ta),
        "g_kv": _lane_table(p["kv_norm_g"][l], ta),
        "w_uq": w_uq.T.astype(BF16),
        "w_uk": w_uk.T.astype(BF16),
        "w_uv": w_uv.T.astype(BF16),
        "g_qh": _lane_table(q_gain, ta),
        "g_kn": _lane_table(p["k_head_g"][l][:QK_NOPE], ta),
        "g_kr": _lane_table(p["k_head_g"][l][QK_NOPE:], ta),
        "conv_w": p["conv_w"][l],
        "conv_b": p["conv_b"][l].reshape(1, -1),
        "w_ai": jnp.concatenate([_block_diag(p["w_a"][l]), _block_diag(p["w_i"][l])],
                                axis=1).astype(BF16),
        "b_ai": jnp.concatenate([p["b_a"][l], p["b_i"][l]]).reshape(1, -1),
        "lam": p["lru_lambda"][l].reshape(1, -1),
        "g_att": p["attn_out_g"][l].reshape(1, -1),
        "g_rnn": p["rnn_out_g"][l].reshape(1, -1),
        "w_out": p["w_out"][l].astype(BF16),
        "g_ffn": p["norm_ffn_g"][l].reshape(1, -1),
    }


def _rope_tables(pos_full):
    inv_freq = ROPE_BASE ** (-jnp.arange(0, QK_ROPE, 2, dtype=F32) / QK_ROPE)
    ang = pos_full.astype(F32)[:, None, :] * inv_freq[None, :, None]
    return jnp.cos(ang), jnp.sin(ang)


def _largest_tile(total, cap, unit):
    best = unit
    for cand in range(unit, cap + 1, unit):
        if total % cand == 0:
            best = cand
    return best


def _pick_tiles(t, b):
    tq = 384 if t > 1024 else LANES
    tp = -(-t // tq) * tq
    tm_post = _largest_tile(tp, 768, 4 * BF16_ROWS)
    tm_ffn = _largest_tile(b * tp, 1024, 256)
    return tp, tq, tm_post, tm_ffn


def kernel(x, positions, meta_tokens, norm_mix_g, w_in, q_norm_g, w_uq, kv_norm_g, w_ukv, q_head_g, k_head_g, conv_w, conv_b, w_a, b_a, w_i, b_i, lru_lambda, attn_out_g, rnn_out_g, w_out, norm_ffn_g, ffn_w_gate, ffn_w_up, ffn_w_down, router_w, moe_w_gate, moe_w_up, moe_w_down):
    p = dict(norm_mix_g=norm_mix_g, w_in=w_in, q_norm_g=q_norm_g, w_uq=w_uq, kv_norm_g=kv_norm_g,
             w_ukv=w_ukv, q_head_g=q_head_g, k_head_g=k_head_g, conv_w=conv_w, conv_b=conv_b,
             w_a=w_a, b_a=b_a, w_i=w_i, b_i=b_i, lru_lambda=lru_lambda, attn_out_g=attn_out_g,
             rnn_out_g=rnn_out_g, w_out=w_out, norm_ffn_g=norm_ffn_g)
    b, seq, d = x.shape
    depth = w_in.shape[0]
    t = N_META + seq
    tp, tq, tm_post, tm_ffn = _pick_tiles(t, b)
    n = b * tp

    meta = jnp.broadcast_to(meta_tokens[None].astype(x.dtype), (b, N_META, d))
    h = jnp.concatenate([meta, x, jnp.zeros((b, tp - t, d), x.dtype)], axis=1)
    meta_pos = jnp.broadcast_to(jnp.arange(N_META, dtype=jnp.int32)[None, :], (b, N_META))
    pos_full = jnp.concatenate([meta_pos, positions.astype(jnp.int32) + N_META,
                                jnp.zeros((b, tp - t), jnp.int32)], axis=1)
    cos_t, sin_t = _rope_tables(pos_full)

    for l in range(depth):
        lw = _layer_weights(l, p, tq)
        qt, k, vt, xr, gt = _pre_mixer(h, cos_t, sin_t, lw, tq)
        att = _attention(qt, k, vt, tq)
        rnn = _rglru(xr.reshape(tp * b, D_RNN), gt.reshape(tp * b, D_RNN), lw, tq, b)
        rnn = rnn.reshape(tp, b * D_RNN)
        j = l // 2
        if l % 2 == 0:
            h, u = _post_mixer(att, rnn, h, lw, tm_post)
            h = _dense_ffn(u.reshape(n, d), h.reshape(n, d), ffn_w_gate[j].astype(BF16),
                           ffn_w_up[j].astype(BF16), ffn_w_down[j].astype(BF16), tm_ffn)
        else:
            w_r = jnp.pad(router_w[j], ((0, 0), (0, LANES - N_EXPERTS)))
            h, u, logits = _post_mixer(att, rnn, h, lw, tm_post, w_router=w_r)
            h = _moe_layer(u.reshape(n, d), h.reshape(n, d), logits.reshape(n, LANES),
                           moe_w_gate[j].astype(BF16), moe_w_up[j].astype(BF16),
                           moe_w_down[j].astype(BF16), min(tm_ffn, 512))
        h = h.reshape(b, tp, d)
    return h[:, N_META:t]
```

```python
import functools
import math

import jax
import jax.numpy as jnp
from jax import lax
from jax.experimental import pallas as pl
from jax.experimental.pallas import tpu as pltpu

F32 = jnp.float32
BF16 = jnp.bfloat16

D_MODEL = 1024
N_META = 16
D_ATTN = D_MODEL // 2
D_RNN = D_MODEL - D_ATTN
N_HEADS = 8
QK_NOPE = 64
QK_ROPE = 32
QK_HEAD = QK_NOPE + QK_ROPE
V_HEAD = D_ATTN // N_HEADS
Q_LORA = 384
KV_LORA = 256
ROPE_BASE = 10000.0
N_RNN_BLOCKS = 8
RNN_BLOCK = D_RNN // N_RNN_BLOCKS
CONV_W = 4
LRU_C = 8.0
N_EXPERTS = 8
TOP_K = 2
EPS = 1e-6

LANES = 128
SUBLANES = 8
MXU_WIDTH = 256
BF16_ROWS = 16
HALF_ROPE = QK_ROPE // 2
V_ROWS = V_HEAD + BF16_ROWS
D_LATENT = Q_LORA + KV_LORA + QK_ROPE
SEQS_PER_STEP = 2
VMEM_LIMIT = 56 * 1024 * 1024
NEG_INF = -1e30


def _rms(x):
    return x * lax.rsqrt(jnp.mean(x * x, axis=-1, keepdims=True) + EPS)


def _rms_rows(x):
    return x * lax.rsqrt(jnp.mean(x * x, axis=0, keepdims=True) + EPS)


def _full(shape):
    return pl.BlockSpec(shape, lambda *_: (0,) * len(shape))


def _pre_kernel(h_ref, cos_ref, sin_ref, gmix_ref, wrnn_ref, wlat_ref, gq_ref, gkv_ref, wuq_ref,
                wuk_ref, wuv_ref, gqh_ref, gkn_ref, gkr_ref, q_ref, k_ref, v_ref, xr_ref, gt_ref):
    u = (_rms(h_ref[0]) * gmix_ref[...]).astype(BF16)
    rnn_in = jnp.dot(u, wrnn_ref[...], preferred_element_type=F32)
    xr_ref[0] = rnn_in[:, :D_RNN].astype(BF16)
    gt_ref[0] = rnn_in[:, D_RNN:].astype(BF16)

    lat = lax.dot_general(wlat_ref[...], u, (((1,), (1,)), ((), ())), preferred_element_type=F32)
    cqn = (_rms_rows(lat[:Q_LORA]) * gq_ref[...]).astype(BF16)
    ckvn = (_rms_rows(lat[Q_LORA:Q_LORA + KV_LORA]) * gkv_ref[...]).astype(BF16)
    kr = lat[Q_LORA + KV_LORA:]
    qt = jnp.dot(wuq_ref[...], cqn, preferred_element_type=F32)
    kt = jnp.dot(wuk_ref[...], ckvn, preferred_element_type=F32)
    vt = jnp.dot(wuv_ref[...], ckvn, preferred_element_type=F32)
    ta = vt.shape[1]
    ones = jnp.ones((V_ROWS - V_HEAD, ta), BF16)
    for hh in range(N_HEADS):
        v_ref[0, hh, :V_HEAD, :] = vt[hh * V_HEAD:(hh + 1) * V_HEAD].astype(BF16)
        v_ref[0, hh, V_HEAD:, :] = ones

    cos, sin = cos_ref[0], sin_ref[0]
    inv_dh = 1.0 / QK_HEAD
    r1, r2 = QK_NOPE, QK_NOPE + HALF_ROPE
    gqh = gqh_ref[...]
    zeros_q = jnp.zeros((LANES - QK_HEAD, ta), BF16)
    for hh in range(N_HEADS):
        x = qt[hh * LANES:(hh + 1) * LANES]
        inv = lax.rsqrt(jnp.sum(x * x, axis=0, keepdims=True) * inv_dh + EPS)
        xn = x[:QK_HEAD] * gqh * inv
        x1, x2 = xn[r1:r2], xn[r2:]
        q_ref[0, hh, :r1, :] = xn[:r1].astype(BF16)
        q_ref[0, hh, r1:r2, :] = (x1 * cos - x2 * sin).astype(BF16)
        q_ref[0, hh, r2:QK_HEAD, :] = (x2 * cos + x1 * sin).astype(BF16)
        q_ref[0, hh, QK_HEAD:, :] = zeros_q

    kr_ss = jnp.sum(kr * kr, axis=0, keepdims=True)
    krg = kr * gkr_ref[...]
    k1, k2 = krg[:HALF_ROPE], krg[HALF_ROPE:]
    kr_rot = jnp.concatenate([k1 * cos - k2 * sin, k2 * cos + k1 * sin,
                              jnp.zeros((LANES - QK_HEAD, ta), F32)], axis=0)
    gkn = gkn_ref[...]
    for hh in range(N_HEADS):
        x = kt[hh * QK_NOPE:(hh + 1) * QK_NOPE]
        inv = lax.rsqrt((jnp.sum(x * x, axis=0, keepdims=True) + kr_ss) * inv_dh + EPS)
        kh = jnp.concatenate([x * gkn, kr_rot], axis=0) * inv
        k_ref[0, hh] = kh.T.astype(BF16)


def _pre_mixer(h, cos_t, sin_t, lw, ta):
    b, tp, d = h.shape
    tok_rows = lambda r: pl.BlockSpec((1, r, ta), lambda bi, ti: (bi, 0, ti))
    head_t = lambda r: pl.BlockSpec((1, N_HEADS, r, ta), lambda bi, ti: (bi, 0, 0, ti))
    rnn_rows = pl.BlockSpec((1, ta, D_RNN), lambda bi, ti: (bi, ti, 0))
    return pl.pallas_call(
        _pre_kernel,
        grid=(b, tp // ta),
        in_specs=[pl.BlockSpec((1, ta, d), lambda bi, ti: (bi, ti, 0)),
                  tok_rows(HALF_ROPE), tok_rows(HALF_ROPE),
                  _full((1, d)), _full((d, 2 * D_RNN)), _full((D_LATENT, d)),
                  _full((Q_LORA, ta)), _full((KV_LORA, ta)),
                  _full((N_HEADS * LANES, Q_LORA)), _full((N_HEADS * QK_NOPE, KV_LORA)),
                  _full((N_HEADS * V_HEAD, KV_LORA)),
                  _full((QK_HEAD, ta)), _full((QK_NOPE, ta)), _full((QK_ROPE, ta))],
        out_specs=[head_t(LANES),
                   pl.BlockSpec((1, N_HEADS, ta, LANES), lambda bi, ti: (bi, 0, ti, 0)),
                   head_t(V_ROWS), rnn_rows, rnn_rows],
        out_shape=[jax.ShapeDtypeStruct((b, N_HEADS, LANES, tp), BF16),
                   jax.ShapeDtypeStruct((b, N_HEADS, tp, LANES), BF16),
                   jax.ShapeDtypeStruct((b, N_HEADS, V_ROWS, tp), BF16),
                   jax.ShapeDtypeStruct((b, tp, D_RNN), BF16),
                   jax.ShapeDtypeStruct((b, tp, D_RNN), BF16)],
        compiler_params=pltpu.CompilerParams(
            dimension_semantics=("parallel", "parallel"), vmem_limit_bytes=VMEM_LIMIT),
        name="pre_mixer",
    )(h, cos_t, sin_t, lw["g_mix"], lw["w_rnn"], lw["w_lat"], lw["g_q"], lw["g_kv"], lw["w_uq"],
      lw["w_uk"], lw["w_uv"], lw["g_qh"], lw["g_kn"], lw["g_kr"])


def _attn_kernel(qi_ref, kj_ref, qt_ref, k_ref, vt_ref, o_ref, m_sc, acc_sc, *, tq):
    qi = qi_ref[pl.program_id(1)]
    kj = kj_ref[pl.program_id(1)]

    @pl.when(kj == 0)
    def _():
        m_sc[...] = jnp.full(m_sc.shape, NEG_INF, F32)
        acc_sc[...] = jnp.zeros(acc_sc.shape, F32)

    nseq = qt_ref.shape[0]

    def step(diagonal):
        if diagonal:
            keep = (lax.broadcasted_iota(jnp.int32, (tq, tq), 0)
                    <= lax.broadcasted_iota(jnp.int32, (tq, tq), 1))
        for si in range(nseq):
            sts = [jnp.dot(k_ref[si, hh], qt_ref[si, hh], preferred_element_type=F32)
                   for hh in range(N_HEADS)]
            for hh in range(N_HEADS):
                st = jnp.where(keep, sts[hh], NEG_INF) if diagonal else sts[hh]
                m_prev = m_sc[si, hh]
                m_new = jnp.maximum(m_prev, jnp.max(st, axis=0, keepdims=True))
                alpha = jnp.exp2(m_prev - m_new)
                p = jnp.exp2(st - m_new).astype(BF16)
                m_sc[si, hh] = m_new
                acc_sc[si, hh] = alpha * acc_sc[si, hh] + jnp.dot(vt_ref[si, hh], p,
                                                                  preferred_element_type=F32)

    @pl.when(kj < qi)
    def _():
        step(False)

    @pl.when(kj == qi)
    def _():
        step(True)
        for si in range(nseq):
            outs = []
            for hh in range(N_HEADS):
                acc = acc_sc[si, hh]
                outs.append(acc[:V_HEAD] / acc[V_HEAD:V_HEAD + 1])
            o_ref[si] = jnp.concatenate(outs, axis=0).T.astype(BF16)


def _attention(qt, k, vt, tq):
    b, nh, tp, _ = k.shape
    nt = tp // tq
    vrows = vt.shape[2]
    ns = SEQS_PER_STEP if b % SEQS_PER_STEP == 0 else 1
    pairs = [(qi, kj) for qi in range(nt) for kj in range(qi + 1)]
    qi_tab = jnp.asarray([p[0] for p in pairs], jnp.int32)
    kj_tab = jnp.asarray([p[1] for p in pairs], jnp.int32)
    grid_spec = pltpu.PrefetchScalarGridSpec(
        num_scalar_prefetch=2,
        grid=(b // ns, len(pairs)),
        in_specs=[pl.BlockSpec((ns, nh, LANES, tq), lambda bi, p, qt_, kt_: (bi, 0, 0, qt_[p])),
                  pl.BlockSpec((ns, nh, tq, LANES), lambda bi, p, qt_, kt_: (bi, 0, kt_[p], 0)),
                  pl.BlockSpec((ns, nh, vrows, tq), lambda bi, p, qt_, kt_: (bi, 0, 0, kt_[p]))],
        out_specs=pl.BlockSpec((ns, tq, D_ATTN), lambda bi, p, qt_, kt_: (bi, qt_[p], 0)),
        scratch_shapes=[pltpu.VMEM((ns, nh, 1, tq), F32), pltpu.VMEM((ns, nh, vrows, tq), F32)],
    )
    return pl.pallas_call(
        functools.partial(_attn_kernel, tq=tq),
        grid_spec=grid_spec,
        out_shape=jax.ShapeDtypeStruct((b, tp, D_ATTN), BF16),
        compiler_params=pltpu.CompilerParams(
            dimension_semantics=("parallel", "arbitrary"), vmem_limit_bytes=VMEM_LIMIT),
        name="attention",
    )(qi_tab, kj_tab, qt, k, vt)


def _sigmoid(x):
    return 0.5 * jnp.tanh(0.5 * x) + 0.5


def _rnn_kernel(xr_ref, gt_ref, perm_ref, permt_ref, cw_ref, cb_ref, wai_ref, bai_ref, lam_ref,
                o_ref, xe_sc, gt_sc, a_sc, b_sc, h_sc, *, tt, nb, rows):
    halo = (CONV_W - 1) * nb
    grp = MXU_WIDTH // nb
    blk = grp * nb

    @pl.when(pl.program_id(0) == 0)
    def _():
        h_sc[...] = jnp.zeros(h_sc.shape, F32)
        xe_sc[pl.ds(0, halo), :] = jnp.zeros((halo, xe_sc.shape[1]), F32)

    def gather_rows(ref, g):
        return jnp.concatenate([ref[bi, g * grp:(g + 1) * grp, :] for bi in range(nb)], axis=0)

    perm = perm_ref[...]
    for g in range(tt // grp):
        xe_sc[pl.ds(halo + g * blk, blk), :] = jnp.dot(perm, gather_rows(xr_ref, g),
                                                      preferred_element_type=F32)
        gt_sc[pl.ds(g * blk, blk), :] = jnp.dot(perm, gather_rows(gt_ref, g),
                                               preferred_element_type=F32)
    lam = lam_ref[...]
    log_a_unit = -LRU_C * (jnp.maximum(-lam, 0.0) + jnp.log1p(jnp.exp(-jnp.abs(lam))))
    cw = cw_ref[...]
    for r0 in range(0, tt * nb, rows):
        xc = cb_ref[...] + xe_sc[pl.ds(r0, rows), :] * cw[0:1]
        for j in range(1, CONV_W):
            xc = xc + xe_sc[pl.ds(r0 + j * nb, rows), :] * cw[j:j + 1]
        gates = jnp.dot(xc.astype(BF16), wai_ref[...], preferred_element_type=F32) + bai_ref[...]
        r = _sigmoid(gates[:, :D_RNN])
        ig = _sigmoid(gates[:, D_RNN:])
        a = jnp.exp(r * log_a_unit)
        a_sc[pl.ds(r0, rows), :] = a
        b_sc[pl.ds(r0, rows), :] = jnp.sqrt(1.0 - a * a) * ig * xc
    xe_sc[pl.ds(0, halo), :] = xe_sc[pl.ds(tt * nb, halo), :]

    def scan_step(t, h):
        sl = pl.ds(pl.multiple_of(t * nb, nb), nb)
        h = a_sc[sl, :] * h + b_sc[sl, :]
        b_sc[sl, :] = h
        return h

    h_sc[...] = lax.fori_loop(0, tt, scan_step, h_sc[...], unroll=8)

    permt = permt_ref[...]
    for g in range(tt // grp):
        sl = pl.ds(g * blk, blk)
        y = (b_sc[sl, :] * jax.nn.gelu(gt_sc[sl, :], approximate=True)).astype(BF16)
        y = jnp.dot(permt, y, preferred_element_type=F32).astype(BF16)
        for bi in range(nb):
            o_ref[bi, g * grp:(g + 1) * grp, :] = y[bi * grp:(bi + 1) * grp]


def _rglru(xr, gt, lw, tt):
    nb, tp, c = xr.shape
    grp = MXU_WIDTH // nb
    src = (jnp.arange(MXU_WIDTH) % nb) * grp + jnp.arange(MXU_WIDTH) // nb
    perm = (src[:, None] == jnp.arange(MXU_WIDTH)[None, :]).astype(BF16)
    blk = pl.BlockSpec((nb, tt, c), lambda ti: (0, ti, 0))
    rows = math.gcd(tt * nb, 512)
    return pl.pallas_call(
        functools.partial(_rnn_kernel, tt=tt, nb=nb, rows=rows),
        grid=(tp // tt,),
        in_specs=[blk, blk, _full((MXU_WIDTH, MXU_WIDTH)), _full((MXU_WIDTH, MXU_WIDTH)),
                  _full((CONV_W, c)), _full((1, c)), _full((c, 2 * c)),
                  _full((1, 2 * c)), _full((1, c))],
        out_specs=blk,
        out_shape=jax.ShapeDtypeStruct((nb, tp, c), BF16),
        scratch_shapes=[pltpu.VMEM(((tt + CONV_W - 1) * nb, c), F32),
                        pltpu.VMEM((tt * nb, c), F32),
                        pltpu.VMEM((tt * nb, c), F32), pltpu.VMEM((tt * nb, c), F32),
                        pltpu.VMEM((nb, c), F32)],
        compiler_params=pltpu.CompilerParams(
            dimension_semantics=("arbitrary",), vmem_limit_bytes=VMEM_LIMIT),
        name="rglru",
    )(xr, gt, perm, perm.T, lw["conv_w"], lw["conv_b"], lw["w_ai"], lw["b_ai"], lw["lam"])


def _post_kernel(att_ref, rnn_ref, h_ref, ga_ref, gr_ref, wo_ref, gf_ref, *rest, with_router):
    if with_router:
        wr_ref, ho_ref, u_ref, lg_ref = rest
    else:
        ho_ref, u_ref = rest
    an = (_rms(att_ref[0].astype(F32)) * ga_ref[...]).astype(BF16)
    rn = (_rms(rnn_ref[0].astype(F32)) * gr_ref[...]).astype(BF16)
    mix = jnp.concatenate([an, rn], axis=-1)
    hn = h_ref[0] + jnp.dot(mix, wo_ref[...], preferred_element_type=F32)
    ho_ref[0] = hn
    u = _rms(hn) * gf_ref[...]
    u_ref[0] = u.astype(BF16)
    if with_router:
        w = wr_ref[...]
        u_hi = u.astype(BF16)
        w_hi = w.astype(BF16)
        u_lo = (u - u_hi.astype(F32)).astype(BF16)
        w_lo = (w - w_hi.astype(F32)).astype(BF16)
        lg_ref[0] = (jnp.dot(u_hi, w_hi, preferred_element_type=F32)
                     + jnp.dot(u_lo, w_hi, preferred_element_type=F32)
                     + jnp.dot(u_hi, w_lo, preferred_element_type=F32))


def _post_mixer(att, rnn, h, lw, tm, w_router=None):
    b, tp, d = h.shape
    with_router = w_router is not None
    row = lambda c: pl.BlockSpec((1, tm, c), lambda bi, ti: (bi, ti, 0))
    in_specs = [row(D_ATTN), row(D_RNN), row(d),
                _full((1, D_ATTN)), _full((1, D_RNN)), _full((d, d)), _full((1, d))]
    out_specs = [row(d), row(d)]
    out_shape = [jax.ShapeDtypeStruct((b, tp, d), F32), jax.ShapeDtypeStruct((b, tp, d), BF16)]
    args = [att, rnn, h, lw["g_att"], lw["g_rnn"], lw["w_out"], lw["g_ffn"]]
    if with_router:
        in_specs.append(_full((d, LANES)))
        out_specs.append(row(LANES))
        out_shape.append(jax.ShapeDtypeStruct((b, tp, LANES), F32))
        args.append(w_router)
    return pl.pallas_call(
        functools.partial(_post_kernel, with_router=with_router),
        grid=(b, tp // tm),
        in_specs=in_specs, out_specs=out_specs, out_shape=out_shape,
        compiler_params=pltpu.CompilerParams(
            dimension_semantics=("parallel", "parallel"), vmem_limit_bytes=VMEM_LIMIT),
        name="post_mixer",
    )(*args)


def _swiglu_tile(u, wg_ref, wu_ref, wd_ref, act_sc):
    f = act_sc.shape[-1]
    full = f - f % MXU_WIDTH
    for c0 in range(0, full, MXU_WIDTH):
        g = jnp.dot(u, wg_ref[:, c0:c0 + MXU_WIDTH], preferred_element_type=F32)
        up = jnp.dot(u, wu_ref[:, c0:c0 + MXU_WIDTH], preferred_element_type=F32)
        act_sc[:, c0:c0 + MXU_WIDTH] = (jax.nn.silu(g) * up).astype(BF16)
    if full < f:
        w = jnp.concatenate([wg_ref[:, full:], wu_ref[:, full:]], axis=1)
        gu = jnp.dot(u, w, preferred_element_type=F32)
        act_sc[:, full:] = (jax.nn.silu(gu[:, :f - full]) * gu[:, f - full:]).astype(BF16)
    return jnp.dot(act_sc[...], wd_ref[...], preferred_element_type=F32)


def _ffn_kernel(u_ref, h_ref, wg_ref, wu_ref, wd_ref, o_ref, act_sc):
    o_ref[...] = h_ref[...] + _swiglu_tile(u_ref[...], wg_ref, wu_ref, wd_ref, act_sc)


def _dense_ffn(u, h, wg, wu, wd, tm):
    n, d = h.shape
    f = wd.shape[0]
    row = pl.BlockSpec((tm, d), lambda i: (i, 0))
    once = pl.Buffered(1)
    return pl.pallas_call(
        _ffn_kernel,
        grid=(n // tm,),
        in_specs=[row, row,
                  pl.BlockSpec((d, f), lambda i: (0, 0), pipeline_mode=once),
                  pl.BlockSpec((d, f), lambda i: (0, 0), pipeline_mode=once),
                  pl.BlockSpec((f, d), lambda i: (0, 0), pipeline_mode=once)],
        out_specs=row,
        out_shape=jax.ShapeDtypeStruct((n, d), F32),
        scratch_shapes=[pltpu.VMEM((tm, f), BF16)],
        compiler_params=pltpu.CompilerParams(
            dimension_semantics=("parallel",), vmem_limit_bytes=VMEM_LIMIT),
        name="dense_ffn",
    )(u, h, wg, wu, wd)


def _moe_kernel(te_ref, nt_ref, x_ref, wg_ref, wu_ref, wd_ref, o_ref, act_sc):
    @pl.when(pl.program_id(0) < nt_ref[0])
    def _():
        o_ref[...] = _swiglu_tile(x_ref[...], wg_ref.at[0], wu_ref.at[0], wd_ref.at[0], act_sc)

    @pl.when(pl.program_id(0) >= nt_ref[0])
    def _():
        o_ref[...] = jnp.zeros(o_ref.shape, F32)


def _grouped_ffn(tile_expert, n_tiles, xs, wg, wu, wd, tm):
    a, d = xs.shape
    f = wd.shape[1]
    by_expert = lambda r, c: pl.BlockSpec((1, r, c), lambda i, te, nt: (te[i], 0, 0))
    grid_spec = pltpu.PrefetchScalarGridSpec(
        num_scalar_prefetch=2,
        grid=(a // tm,),
        in_specs=[pl.BlockSpec((tm, d), lambda i, te, nt: (i, 0)),
                  by_expert(d, f), by_expert(d, f), by_expert(f, d)],
        out_specs=pl.BlockSpec((tm, d), lambda i, te, nt: (i, 0)),
        scratch_shapes=[pltpu.VMEM((tm, f), BF16)],
    )
    return pl.pallas_call(
        _moe_kernel,
        grid_spec=grid_spec,
        out_shape=jax.ShapeDtypeStruct((a, d), F32),
        compiler_params=pltpu.CompilerParams(
            dimension_semantics=("arbitrary",), vmem_limit_bytes=VMEM_LIMIT),
        name="moe_ffn",
    )(tile_expert, n_tiles, xs, wg, wu, wd)


def _moe_layer(u, h, logits, wg, wu, wd, tm):
    n, d = h.shape
    n2 = TOP_K * n
    experts = jnp.arange(N_EXPERTS, dtype=jnp.int32)
    top_val, top_idx = lax.top_k(logits[:, :N_EXPERTS], TOP_K)
    top_w = jax.nn.softmax(top_val, axis=-1)
    e_flat = top_idx.reshape(-1).astype(jnp.int32)
    onehot = (e_flat[:, None] == experts[None, :]).astype(jnp.int32)
    rank = jnp.sum((jnp.cumsum(onehot, axis=0) - onehot) * onehot, axis=-1)
    counts = jnp.sum(onehot, axis=0)
    padded = ((counts + tm - 1) // tm) * tm
    ends = jnp.cumsum(padded)
    starts = ends - padded
    pos = jnp.sum(onehot * starts[None, :], axis=-1) + rank
    order = jnp.sort(e_flat * n2 + jnp.arange(n2, dtype=jnp.int32)) % n2
    a_pad = (n2 // tm + N_EXPERTS) * tm
    slot = jnp.arange(a_pad, dtype=jnp.int32)
    slot_onehot = (jnp.sum((slot[:, None] >= ends[None, :]).astype(jnp.int32), axis=-1,
                           keepdims=True) == experts[None, :]).astype(jnp.int32)
    r = slot - jnp.sum(slot_onehot * starts[None, :], axis=-1)
    valid = r < jnp.sum(slot_onehot * counts[None, :], axis=-1)
    dense = jnp.sum(slot_onehot * (jnp.cumsum(counts) - counts)[None, :], axis=-1) + r
    slot_tok = jnp.where(valid, jnp.take(order, jnp.clip(dense, 0, n2 - 1)) // TOP_K, 0)
    tile_start = jnp.arange(a_pad // tm, dtype=jnp.int32) * tm
    tile_expert = jnp.minimum(jnp.sum((tile_start[:, None] >= ends[None, :]).astype(jnp.int32),
                                      axis=-1), N_EXPERTS - 1).astype(jnp.int32)
    n_tiles = (ends[-1] // tm).astype(jnp.int32).reshape(1)
    xs = jnp.take(u, slot_tok, axis=0)
    y = _grouped_ffn(tile_expert, n_tiles, xs, wg, wu, wd, tm)
    pos2 = pos.reshape(n, TOP_K)
    return (h + top_w[:, 0:1] * jnp.take(y, pos2[:, 0], axis=0)
            + top_w[:, 1:2] * jnp.take(y, pos2[:, 1], axis=0))


def _block_diag(w):
    nb, c, _ = w.shape
    eye = jnp.eye(nb, dtype=w.dtype)
    return (eye[:, None, :, None] * w[:, :, None, :]).reshape(nb * c, nb * c)


def _lane_table(g, ta):
    return jnp.broadcast_to(g[:, None], (g.shape[0], ta))


def _layer_weights(l, p, ta):
    w_in = p["w_in"][l]
    w_uq = p["w_uq"][l].reshape(Q_LORA, N_HEADS, QK_HEAD)
    w_uq = jnp.pad(w_uq, ((0, 0), (0, 0), (0, LANES - QK_HEAD))).reshape(Q_LORA, N_HEADS * LANES)
    w_ukv = p["w_ukv"][l].reshape(KV_LORA, N_HEADS, QK_NOPE + V_HEAD)
    w_uk = w_ukv[:, :, :QK_NOPE].reshape(KV_LORA, N_HEADS * QK_NOPE)
    w_uv = w_ukv[:, :, QK_NOPE:].reshape(KV_LORA, N_HEADS * V_HEAD)
    q_gain = p["q_head_g"][l] * (QK_HEAD ** -0.5 * math.log2(math.e))
    return {
        "g_mix": p["norm_mix_g"][l].reshape(1, -1),
        "w_rnn": w_in[:, D_LATENT:].astype(BF16),
        "w_lat": w_in[:, :D_LATENT].T.astype(BF16),
        "g_q": _lane_table(p["q_norm_g"][l], ta),
        "g_kv": _lane_table(p["kv_norm_g"][l], ta),
        "w_uq": w_uq.T.astype(BF16),
        "w_uk": w_uk.T.astype(BF16),
        "w_uv": w_uv.T.astype(BF16),
        "g_qh": _lane_table(q_gain, ta),
        "g_kn": _lane_table(p["k_head_g"][l][:QK_NOPE], ta),
        "g_kr": _lane_table(p["k_head_g"][l][QK_NOPE:], ta),
        "conv_w": p["conv_w"][l],
        "conv_b": p["conv_b"][l].reshape(1, -1),
        "w_ai": jnp.concatenate([_block_diag(p["w_a"][l]), _block_diag(p["w_i"][l])],
                                axis=1).astype(BF16),
        "b_ai": jnp.concatenate([p["b_a"][l], p["b_i"][l]]).reshape(1, -1),
        "lam": p["lru_lambda"][l].reshape(1, -1),
        "g_att": p["attn_out_g"][l].reshape(1, -1),
        "g_rnn": p["rnn_out_g"][l].reshape(1, -1),
        "w_out": p["w_out"][l].astype(BF16),
        "g_ffn": p["norm_ffn_g"][l].reshape(1, -1),
    }


def _rope_tables(pos_full):
    inv_freq = ROPE_BASE ** (-jnp.arange(0, QK_ROPE, 2, dtype=F32) / QK_ROPE)
    ang = pos_full.astype(F32)[:, None, :] * inv_freq[None, :, None]
    return jnp.cos(ang), jnp.sin(ang)


def _largest_tile(total, cap, unit):
    best = unit
    for cand in range(unit, cap + 1, unit):
        if total % cand == 0:
            best = cand
    return best


def _pick_tiles(t, b):
    tq = 384 if t > 1024 else LANES
    tp = -(-t // tq) * tq
    tm_post = _largest_tile(tp, 768, 4 * BF16_ROWS)
    tm_ffn = _largest_tile(b * tp, 1024, 256)
    return tp, tq, tm_post, tm_ffn


def kernel(x, positions, meta_tokens, norm_mix_g, w_in, q_norm_g, w_uq, kv_norm_g, w_ukv, q_head_g, k_head_g, conv_w, conv_b, w_a, b_a, w_i, b_i, lru_lambda, attn_out_g, rnn_out_g, w_out, norm_ffn_g, ffn_w_gate, ffn_w_up, ffn_w_down, router_w, moe_w_gate, moe_w_up, moe_w_down):
    p = dict(norm_mix_g=norm_mix_g, w_in=w_in, q_norm_g=q_norm_g, w_uq=w_uq, kv_norm_g=kv_norm_g,
             w_ukv=w_ukv, q_head_g=q_head_g, k_head_g=k_head_g, conv_w=conv_w, conv_b=conv_b,
             w_a=w_a, b_a=b_a, w_i=w_i, b_i=b_i, lru_lambda=lru_lambda, attn_out_g=attn_out_g,
             rnn_out_g=rnn_out_g, w_out=w_out, norm_ffn_g=norm_ffn_g)
    b, seq, d = x.shape
    depth = w_in.shape[0]
    t = N_META + seq
    tp, tq, tm_post, tm_ffn = _pick_tiles(t, b)
    n = b * tp

    meta = jnp.broadcast_to(meta_tokens[None].astype(x.dtype), (b, N_META, d))
    h = jnp.concatenate([meta, x, jnp.zeros((b, tp - t, d), x.dtype)], axis=1)
    meta_pos = jnp.broadcast_to(jnp.arange(N_META, dtype=jnp.int32)[None, :], (b, N_META))
    pos_full = jnp.concatenate([meta_pos, positions.astype(jnp.int32) + N_META,
                                jnp.zeros((b, tp - t), jnp.int32)], axis=1)
    cos_t, sin_t = _rope_tables(pos_full)

    for l in range(depth):
        lw = _layer_weights(l, p, tq)
        qt, k, vt, xr, gt = _pre_mixer(h, cos_t, sin_t, lw, tq)
        att = _attention(qt, k, vt, tq)
        rnn = _rglru(xr, gt, lw, tq)
        j = l // 2
        if l % 2 == 0:
            h, u = _post_mixer(att, rnn, h, lw, tm_post)
            h = _dense_ffn(u.reshape(n, d), h.reshape(n, d), ffn_w_gate[j].astype(BF16),
                           ffn_w_up[j].astype(BF16), ffn_w_down[j].astype(BF16), tm_ffn)
        else:
            w_r = jnp.pad(router_w[j], ((0, 0), (0, LANES - N_EXPERTS)))
            h, u, logits = _post_mixer(att, rnn, h, lw, tm_post, w_router=w_r)
            h = _moe_layer(u.reshape(n, d), h.reshape(n, d), logits.reshape(n, LANES),
                           moe_w_gate[j].astype(BF16), moe_w_up[j].astype(BF16),
                           moe_w_down[j].astype(BF16), min(tm_ffn, 512))
        h = h.reshape(b, tp, d)
    return h[:, N_META:t]
```

```python
import functools
import math

import jax
import jax.numpy as jnp
from jax import lax
from jax.experimental import pallas as pl
from jax.experimental.pallas import tpu as pltpu

F32 = jnp.float32
BF16 = jnp.bfloat16

D_MODEL = 1024
N_META = 16
D_ATTN = D_MODEL // 2
D_RNN = D_MODEL - D_ATTN
N_HEADS = 8
QK_NOPE = 64
QK_ROPE = 32
QK_HEAD = QK_NOPE + QK_ROPE
V_HEAD = D_ATTN // N_HEADS
Q_LORA = 384
KV_LORA = 256
ROPE_BASE = 10000.0
N_RNN_BLOCKS = 8
RNN_BLOCK = D_RNN // N_RNN_BLOCKS
CONV_W = 4
LRU_C = 8.0
N_EXPERTS = 8
TOP_K = 2
EPS = 1e-6

LANES = 128
SUBLANES = 8
MXU_WIDTH = 256
BF16_ROWS = 16
HALF_ROPE = QK_ROPE // 2
V_ROWS = V_HEAD + BF16_ROWS
D_LATENT = Q_LORA + KV_LORA + QK_ROPE
SEQS_PER_STEP = 2
VMEM_LIMIT = 56 * 1024 * 1024
NEG_INF = -1e30


def _rms(x):
    return x * lax.rsqrt(jnp.mean(x * x, axis=-1, keepdims=True) + EPS)


def _rms_rows(x):
    return x * lax.rsqrt(jnp.mean(x * x, axis=0, keepdims=True) + EPS)


def _full(shape):
    return pl.BlockSpec(shape, lambda *_: (0,) * len(shape))


def _pre_kernel(h_ref, cos_ref, sin_ref, gmix_ref, wrnn_ref, wlat_ref, gq_ref, gkv_ref, wuq_ref,
                wuk_ref, wuv_ref, gqh_ref, gkn_ref, gkr_ref, q_ref, k_ref, v_ref, xr_ref, gt_ref):
    u = (_rms(h_ref[0]) * gmix_ref[...]).astype(BF16)
    rnn_in = jnp.dot(u, wrnn_ref[...], preferred_element_type=F32)
    xr_ref[0] = rnn_in[:, :D_RNN].astype(BF16)
    gt_ref[0] = rnn_in[:, D_RNN:].astype(BF16)

    lat = lax.dot_general(wlat_ref[...], u, (((1,), (1,)), ((), ())), preferred_element_type=F32)
    cqn = (_rms_rows(lat[:Q_LORA]) * gq_ref[...]).astype(BF16)
    ckvn = (_rms_rows(lat[Q_LORA:Q_LORA + KV_LORA]) * gkv_ref[...]).astype(BF16)
    kr = lat[Q_LORA + KV_LORA:]
    qt = jnp.dot(wuq_ref[...], cqn, preferred_element_type=F32)
    kt = jnp.dot(wuk_ref[...], ckvn, preferred_element_type=F32)
    vt = jnp.dot(wuv_ref[...], ckvn, preferred_element_type=F32)
    ta = vt.shape[1]
    ones = jnp.ones((V_ROWS - V_HEAD, ta), BF16)
    for hh in range(N_HEADS):
        v_ref[0, hh, :V_HEAD, :] = vt[hh * V_HEAD:(hh + 1) * V_HEAD].astype(BF16)
        v_ref[0, hh, V_HEAD:, :] = ones

    cos, sin = cos_ref[0], sin_ref[0]
    inv_dh = 1.0 / QK_HEAD
    r1, r2 = QK_NOPE, QK_NOPE + HALF_ROPE
    gqh = gqh_ref[...]
    zeros_q = jnp.zeros((LANES - QK_HEAD, ta), BF16)
    for hh in range(N_HEADS):
        x = qt[hh * LANES:(hh + 1) * LANES]
        inv = lax.rsqrt(jnp.sum(x * x, axis=0, keepdims=True) * inv_dh + EPS)
        xn = x[:QK_HEAD] * gqh * inv
        x1, x2 = xn[r1:r2], xn[r2:]
        q_ref[0, hh, :r1, :] = xn[:r1].astype(BF16)
        q_ref[0, hh, r1:r2, :] = (x1 * cos - x2 * sin).astype(BF16)
        q_ref[0, hh, r2:QK_HEAD, :] = (x2 * cos + x1 * sin).astype(BF16)
        q_ref[0, hh, QK_HEAD:, :] = zeros_q

    kr_ss = jnp.sum(kr * kr, axis=0, keepdims=True)
    krg = kr * gkr_ref[...]
    k1, k2 = krg[:HALF_ROPE], krg[HALF_ROPE:]
    kr_rot = jnp.concatenate([k1 * cos - k2 * sin, k2 * cos + k1 * sin,
                              jnp.zeros((LANES - QK_HEAD, ta), F32)], axis=0)
    gkn = gkn_ref[...]
    for hh in range(N_HEADS):
        x = kt[hh * QK_NOPE:(hh + 1) * QK_NOPE]
        inv = lax.rsqrt((jnp.sum(x * x, axis=0, keepdims=True) + kr_ss) * inv_dh + EPS)
        kh = jnp.concatenate([x * gkn, kr_rot], axis=0) * inv
        k_ref[0, hh] = kh.T.astype(BF16)


def _pre_mixer(h, cos_t, sin_t, lw, ta):
    b, tp, d = h.shape
    tok_rows = lambda r: pl.BlockSpec((1, r, ta), lambda bi, ti: (bi, 0, ti))
    head_t = lambda r: pl.BlockSpec((1, N_HEADS, r, ta), lambda bi, ti: (bi, 0, 0, ti))
    rnn_rows = pl.BlockSpec((1, ta, D_RNN), lambda bi, ti: (bi, ti, 0))
    return pl.pallas_call(
        _pre_kernel,
        grid=(b, tp // ta),
        in_specs=[pl.BlockSpec((1, ta, d), lambda bi, ti: (bi, ti, 0)),
                  tok_rows(HALF_ROPE), tok_rows(HALF_ROPE),
                  _full((1, d)), _full((d, 2 * D_RNN)), _full((D_LATENT, d)),
                  _full((Q_LORA, ta)), _full((KV_LORA, ta)),
                  _full((N_HEADS * LANES, Q_LORA)), _full((N_HEADS * QK_NOPE, KV_LORA)),
                  _full((N_HEADS * V_HEAD, KV_LORA)),
                  _full((QK_HEAD, ta)), _full((QK_NOPE, ta)), _full((QK_ROPE, ta))],
        out_specs=[head_t(LANES),
                   pl.BlockSpec((1, N_HEADS, ta, LANES), lambda bi, ti: (bi, 0, ti, 0)),
                   head_t(V_ROWS), rnn_rows, rnn_rows],
        out_shape=[jax.ShapeDtypeStruct((b, N_HEADS, LANES, tp), BF16),
                   jax.ShapeDtypeStruct((b, N_HEADS, tp, LANES), BF16),
                   jax.ShapeDtypeStruct((b, N_HEADS, V_ROWS, tp), BF16),
                   jax.ShapeDtypeStruct((b, tp, D_RNN), BF16),
                   jax.ShapeDtypeStruct((b, tp, D_RNN), BF16)],
        compiler_params=pltpu.CompilerParams(
            dimension_semantics=("parallel", "parallel"), vmem_limit_bytes=VMEM_LIMIT),
        name="pre_mixer",
    )(h, cos_t, sin_t, lw["g_mix"], lw["w_rnn"], lw["w_lat"], lw["g_q"], lw["g_kv"], lw["w_uq"],
      lw["w_uk"], lw["w_uv"], lw["g_qh"], lw["g_kn"], lw["g_kr"])


def _attn_kernel(qi_ref, kj_ref, qt_ref, k_ref, vt_ref, o_ref, m_sc, acc_sc, *, tq):
    qi = qi_ref[pl.program_id(1)]
    kj = kj_ref[pl.program_id(1)]

    @pl.when(kj == 0)
    def _():
        m_sc[...] = jnp.full(m_sc.shape, NEG_INF, F32)
        acc_sc[...] = jnp.zeros(acc_sc.shape, F32)

    nseq = qt_ref.shape[0]

    def step(diagonal):
        if diagonal:
            keep = (lax.broadcasted_iota(jnp.int32, (tq, tq), 0)
                    <= lax.broadcasted_iota(jnp.int32, (tq, tq), 1))
        for si in range(nseq):
            sts = [jnp.dot(k_ref[si, hh], qt_ref[si, hh], preferred_element_type=F32)
                   for hh in range(N_HEADS)]
            for hh in range(N_HEADS):
                st = jnp.where(keep, sts[hh], NEG_INF) if diagonal else sts[hh]
                m_prev = m_sc[si, hh]
                m_new = jnp.maximum(m_prev, jnp.max(st, axis=0, keepdims=True))
                alpha = jnp.exp2(m_prev - m_new)
                p = jnp.exp2(st - m_new).astype(BF16)
                m_sc[si, hh] = m_new
                acc_sc[si, hh] = alpha * acc_sc[si, hh] + jnp.dot(vt_ref[si, hh], p,
                                                                  preferred_element_type=F32)

    @pl.when(kj < qi)
    def _():
        step(False)

    @pl.when(kj == qi)
    def _():
        step(True)
        for si in range(nseq):
            outs = []
            for hh in range(N_HEADS):
                acc = acc_sc[si, hh]
                outs.append(acc[:V_HEAD] / acc[V_HEAD:V_HEAD + 1])
            o_ref[si] = jnp.concatenate(outs, axis=0).T.astype(BF16)


def _attention(qt, k, vt, tq):
    b, nh, tp, _ = k.shape
    nt = tp // tq
    vrows = vt.shape[2]
    ns = SEQS_PER_STEP if b % SEQS_PER_STEP == 0 else 1
    pairs = [(qi, kj) for qi in range(nt) for kj in range(qi + 1)]
    qi_tab = jnp.asarray([p[0] for p in pairs], jnp.int32)
    kj_tab = jnp.asarray([p[1] for p in pairs], jnp.int32)
    grid_spec = pltpu.PrefetchScalarGridSpec(
        num_scalar_prefetch=2,
        grid=(b // ns, len(pairs)),
        in_specs=[pl.BlockSpec((ns, nh, LANES, tq), lambda bi, p, qt_, kt_: (bi, 0, 0, qt_[p])),
                  pl.BlockSpec((ns, nh, tq, LANES), lambda bi, p, qt_, kt_: (bi, 0, kt_[p], 0)),
                  pl.BlockSpec((ns, nh, vrows, tq), lambda bi, p, qt_, kt_: (bi, 0, 0, kt_[p]))],
        out_specs=pl.BlockSpec((ns, tq, D_ATTN), lambda bi, p, qt_, kt_: (bi, qt_[p], 0)),
        scratch_shapes=[pltpu.VMEM((ns, nh, 1, tq), F32), pltpu.VMEM((ns, nh, vrows, tq), F32)],
    )
    return pl.pallas_call(
        functools.partial(_attn_kernel, tq=tq),
        grid_spec=grid_spec,
        out_shape=jax.ShapeDtypeStruct((b, tp, D_ATTN), BF16),
        compiler_params=pltpu.CompilerParams(
            dimension_semantics=("parallel", "arbitrary"), vmem_limit_bytes=VMEM_LIMIT),
        name="attention",
    )(qi_tab, kj_tab, qt, k, vt)


def _sigmoid(x):
    return 0.5 * jnp.tanh(0.5 * x) + 0.5


def _rnn_kernel(xr_ref, gt_ref, perm_ref, permt_ref, cw_ref, cb_ref, wai_ref, bai_ref, lam_ref,
                o_ref, xe_sc, gt_sc, a_sc, b_sc, h_sc, *, tt, nb, rows):
    halo = (CONV_W - 1) * nb
    grp = MXU_WIDTH // nb
    blk = grp * nb

    @pl.when(pl.program_id(0) == 0)
    def _():
        h_sc[...] = jnp.zeros(h_sc.shape, F32)
        xe_sc[pl.ds(0, halo), :] = jnp.zeros((halo, xe_sc.shape[1]), F32)

    def gather_rows(ref, g):
        return jnp.concatenate([ref[bi, g * grp:(g + 1) * grp, :] for bi in range(nb)], axis=0)

    perm = perm_ref[...]
    for g in range(tt // grp):
        xe_sc[pl.ds(halo + g * blk, blk), :] = jnp.dot(perm, gather_rows(xr_ref, g),
                                                      preferred_element_type=F32)
        gt_sc[pl.ds(g * blk, blk), :] = jnp.dot(perm, gather_rows(gt_ref, g),
                                               preferred_element_type=F32)
    lam = lam_ref[...]
    log_a_unit = -LRU_C * (jnp.maximum(-lam, 0.0) + jnp.log1p(jnp.exp(-jnp.abs(lam))))
    cw = cw_ref[...]
    for r0 in range(0, tt * nb, rows):
        xc = cb_ref[...] + xe_sc[pl.ds(r0, rows), :] * cw[0:1]
        for j in range(1, CONV_W):
            xc = xc + xe_sc[pl.ds(r0 + j * nb, rows), :] * cw[j:j + 1]
        gates = jnp.dot(xc.astype(BF16), wai_ref[...], preferred_element_type=F32) + bai_ref[...]
        r = _sigmoid(gates[:, :D_RNN])
        ig = _sigmoid(gates[:, D_RNN:])
        a = jnp.exp(r * log_a_unit)
        a_sc[pl.ds(r0, rows), :] = a
        b_sc[pl.ds(r0, rows), :] = jnp.sqrt(1.0 - a * a) * ig * xc
    xe_sc[pl.ds(0, halo), :] = xe_sc[pl.ds(tt * nb, halo), :]

    def scan_step(t, h):
        sl = pl.ds(pl.multiple_of(t * nb, nb), nb)
        h = a_sc[sl, :] * h + b_sc[sl, :]
        b_sc[sl, :] = h
        return h

    h_sc[...] = lax.fori_loop(0, tt, scan_step, h_sc[...], unroll=8)

    permt = permt_ref[...]
    for g in range(tt // grp):
        sl = pl.ds(g * blk, blk)
        y = (b_sc[sl, :] * jax.nn.gelu(gt_sc[sl, :], approximate=True)).astype(BF16)
        y = jnp.dot(permt, y, preferred_element_type=F32).astype(BF16)
        for bi in range(nb):
            o_ref[bi, g * grp:(g + 1) * grp, :] = y[bi * grp:(bi + 1) * grp]


def _rglru(xr, gt, lw, tt):
    nb, tp, c = xr.shape
    grp = MXU_WIDTH // nb
    src = (jnp.arange(MXU_WIDTH) % nb) * grp + jnp.arange(MXU_WIDTH) // nb
    perm = (src[:, None] == jnp.arange(MXU_WIDTH)[None, :]).astype(BF16)
    blk = pl.BlockSpec((nb, tt, c), lambda ti: (0, ti, 0))
    rows = math.gcd(tt * nb, 512)
    return pl.pallas_call(
        functools.partial(_rnn_kernel, tt=tt, nb=nb, rows=rows),
        grid=(tp // tt,),
        in_specs=[blk, blk, _full((MXU_WIDTH, MXU_WIDTH)), _full((MXU_WIDTH, MXU_WIDTH)),
                  _full((CONV_W, c)), _full((1, c)), _full((c, 2 * c)),
                  _full((1, 2 * c)), _full((1, c))],
        out_specs=blk,
        out_shape=jax.ShapeDtypeStruct((nb, tp, c), BF16),
        scratch_shapes=[pltpu.VMEM(((tt + CONV_W - 1) * nb, c), F32),
                        pltpu.VMEM((tt * nb, c), F32),
                        pltpu.VMEM((tt * nb, c), F32), pltpu.VMEM((tt * nb, c), F32),
                        pltpu.VMEM((nb, c), F32)],
        compiler_params=pltpu.CompilerParams(
            dimension_semantics=("arbitrary",), vmem_limit_bytes=VMEM_LIMIT),
        name="rglru",
    )(xr, gt, perm, perm.T, lw["conv_w"], lw["conv_b"], lw["w_ai"], lw["b_ai"], lw["lam"])


def _post_kernel(att_ref, rnn_ref, h_ref, ga_ref, gr_ref, wo_ref, gf_ref, *rest, with_router):
    if with_router:
        wr_ref, ho_ref, u_ref, lg_ref = rest
    else:
        ho_ref, u_ref = rest
    an = (_rms(att_ref[0].astype(F32)) * ga_ref[...]).astype(BF16)
    rn = (_rms(rnn_ref[0].astype(F32)) * gr_ref[...]).astype(BF16)
    mix = jnp.concatenate([an, rn], axis=-1)
    hn = h_ref[0] + jnp.dot(mix, wo_ref[...], preferred_element_type=F32)
    ho_ref[0] = hn
    u = _rms(hn) * gf_ref[...]
    u_ref[0] = u.astype(BF16)
    if with_router:
        w = wr_ref[...]
        u_hi = u.astype(BF16)
        w_hi = w.astype(BF16)
        u_lo = (u - u_hi.astype(F32)).astype(BF16)
        w_lo = (w - w_hi.astype(F32)).astype(BF16)
        hh_hl = jnp.dot(u_hi, jnp.concatenate([w_hi, w_lo], axis=1), preferred_element_type=F32)
        lg_ref[0] = (hh_hl[:, :LANES] + hh_hl[:, LANES:]
                     + jnp.dot(u_lo, w_hi, preferred_element_type=F32))


def _post_mixer(att, rnn, h, lw, tm, w_router=None):
    b, tp, d = h.shape
    with_router = w_router is not None
    row = lambda c: pl.BlockSpec((1, tm, c), lambda bi, ti: (bi, ti, 0))
    in_specs = [row(D_ATTN), row(D_RNN), row(d),
                _full((1, D_ATTN)), _full((1, D_RNN)), _full((d, d)), _full((1, d))]
    out_specs = [row(d), row(d)]
    out_shape = [jax.ShapeDtypeStruct((b, tp, d), F32), jax.ShapeDtypeStruct((b, tp, d), BF16)]
    args = [att, rnn, h, lw["g_att"], lw["g_rnn"], lw["w_out"], lw["g_ffn"]]
    if with_router:
        in_specs.append(_full((d, LANES)))
        out_specs.append(row(LANES))
        out_shape.append(jax.ShapeDtypeStruct((b, tp, LANES), F32))
        args.append(w_router)
    return pl.pallas_call(
        functools.partial(_post_kernel, with_router=with_router),
        grid=(b, tp // tm),
        in_specs=in_specs, out_specs=out_specs, out_shape=out_shape,
        compiler_params=pltpu.CompilerParams(
            dimension_semantics=("parallel", "parallel"), vmem_limit_bytes=VMEM_LIMIT),
        name="post_mixer",
    )(*args)


def _swiglu_tile(u, wg_ref, wu_ref, wd_ref, act_sc):
    f = act_sc.shape[-1]
    full = f - f % MXU_WIDTH
    for c0 in range(0, full, MXU_WIDTH):
        g = jnp.dot(u, wg_ref[:, c0:c0 + MXU_WIDTH], preferred_element_type=F32)
        up = jnp.dot(u, wu_ref[:, c0:c0 + MXU_WIDTH], preferred_element_type=F32)
        act_sc[:, c0:c0 + MXU_WIDTH] = (jax.nn.silu(g) * up).astype(BF16)
    if full < f:
        w = jnp.concatenate([wg_ref[:, full:], wu_ref[:, full:]], axis=1)
        gu = jnp.dot(u, w, preferred_element_type=F32)
        act_sc[:, full:] = (jax.nn.silu(gu[:, :f - full]) * gu[:, f - full:]).astype(BF16)
    return jnp.dot(act_sc[...], wd_ref[...], preferred_element_type=F32)


def _ffn_kernel(u_ref, h_ref, wg_ref, wu_ref, wd_ref, o_ref, act_sc):
    o_ref[...] = h_ref[...] + _swiglu_tile(u_ref[...], wg_ref, wu_ref, wd_ref, act_sc)


def _dense_ffn(u, h, wg, wu, wd, tm):
    n, d = h.shape
    f = wd.shape[0]
    row = pl.BlockSpec((tm, d), lambda i: (i, 0))
    once = pl.Buffered(1)
    return pl.pallas_call(
        _ffn_kernel,
        grid=(n // tm,),
        in_specs=[row, row,
                  pl.BlockSpec((d, f), lambda i: (0, 0), pipeline_mode=once),
                  pl.BlockSpec((d, f), lambda i: (0, 0), pipeline_mode=once),
                  pl.BlockSpec((f, d), lambda i: (0, 0), pipeline_mode=once)],
        out_specs=row,
        out_shape=jax.ShapeDtypeStruct((n, d), F32),
        scratch_shapes=[pltpu.VMEM((tm, f), BF16)],
        compiler_params=pltpu.CompilerParams(
            dimension_semantics=("parallel",), vmem_limit_bytes=VMEM_LIMIT),
        name="dense_ffn",
    )(u, h, wg, wu, wd)


def _moe_kernel(te_ref, nt_ref, x_ref, wg_ref, wu_ref, wd_ref, o_ref, wg_sc, wu_sc, wd_sc, act_sc):
    i = pl.program_id(0)

    @pl.when((i == 0) | (te_ref[i] != te_ref[jnp.maximum(i - 1, 0)]))
    def _():
        wg_sc[...] = wg_ref[0].astype(BF16)
        wu_sc[...] = wu_ref[0].astype(BF16)
        wd_sc[...] = wd_ref[0].astype(BF16)

    @pl.when(i < nt_ref[0])
    def _():
        o_ref[...] = _swiglu_tile(x_ref[...], wg_sc, wu_sc, wd_sc, act_sc).astype(BF16)

    @pl.when(i >= nt_ref[0])
    def _():
        o_ref[...] = jnp.zeros(o_ref.shape, BF16)


def _grouped_ffn(tile_expert, n_tiles, xs, wg, wu, wd, tm):
    a, d = xs.shape
    f = wd.shape[1]
    by_expert = lambda r, c: pl.BlockSpec((1, r, c), lambda i, te, nt: (te[i], 0, 0))
    grid_spec = pltpu.PrefetchScalarGridSpec(
        num_scalar_prefetch=2,
        grid=(a // tm,),
        in_specs=[pl.BlockSpec((tm, d), lambda i, te, nt: (i, 0)),
                  by_expert(d, f), by_expert(d, f), by_expert(f, d)],
        out_specs=pl.BlockSpec((tm, d), lambda i, te, nt: (i, 0)),
        scratch_shapes=[pltpu.VMEM((d, f), BF16), pltpu.VMEM((d, f), BF16),
                        pltpu.VMEM((f, d), BF16), pltpu.VMEM((tm, f), BF16)],
    )
    return pl.pallas_call(
        _moe_kernel,
        grid_spec=grid_spec,
        out_shape=jax.ShapeDtypeStruct((a, d), BF16),
        compiler_params=pltpu.CompilerParams(
            dimension_semantics=("arbitrary",), vmem_limit_bytes=VMEM_LIMIT),
        name="moe_ffn",
    )(tile_expert, n_tiles, xs, wg, wu, wd)


def _moe_layer(u, h, logits, wg, wu, wd, tm, keep=None):
    n, d = h.shape
    n2 = TOP_K * n
    experts = jnp.arange(N_EXPERTS, dtype=jnp.int32)
    top_val, top_idx = lax.top_k(logits[:, :N_EXPERTS], TOP_K)
    top_w = jax.nn.softmax(top_val, axis=-1)
    e_flat = top_idx.reshape(-1).astype(jnp.int32)
    onehot = (e_flat[:, None] == experts[None, :]).astype(jnp.int32)
    rank = jnp.sum((jnp.cumsum(onehot, axis=0) - onehot) * onehot, axis=-1)
    counts = jnp.sum(onehot, axis=0)
    padded = ((counts + tm - 1) // tm) * tm
    ends = jnp.cumsum(padded)
    starts = ends - padded
    pos = jnp.sum(onehot * starts[None, :], axis=-1) + rank
    order = jnp.sort(e_flat * n2 + jnp.arange(n2, dtype=jnp.int32)) % n2
    a_pad = (n2 // tm + N_EXPERTS) * tm
    slot = jnp.arange(a_pad, dtype=jnp.int32)
    slot_onehot = (jnp.sum((slot[:, None] >= ends[None, :]).astype(jnp.int32), axis=-1,
                           keepdims=True) == experts[None, :]).astype(jnp.int32)
    r = slot - jnp.sum(slot_onehot * starts[None, :], axis=-1)
    valid = r < jnp.sum(slot_onehot * counts[None, :], axis=-1)
    dense = jnp.sum(slot_onehot * (jnp.cumsum(counts) - counts)[None, :], axis=-1) + r
    slot_tok = jnp.where(valid, jnp.take(order, jnp.clip(dense, 0, n2 - 1)) // TOP_K, 0)
    tile_start = jnp.arange(a_pad // tm, dtype=jnp.int32) * tm
    tile_expert = jnp.minimum(jnp.sum((tile_start[:, None] >= ends[None, :]).astype(jnp.int32),
                                      axis=-1), N_EXPERTS - 1).astype(jnp.int32)
    n_tiles = (ends[-1] // tm).astype(jnp.int32).reshape(1)
    xs = jnp.take(u, slot_tok, axis=0)
    y = _grouped_ffn(tile_expert, n_tiles, xs, wg, wu, wd, tm)
    pos2 = pos.reshape(n, TOP_K)
    if keep is not None:
        b, tp, lo, hi = keep
        rows = lambda v: v.reshape(b, tp, v.shape[-1])[:, lo:hi].reshape(b * (hi - lo), v.shape[-1])
        h, pos2, top_w = rows(h), rows(pos2), rows(top_w)
    return (h + top_w[:, 0:1] * jnp.take(y, pos2[:, 0], axis=0).astype(F32)
            + top_w[:, 1:2] * jnp.take(y, pos2[:, 1], axis=0).astype(F32))


def _block_diag(w):
    nb, c, _ = w.shape
    eye = jnp.eye(nb, dtype=w.dtype)
    return (eye[:, None, :, None] * w[:, :, None, :]).reshape(nb * c, nb * c)


def _lane_table(g, ta):
    return jnp.broadcast_to(g[:, None], (g.shape[0], ta))


def _layer_weights(l, p, ta):
    w_in = p["w_in"][l]
    w_uq = p["w_uq"][l].reshape(Q_LORA, N_HEADS, QK_HEAD)
    w_uq = jnp.pad(w_uq, ((0, 0), (0, 0), (0, LANES - QK_HEAD))).reshape(Q_LORA, N_HEADS * LANES)
    w_ukv = p["w_ukv"][l].reshape(KV_LORA, N_HEADS, QK_NOPE + V_HEAD)
    w_uk = w_ukv[:, :, :QK_NOPE].reshape(KV_LORA, N_HEADS * QK_NOPE)
    w_uv = w_ukv[:, :, QK_NOPE:].reshape(KV_LORA, N_HEADS * V_HEAD)
    q_gain = p["q_head_g"][l] * (QK_HEAD ** -0.5 * math.log2(math.e))
    return {
        "g_mix": p["norm_mix_g"][l].reshape(1, -1),
        "w_rnn": w_in[:, D_LATENT:].astype(BF16),
        "w_lat": w_in[:, :D_LATENT].T.astype(BF16),
        "g_q": _lane_table(p["q_norm_g"][l], ta),
        "g_kv": _lane_table(p["kv_norm_g"][l], ta),
        "w_uq": w_uq.T.astype(BF16),
        "w_uk": w_uk.T.astype(BF16),
        "w_uv": w_uv.T.astype(BF16),
        "g_qh": _lane_table(q_gain, ta),
        "g_kn": _lane_table(p["k_head_g"][l][:QK_NOPE], ta),
        "g_kr": _lane_table(p["k_head_g"][l][QK_NOPE:], ta),
        "conv_w": p["conv_w"][l],
        "conv_b": p["conv_b"][l].reshape(1, -1),
        "w_ai": jnp.concatenate([_block_diag(p["w_a"][l]), _block_diag(p["w_i"][l])],
                                axis=1).astype(BF16),
        "b_ai": jnp.concatenate([p["b_a"][l], p["b_i"][l]]).reshape(1, -1),
        "lam": p["lru_lambda"][l].reshape(1, -1),
        "g_att": p["attn_out_g"][l].reshape(1, -1),
        "g_rnn": p["rnn_out_g"][l].reshape(1, -1),
        "w_out": p["w_out"][l].astype(BF16),
        "g_ffn": p["norm_ffn_g"][l].reshape(1, -1),
    }


def _rope_tables(pos_full):
    inv_freq = ROPE_BASE ** (-jnp.arange(0, QK_ROPE, 2, dtype=F32) / QK_ROPE)
    ang = pos_full.astype(F32)[:, None, :] * inv_freq[None, :, None]
    return jnp.cos(ang), jnp.sin(ang)


def _largest_tile(total, cap, unit):
    best = unit
    for cand in range(unit, cap + 1, unit):
        if total % cand == 0:
            best = cand
    return best


def _pick_tiles(t, b):
    tq = 384 if t > 1024 else LANES
    tp = -(-t // tq) * tq
    tm_post = _largest_tile(tp, 768, 4 * BF16_ROWS)
    tm_ffn = _largest_tile(b * tp, 1024, 256)
    return tp, tq, tm_post, tm_ffn


def kernel(x, positions, meta_tokens, norm_mix_g, w_in, q_norm_g, w_uq, kv_norm_g, w_ukv, q_head_g, k_head_g, conv_w, conv_b, w_a, b_a, w_i, b_i, lru_lambda, attn_out_g, rnn_out_g, w_out, norm_ffn_g, ffn_w_gate, ffn_w_up, ffn_w_down, router_w, moe_w_gate, moe_w_up, moe_w_down):
    p = dict(norm_mix_g=norm_mix_g, w_in=w_in, q_norm_g=q_norm_g, w_uq=w_uq, kv_norm_g=kv_norm_g,
             w_ukv=w_ukv, q_head_g=q_head_g, k_head_g=k_head_g, conv_w=conv_w, conv_b=conv_b,
             w_a=w_a, b_a=b_a, w_i=w_i, b_i=b_i, lru_lambda=lru_lambda, attn_out_g=attn_out_g,
             rnn_out_g=rnn_out_g, w_out=w_out, norm_ffn_g=norm_ffn_g)
    b, seq, d = x.shape
    depth = w_in.shape[0]
    t = N_META + seq
    tp, tq, tm_post, tm_ffn = _pick_tiles(t, b)
    n = b * tp

    meta = jnp.broadcast_to(meta_tokens[None].astype(x.dtype), (b, N_META, d))
    h = jnp.concatenate([meta, x, jnp.zeros((b, tp - t, d), x.dtype)], axis=1)
    meta_pos = jnp.broadcast_to(jnp.arange(N_META, dtype=jnp.int32)[None, :], (b, N_META))
    pos_full = jnp.concatenate([meta_pos, positions.astype(jnp.int32) + N_META,
                                jnp.zeros((b, tp - t), jnp.int32)], axis=1)
    cos_t, sin_t = _rope_tables(pos_full)

    for l in range(depth):
        lw = _layer_weights(l, p, tq)
        qt, k, vt, xr, gt = _pre_mixer(h, cos_t, sin_t, lw, tq)
        att = _attention(qt, k, vt, tq)
        rnn = _rglru(xr, gt, lw, tq)
        j = l // 2
        if l % 2 == 0:
            h, u = _post_mixer(att, rnn, h, lw, tm_post)
            h = _dense_ffn(u.reshape(n, d), h.reshape(n, d), ffn_w_gate[j].astype(BF16),
                           ffn_w_up[j].astype(BF16), ffn_w_down[j].astype(BF16), tm_ffn)
        else:
            w_r = jnp.pad(router_w[j], ((0, 0), (0, LANES - N_EXPERTS)))
            h, u, logits = _post_mixer(att, rnn, h, lw, tm_post, w_router=w_r)
            last = l == depth - 1
            h = _moe_layer(u.reshape(n, d), h.reshape(n, d), logits.reshape(n, LANES),
                           moe_w_gate[j], moe_w_up[j], moe_w_down[j], min(tm_ffn, 512),
                           keep=(b, tp, N_META, t) if last else None)
            if last:
                return h.reshape(b, seq, d)
        h = h.reshape(b, tp, d)
    return h[:, N_META:t]
```

```python
import functools
import math

import jax
import jax.numpy as jnp
from jax import lax
from jax.experimental import pallas as pl
from jax.experimental.pallas import tpu as pltpu

F32 = jnp.float32
BF16 = jnp.bfloat16

D_MODEL = 1024
N_META = 16
D_ATTN = D_MODEL // 2
D_RNN = D_MODEL - D_ATTN
N_HEADS = 8
QK_NOPE = 64
QK_ROPE = 32
QK_HEAD = QK_NOPE + QK_ROPE
V_HEAD = D_ATTN // N_HEADS
Q_LORA = 384
KV_LORA = 256
ROPE_BASE = 10000.0
N_RNN_BLOCKS = 8
RNN_BLOCK = D_RNN // N_RNN_BLOCKS
CONV_W = 4
LRU_C = 8.0
N_EXPERTS = 8
TOP_K = 2
EPS = 1e-6

LANES = 128
SUBLANES = 8
MXU_WIDTH = 256
BF16_ROWS = 16
HALF_ROPE = QK_ROPE // 2
V_ROWS = V_HEAD + BF16_ROWS
D_LATENT = Q_LORA + KV_LORA + QK_ROPE
SEQS_PER_STEP = 4
VMEM_LIMIT = 56 * 1024 * 1024
NEG_INF = -1e30


def _rms(x):
    return x * lax.rsqrt(jnp.mean(x * x, axis=-1, keepdims=True) + EPS)


def _rms_rows(x):
    return x * lax.rsqrt(jnp.mean(x * x, axis=0, keepdims=True) + EPS)


def _full(shape):
    return pl.BlockSpec(shape, lambda *_: (0,) * len(shape))


def _pre_kernel(h_ref, cos_ref, sin_ref, gmix_ref, wrnn_ref, wlat_ref, gq_ref, gkv_ref, wuq_ref,
                wuk_ref, wuv_ref, gqh_ref, gkn_ref, gkr_ref, q_ref, k_ref, v_ref, xr_ref, gt_ref):
    for si in range(h_ref.shape[0]):
        u = (_rms(h_ref[si]) * gmix_ref[...]).astype(BF16)
        rnn_in = jnp.dot(u, wrnn_ref[...], preferred_element_type=F32)
        xr_ref[si] = rnn_in[:, :D_RNN].astype(BF16)
        gt_ref[si] = rnn_in[:, D_RNN:].astype(BF16)

        lat = lax.dot_general(wlat_ref[...], u, (((1,), (1,)), ((), ())),
                              preferred_element_type=F32)
        cqn = (_rms_rows(lat[:Q_LORA]) * gq_ref[...]).astype(BF16)
        ckvn = (_rms_rows(lat[Q_LORA:Q_LORA + KV_LORA]) * gkv_ref[...]).astype(BF16)
        kr = lat[Q_LORA + KV_LORA:]
        qt = jnp.dot(wuq_ref[...], cqn, preferred_element_type=F32)
        kt = jnp.dot(wuk_ref[...], ckvn, preferred_element_type=F32)
        vt = jnp.dot(wuv_ref[...], ckvn, preferred_element_type=F32)
        ta = vt.shape[1]
        ones = jnp.ones((V_ROWS - V_HEAD, ta), BF16)
        for hh in range(N_HEADS):
            v_ref[si, hh, :V_HEAD, :] = vt[hh * V_HEAD:(hh + 1) * V_HEAD].astype(BF16)
            v_ref[si, hh, V_HEAD:, :] = ones

        cos, sin = cos_ref[si], sin_ref[si]
        inv_dh = 1.0 / QK_HEAD
        r1, r2 = QK_NOPE, QK_NOPE + HALF_ROPE
        gqh = gqh_ref[...]
        zeros_q = jnp.zeros((LANES - QK_HEAD, ta), BF16)
        for hh in range(N_HEADS):
            x = qt[hh * LANES:(hh + 1) * LANES]
            inv = lax.rsqrt(jnp.sum(x * x, axis=0, keepdims=True) * inv_dh + EPS)
            xn = x[:QK_HEAD] * gqh * inv
            x1, x2 = xn[r1:r2], xn[r2:]
            q_ref[si, hh, :r1, :] = xn[:r1].astype(BF16)
            q_ref[si, hh, r1:r2, :] = (x1 * cos - x2 * sin).astype(BF16)
            q_ref[si, hh, r2:QK_HEAD, :] = (x2 * cos + x1 * sin).astype(BF16)
            q_ref[si, hh, QK_HEAD:, :] = zeros_q

        kr_ss = jnp.sum(kr * kr, axis=0, keepdims=True)
        krg = kr * gkr_ref[...]
        k1, k2 = krg[:HALF_ROPE], krg[HALF_ROPE:]
        kr_rot = jnp.concatenate([k1 * cos - k2 * sin, k2 * cos + k1 * sin,
                                  jnp.zeros((LANES - QK_HEAD, ta), F32)], axis=0)
        gkn = gkn_ref[...]
        for hh in range(N_HEADS):
            x = kt[hh * QK_NOPE:(hh + 1) * QK_NOPE]
            inv = lax.rsqrt((jnp.sum(x * x, axis=0, keepdims=True) + kr_ss) * inv_dh + EPS)
            kh = jnp.concatenate([x * gkn, kr_rot], axis=0) * inv
            k_ref[si, hh] = kh.T.astype(BF16)


def _pre_mixer(h, cos_t, sin_t, lw, ta):
    b, tp, d = h.shape
    ns = 2 if b % 2 == 0 else 1
    tok_rows = lambda r: pl.BlockSpec((ns, r, ta), lambda bi, ti: (bi, 0, ti))
    head_t = lambda r: pl.BlockSpec((ns, N_HEADS, r, ta), lambda bi, ti: (bi, 0, 0, ti))
    rnn_rows = pl.BlockSpec((ns, ta, D_RNN), lambda bi, ti: (bi, ti, 0))
    return pl.pallas_call(
        _pre_kernel,
        grid=(b // ns, tp // ta),
        in_specs=[pl.BlockSpec((ns, ta, d), lambda bi, ti: (bi, ti, 0)),
                  tok_rows(HALF_ROPE), tok_rows(HALF_ROPE),
                  _full((1, d)), _full((d, 2 * D_RNN)), _full((D_LATENT, d)),
                  _full((Q_LORA, ta)), _full((KV_LORA, ta)),
                  _full((N_HEADS * LANES, Q_LORA)), _full((N_HEADS * QK_NOPE, KV_LORA)),
                  _full((N_HEADS * V_HEAD, KV_LORA)),
                  _full((QK_HEAD, ta)), _full((QK_NOPE, ta)), _full((QK_ROPE, ta))],
        out_specs=[head_t(LANES),
                   pl.BlockSpec((ns, N_HEADS, ta, LANES), lambda bi, ti: (bi, 0, ti, 0)),
                   head_t(V_ROWS), rnn_rows, rnn_rows],
        out_shape=[jax.ShapeDtypeStruct((b, N_HEADS, LANES, tp), BF16),
                   jax.ShapeDtypeStruct((b, N_HEADS, tp, LANES), BF16),
                   jax.ShapeDtypeStruct((b, N_HEADS, V_ROWS, tp), BF16),
                   jax.ShapeDtypeStruct((b, tp, D_RNN), BF16),
                   jax.ShapeDtypeStruct((b, tp, D_RNN), BF16)],
        compiler_params=pltpu.CompilerParams(
            dimension_semantics=("parallel", "parallel"), vmem_limit_bytes=VMEM_LIMIT),
        name="pre_mixer",
    )(h, cos_t, sin_t, lw["g_mix"], lw["w_rnn"], lw["w_lat"], lw["g_q"], lw["g_kv"], lw["w_uq"],
      lw["w_uk"], lw["w_uv"], lw["g_qh"], lw["g_kn"], lw["g_kr"])


def _attn_kernel(qi_ref, kj_ref, qt_ref, k_ref, vt_ref, o_ref, m_sc, acc_sc, *, tq):
    qi = qi_ref[pl.program_id(1)]
    kj = kj_ref[pl.program_id(1)]

    @pl.when(kj == 0)
    def _():
        m_sc[...] = jnp.full(m_sc.shape, NEG_INF, F32)
        acc_sc[...] = jnp.zeros(acc_sc.shape, F32)

    nseq = qt_ref.shape[0]

    def step(diagonal):
        if diagonal:
            keep = (lax.broadcasted_iota(jnp.int32, (tq, tq), 0)
                    <= lax.broadcasted_iota(jnp.int32, (tq, tq), 1))
        for si in range(nseq):
            sts = [jnp.dot(k_ref[si, hh], qt_ref[si, hh], preferred_element_type=F32)
                   for hh in range(N_HEADS)]
            for hh in range(N_HEADS):
                st = jnp.where(keep, sts[hh], NEG_INF) if diagonal else sts[hh]
                m_prev = m_sc[si, hh]
                m_new = jnp.maximum(m_prev, jnp.max(st, axis=0, keepdims=True))
                alpha = jnp.exp2(m_prev - m_new)
                p = jnp.exp2(st - m_new).astype(BF16)
                m_sc[si, hh] = m_new
                acc_sc[si, hh] = alpha * acc_sc[si, hh] + jnp.dot(vt_ref[si, hh], p,
                                                                  preferred_element_type=F32)

    @pl.when(kj < qi)
    def _():
        step(False)

    @pl.when(kj == qi)
    def _():
        step(True)
        for si in range(nseq):
            outs = []
            for hh in range(N_HEADS):
                acc = acc_sc[si, hh]
                outs.append(acc[:V_HEAD] / acc[V_HEAD:V_HEAD + 1])
            o_ref[si] = jnp.concatenate(outs, axis=0).T.astype(BF16)


def _attention(qt, k, vt, tq):
    b, nh, tp, _ = k.shape
    nt = tp // tq
    vrows = vt.shape[2]
    ns = SEQS_PER_STEP if b % SEQS_PER_STEP == 0 else 1
    pairs = [(qi, kj) for qi in range(nt) for kj in range(qi + 1)]
    qi_tab = jnp.asarray([p[0] for p in pairs], jnp.int32)
    kj_tab = jnp.asarray([p[1] for p in pairs], jnp.int32)
    grid_spec = pltpu.PrefetchScalarGridSpec(
        num_scalar_prefetch=2,
        grid=(b // ns, len(pairs)),
        in_specs=[pl.BlockSpec((ns, nh, LANES, tq), lambda bi, p, qt_, kt_: (bi, 0, 0, qt_[p])),
                  pl.BlockSpec((ns, nh, tq, LANES), lambda bi, p, qt_, kt_: (bi, 0, kt_[p], 0)),
                  pl.BlockSpec((ns, nh, vrows, tq), lambda bi, p, qt_, kt_: (bi, 0, 0, kt_[p]))],
        out_specs=pl.BlockSpec((ns, tq, D_ATTN), lambda bi, p, qt_, kt_: (bi, qt_[p], 0)),
        scratch_shapes=[pltpu.VMEM((ns, nh, 1, tq), F32), pltpu.VMEM((ns, nh, vrows, tq), F32)],
    )
    return pl.pallas_call(
        functools.partial(_attn_kernel, tq=tq),
        grid_spec=grid_spec,
        out_shape=jax.ShapeDtypeStruct((b, tp, D_ATTN), BF16),
        compiler_params=pltpu.CompilerParams(
            dimension_semantics=("parallel", "arbitrary"), vmem_limit_bytes=VMEM_LIMIT),
        name="attention",
    )(qi_tab, kj_tab, qt, k, vt)


def _sigmoid(x):
    return 0.5 * jnp.tanh(0.5 * x) + 0.5


def _rnn_kernel(xr_ref, gt_ref, perm_ref, permt_ref, cw_ref, cb_ref, wai_ref, bai_ref, lam_ref,
                o_ref, xe_sc, gt_sc, a_sc, b_sc, h_sc, *, tt, nb, rows):
    halo = (CONV_W - 1) * nb
    grp = MXU_WIDTH // nb
    blk = grp * nb

    @pl.when(pl.program_id(0) == 0)
    def _():
        h_sc[...] = jnp.zeros(h_sc.shape, F32)
        xe_sc[pl.ds(0, halo), :] = jnp.zeros((halo, xe_sc.shape[1]), F32)

    def gather_rows(ref, g):
        return jnp.concatenate([ref[bi, g * grp:(g + 1) * grp, :] for bi in range(nb)], axis=0)

    perm = perm_ref[...]
    for g in range(tt // grp):
        xe_sc[pl.ds(halo + g * blk, blk), :] = jnp.dot(perm, gather_rows(xr_ref, g),
                                                      preferred_element_type=F32)
        gt_sc[pl.ds(g * blk, blk), :] = jnp.dot(perm, gather_rows(gt_ref, g),
                                               preferred_element_type=F32)
    lam = lam_ref[...]
    log_a_unit = -LRU_C * (jnp.maximum(-lam, 0.0) + jnp.log1p(jnp.exp(-jnp.abs(lam))))
    cw = cw_ref[...]
    for r0 in range(0, tt * nb, rows):
        xc = cb_ref[...] + xe_sc[pl.ds(r0, rows), :] * cw[0:1]
        for j in range(1, CONV_W):
            xc = xc + xe_sc[pl.ds(r0 + j * nb, rows), :] * cw[j:j + 1]
        gates = jnp.dot(xc.astype(BF16), wai_ref[...], preferred_element_type=F32) + bai_ref[...]
        r = _sigmoid(gates[:, :D_RNN])
        ig = _sigmoid(gates[:, D_RNN:])
        a = jnp.exp(r * log_a_unit)
        a_sc[pl.ds(r0, rows), :] = a
        b_sc[pl.ds(r0, rows), :] = jnp.sqrt(1.0 - a * a) * ig * xc
    xe_sc[pl.ds(0, halo), :] = xe_sc[pl.ds(tt * nb, halo), :]

    def scan_step(t, h):
        sl = pl.ds(pl.multiple_of(t * nb, nb), nb)
        h = a_sc[sl, :] * h + b_sc[sl, :]
        b_sc[sl, :] = h
        return h

    h_sc[...] = lax.fori_loop(0, tt, scan_step, h_sc[...], unroll=8)

    permt = permt_ref[...]
    for g in range(tt // grp):
        sl = pl.ds(g * blk, blk)
        y = (b_sc[sl, :] * jax.nn.gelu(gt_sc[sl, :], approximate=True)).astype(BF16)
        y = jnp.dot(permt, y, preferred_element_type=F32).astype(BF16)
        for bi in range(nb):
            o_ref[bi, g * grp:(g + 1) * grp, :] = y[bi * grp:(bi + 1) * grp]


def _rglru(xr, gt, lw, tt):
    nb, tp, c = xr.shape
    grp = MXU_WIDTH // nb
    src = (jnp.arange(MXU_WIDTH) % nb) * grp + jnp.arange(MXU_WIDTH) // nb
    perm = (src[:, None] == jnp.arange(MXU_WIDTH)[None, :]).astype(BF16)
    blk = pl.BlockSpec((nb, tt, c), lambda ti: (0, ti, 0))
    rows = math.gcd(tt * nb, 512)
    return pl.pallas_call(
        functools.partial(_rnn_kernel, tt=tt, nb=nb, rows=rows),
        grid=(tp // tt,),
        in_specs=[blk, blk, _full((MXU_WIDTH, MXU_WIDTH)), _full((MXU_WIDTH, MXU_WIDTH)),
                  _full((CONV_W, c)), _full((1, c)), _full((c, 2 * c)),
                  _full((1, 2 * c)), _full((1, c))],
        out_specs=blk,
        out_shape=jax.ShapeDtypeStruct((nb, tp, c), BF16),
        scratch_shapes=[pltpu.VMEM(((tt + CONV_W - 1) * nb, c), F32),
                        pltpu.VMEM((tt * nb, c), F32),
                        pltpu.VMEM((tt * nb, c), F32), pltpu.VMEM((tt * nb, c), F32),
                        pltpu.VMEM((nb, c), F32)],
        compiler_params=pltpu.CompilerParams(
            dimension_semantics=("arbitrary",), vmem_limit_bytes=VMEM_LIMIT),
        name="rglru",
    )(xr, gt, perm, perm.T, lw["conv_w"], lw["conv_b"], lw["w_ai"], lw["b_ai"], lw["lam"])


def _post_kernel(att_ref, rnn_ref, h_ref, ga_ref, gr_ref, wo_ref, gf_ref, *rest, with_router):
    if with_router:
        wr_ref, ho_ref, u_ref, lg_ref = rest
    else:
        ho_ref, u_ref = rest
    an = (_rms(att_ref[0].astype(F32)) * ga_ref[...]).astype(BF16)
    rn = (_rms(rnn_ref[0].astype(F32)) * gr_ref[...]).astype(BF16)
    mix = jnp.concatenate([an, rn], axis=-1)
    hn = h_ref[0] + jnp.dot(mix, wo_ref[...], preferred_element_type=F32)
    ho_ref[0] = hn
    u = _rms(hn) * gf_ref[...]
    u_ref[0] = u.astype(BF16)
    if with_router:
        w = wr_ref[...]
        u_hi = u.astype(BF16)
        w_hi = w.astype(BF16)
        u_lo = (u - u_hi.astype(F32)).astype(BF16)
        w_lo = (w - w_hi.astype(F32)).astype(BF16)
        hh_hl = jnp.dot(u_hi, jnp.concatenate([w_hi, w_lo], axis=1), preferred_element_type=F32)
        lg_ref[0] = (hh_hl[:, :LANES] + hh_hl[:, LANES:]
                     + jnp.dot(u_lo, w_hi, preferred_element_type=F32))


def _post_mixer(att, rnn, h, lw, tm, w_router=None):
    b, tp, d = h.shape
    with_router = w_router is not None
    row = lambda c: pl.BlockSpec((1, tm, c), lambda bi, ti: (bi, ti, 0))
    in_specs = [row(D_ATTN), row(D_RNN), row(d),
                _full((1, D_ATTN)), _full((1, D_RNN)), _full((d, d)), _full((1, d))]
    out_specs = [row(d), row(d)]
    out_shape = [jax.ShapeDtypeStruct((b, tp, d), F32), jax.ShapeDtypeStruct((b, tp, d), BF16)]
    args = [att, rnn, h, lw["g_att"], lw["g_rnn"], lw["w_out"], lw["g_ffn"]]
    if with_router:
        in_specs.append(_full((d, LANES)))
        out_specs.append(row(LANES))
        out_shape.append(jax.ShapeDtypeStruct((b, tp, LANES), F32))
        args.append(w_router)
    return pl.pallas_call(
        functools.partial(_post_kernel, with_router=with_router),
        grid=(b, tp // tm),
        in_specs=in_specs, out_specs=out_specs, out_shape=out_shape,
        compiler_params=pltpu.CompilerParams(
            dimension_semantics=("parallel", "parallel"), vmem_limit_bytes=VMEM_LIMIT),
        name="post_mixer",
    )(*args)


def _swiglu_tile(u, wg_ref, wu_ref, wd_ref, act_sc):
    f = act_sc.shape[-1]
    full = f - f % MXU_WIDTH
    for c0 in range(0, full, MXU_WIDTH):
        g = jnp.dot(u, wg_ref[:, c0:c0 + MXU_WIDTH], preferred_element_type=F32)
        up = jnp.dot(u, wu_ref[:, c0:c0 + MXU_WIDTH], preferred_element_type=F32)
        act_sc[:, c0:c0 + MXU_WIDTH] = (jax.nn.silu(g) * up).astype(BF16)
    if full < f:
        w = jnp.concatenate([wg_ref[:, full:], wu_ref[:, full:]], axis=1)
        gu = jnp.dot(u, w, preferred_element_type=F32)
        act_sc[:, full:] = (jax.nn.silu(gu[:, :f - full]) * gu[:, f - full:]).astype(BF16)
    return jnp.dot(act_sc[...], wd_ref[...], preferred_element_type=F32)


def _ffn_kernel(u_ref, h_ref, wg_ref, wu_ref, wd_ref, o_ref, act_sc):
    o_ref[...] = h_ref[...] + _swiglu_tile(u_ref[...], wg_ref, wu_ref, wd_ref, act_sc)


def _dense_ffn(u, h, wg, wu, wd, tm):
    n, d = h.shape
    f = wd.shape[0]
    row = pl.BlockSpec((tm, d), lambda i: (i, 0))
    once = pl.Buffered(1)
    return pl.pallas_call(
        _ffn_kernel,
        grid=(n // tm,),
        in_specs=[row, row,
                  pl.BlockSpec((d, f), lambda i: (0, 0), pipeline_mode=once),
                  pl.BlockSpec((d, f), lambda i: (0, 0), pipeline_mode=once),
                  pl.BlockSpec((f, d), lambda i: (0, 0), pipeline_mode=once)],
        out_specs=row,
        out_shape=jax.ShapeDtypeStruct((n, d), F32),
        scratch_shapes=[pltpu.VMEM((tm, f), BF16)],
        compiler_params=pltpu.CompilerParams(
            dimension_semantics=("parallel",), vmem_limit_bytes=VMEM_LIMIT),
        name="dense_ffn",
    )(u, h, wg, wu, wd)


def _moe_kernel(te_ref, nt_ref, x_ref, wg_ref, wu_ref, wd_ref, o_ref, wg_sc, wu_sc, wd_sc, act_sc):
    i = pl.program_id(0)

    @pl.when((i == 0) | (te_ref[i] != te_ref[jnp.maximum(i - 1, 0)]))
    def _():
        wg_sc[...] = wg_ref[0, 0].astype(BF16)
        wu_sc[...] = wu_ref[0, 0].astype(BF16)
        wd_sc[...] = wd_ref[0, 0].astype(BF16)

    @pl.when(i < nt_ref[0])
    def _():
        o_ref[...] = _swiglu_tile(x_ref[...], wg_sc, wu_sc, wd_sc, act_sc).astype(BF16)

    @pl.when(i >= nt_ref[0])
    def _():
        o_ref[...] = jnp.zeros(o_ref.shape, BF16)


def _grouped_ffn(tile_expert, n_tiles, xs, wg, wu, wd, layer, tm):
    a, d = xs.shape
    f = wd.shape[2]
    by_expert = lambda r, c: pl.BlockSpec((1, 1, r, c), lambda i, te, nt: (layer, te[i], 0, 0))
    grid_spec = pltpu.PrefetchScalarGridSpec(
        num_scalar_prefetch=2,
        grid=(a // tm,),
        in_specs=[pl.BlockSpec((tm, d), lambda i, te, nt: (i, 0)),
                  by_expert(d, f), by_expert(d, f), by_expert(f, d)],
        out_specs=pl.BlockSpec((tm, d), lambda i, te, nt: (i, 0)),
        scratch_shapes=[pltpu.VMEM((d, f), BF16), pltpu.VMEM((d, f), BF16),
                        pltpu.VMEM((f, d), BF16), pltpu.VMEM((tm, f), BF16)],
    )
    return pl.pallas_call(
        _moe_kernel,
        grid_spec=grid_spec,
        out_shape=jax.ShapeDtypeStruct((a, d), BF16),
        compiler_params=pltpu.CompilerParams(
            dimension_semantics=("arbitrary",), vmem_limit_bytes=VMEM_LIMIT),
        name="moe_ffn",
    )(tile_expert, n_tiles, xs, wg, wu, wd)


def _moe_layer(u, h, logits, wg, wu, wd, layer, tm, keep=None):
    n, d = h.shape
    n2 = TOP_K * n
    experts = jnp.arange(N_EXPERTS, dtype=jnp.int32)
    top_val, top_idx = lax.top_k(logits[:, :N_EXPERTS], TOP_K)
    top_w = jax.nn.softmax(top_val, axis=-1)
    e_flat = top_idx.reshape(-1).astype(jnp.int32)
    onehot = (e_flat[:, None] == experts[None, :]).astype(jnp.int32)
    rank = jnp.sum((jnp.cumsum(onehot, axis=0) - onehot) * onehot, axis=-1)
    counts = jnp.sum(onehot, axis=0)
    padded = ((counts + tm - 1) // tm) * tm
    ends = jnp.cumsum(padded)
    starts = ends - padded
    pos = jnp.sum(onehot * starts[None, :], axis=-1) + rank
    order = jnp.sort(e_flat * n2 + jnp.arange(n2, dtype=jnp.int32)) % n2
    a_pad = (n2 // tm + N_EXPERTS) * tm
    slot = jnp.arange(a_pad, dtype=jnp.int32)
    slot_onehot = (jnp.sum((slot[:, None] >= ends[None, :]).astype(jnp.int32), axis=-1,
                           keepdims=True) == experts[None, :]).astype(jnp.int32)
    r = slot - jnp.sum(slot_onehot * starts[None, :], axis=-1)
    valid = r < jnp.sum(slot_onehot * counts[None, :], axis=-1)
    dense = jnp.sum(slot_onehot * (jnp.cumsum(counts) - counts)[None, :], axis=-1) + r
    slot_tok = jnp.where(valid, jnp.take(order, jnp.clip(dense, 0, n2 - 1)) // TOP_K, 0)
    tile_start = jnp.arange(a_pad // tm, dtype=jnp.int32) * tm
    tile_expert = jnp.minimum(jnp.sum((tile_start[:, None] >= ends[None, :]).astype(jnp.int32),
                                      axis=-1), N_EXPERTS - 1).astype(jnp.int32)
    n_tiles = (ends[-1] // tm).astype(jnp.int32).reshape(1)
    xs = jnp.take(u, slot_tok, axis=0)
    y = _grouped_ffn(tile_expert, n_tiles, xs, wg, wu, wd, layer, tm)
    pos2 = pos.reshape(n, TOP_K)
    if keep is not None:
        b, tp, lo, hi = keep
        rows = lambda v: v.reshape(b, tp, v.shape[-1])[:, lo:hi].reshape(b * (hi - lo), v.shape[-1])
        h, pos2, top_w = rows(h), rows(pos2), rows(top_w)
    return (h + top_w[:, 0:1] * jnp.take(y, pos2[:, 0], axis=0).astype(F32)
            + top_w[:, 1:2] * jnp.take(y, pos2[:, 1], axis=0).astype(F32))


def _block_diag(w):
    nb, c, _ = w.shape
    eye = jnp.eye(nb, dtype=w.dtype)
    return (eye[:, None, :, None] * w[:, :, None, :]).reshape(nb * c, nb * c)


def _lane_table(g, ta):
    return jnp.broadcast_to(g[:, None], (g.shape[0], ta))


def _layer_weights(l, p, ta):
    w_in = p["w_in"][l]
    w_uq = p["w_uq"][l].reshape(Q_LORA, N_HEADS, QK_HEAD)
    w_uq = jnp.pad(w_uq, ((0, 0), (0, 0), (0, LANES - QK_HEAD))).reshape(Q_LORA, N_HEADS * LANES)
    w_ukv = p["w_ukv"][l].reshape(KV_LORA, N_HEADS, QK_NOPE + V_HEAD)
    w_uk = w_ukv[:, :, :QK_NOPE].reshape(KV_LORA, N_HEADS * QK_NOPE)
    w_uv = w_ukv[:, :, QK_NOPE:].reshape(KV_LORA, N_HEADS * V_HEAD)
    q_gain = p["q_head_g"][l] * (QK_HEAD ** -0.5 * math.log2(math.e))
    return {
        "g_mix": p["norm_mix_g"][l].reshape(1, -1),
        "w_rnn": w_in[:, D_LATENT:].astype(BF16),
        "w_lat": w_in[:, :D_LATENT].T.astype(BF16),
        "g_q": _lane_table(p["q_norm_g"][l], ta),
        "g_kv": _lane_table(p["kv_norm_g"][l], ta),
        "w_uq": w_uq.T.astype(BF16),
        "w_uk": w_uk.T.astype(BF16),
        "w_uv": w_uv.T.astype(BF16),
        "g_qh": _lane_table(q_gain, ta),
        "g_kn": _lane_table(p["k_head_g"][l][:QK_NOPE], ta),
        "g_kr": _lane_table(p["k_head_g"][l][QK_NOPE:], ta),
        "conv_w": p["conv_w"][l],
        "conv_b": p["conv_b"][l].reshape(1, -1),
        "w_ai": jnp.concatenate([_block_diag(p["w_a"][l]), _block_diag(p["w_i"][l])],
                                axis=1).astype(BF16),
        "b_ai": jnp.concatenate([p["b_a"][l], p["b_i"][l]]).reshape(1, -1),
        "lam": p["lru_lambda"][l].reshape(1, -1),
        "g_att": p["attn_out_g"][l].reshape(1, -1),
        "g_rnn": p["rnn_out_g"][l].reshape(1, -1),
        "w_out": p["w_out"][l].astype(BF16),
        "g_ffn": p["norm_ffn_g"][l].reshape(1, -1),
    }


def _rope_tables(pos_full):
    inv_freq = ROPE_BASE ** (-jnp.arange(0, QK_ROPE, 2, dtype=F32) / QK_ROPE)
    ang = pos_full.astype(F32)[:, None, :] * inv_freq[None, :, None]
    return jnp.cos(ang), jnp.sin(ang)


def _largest_tile(total, cap, unit):
    best = unit
    for cand in range(unit, cap + 1, unit):
        if total % cand == 0:
            best = cand
    return best


def _pick_tiles(t, b):
    tq = 384 if t > 1024 else LANES
    tp = -(-t // tq) * tq
    tm_post = _largest_tile(tp, 768, 4 * BF16_ROWS)
    tm_ffn = _largest_tile(b * tp, 1024, 256)
    return tp, tq, tm_post, tm_ffn


def kernel(x, positions, meta_tokens, norm_mix_g, w_in, q_norm_g, w_uq, kv_norm_g, w_ukv, q_head_g, k_head_g, conv_w, conv_b, w_a, b_a, w_i, b_i, lru_lambda, attn_out_g, rnn_out_g, w_out, norm_ffn_g, ffn_w_gate, ffn_w_up, ffn_w_down, router_w, moe_w_gate, moe_w_up, moe_w_down):
    p = dict(norm_mix_g=norm_mix_g, w_in=w_in, q_norm_g=q_norm_g, w_uq=w_uq, kv_norm_g=kv_norm_g,
             w_ukv=w_ukv, q_head_g=q_head_g, k_head_g=k_head_g, conv_w=conv_w, conv_b=conv_b,
             w_a=w_a, b_a=b_a, w_i=w_i, b_i=b_i, lru_lambda=lru_lambda, attn_out_g=attn_out_g,
             rnn_out_g=rnn_out_g, w_out=w_out, norm_ffn_g=norm_ffn_g)
    b, seq, d = x.shape
    depth = w_in.shape[0]
    t = N_META + seq
    tp, tq, tm_post, tm_ffn = _pick_tiles(t, b)
    n = b * tp

    meta = jnp.broadcast_to(meta_tokens[None].astype(x.dtype), (b, N_META, d))
    h = jnp.concatenate([meta, x, jnp.zeros((b, tp - t, d), x.dtype)], axis=1)
    meta_pos = jnp.broadcast_to(jnp.arange(N_META, dtype=jnp.int32)[None, :], (b, N_META))
    pos_full = jnp.concatenate([meta_pos, positions.astype(jnp.int32) + N_META,
                                jnp.zeros((b, tp - t), jnp.int32)], axis=1)
    cos_t, sin_t = _rope_tables(pos_full)

    for l in range(depth):
        lw = _layer_weights(l, p, tq)
        qt, k, vt, xr, gt = _pre_mixer(h, cos_t, sin_t, lw, tq)
        att = _attention(qt, k, vt, tq)
        rnn = _rglru(xr, gt, lw, tq)
        j = l // 2
        if l % 2 == 0:
            h, u = _post_mixer(att, rnn, h, lw, tm_post)
            h = _dense_ffn(u.reshape(n, d), h.reshape(n, d), ffn_w_gate[j].astype(BF16),
                           ffn_w_up[j].astype(BF16), ffn_w_down[j].astype(BF16), tm_ffn)
        else:
            w_r = jnp.pad(router_w[j], ((0, 0), (0, LANES - N_EXPERTS)))
            h, u, logits = _post_mixer(att, rnn, h, lw, tm_post, w_router=w_r)
            last = l == depth - 1
            h = _moe_layer(u.reshape(n, d), h.reshape(n, d), logits.reshape(n, LANES),
                           moe_w_gate, moe_w_up, moe_w_down, j, min(tm_ffn, 512),
                           keep=(b, tp, N_META, t) if last else None)
            if last:
                return h.reshape(b, seq, d)
        h = h.reshape(b, tp, d)
    return h[:, N_META:t]
```

```python
import functools
import math

import jax
import jax.numpy as jnp
from jax import lax
from jax.experimental import pallas as pl
from jax.experimental.pallas import tpu as pltpu

F32 = jnp.float32
BF16 = jnp.bfloat16

D_MODEL = 1024
N_META = 16
D_ATTN = D_MODEL // 2
D_RNN = D_MODEL - D_ATTN
N_HEADS = 8
QK_NOPE = 64
QK_ROPE = 32
QK_HEAD = QK_NOPE + QK_ROPE
V_HEAD = D_ATTN // N_HEADS
Q_LORA = 384
KV_LORA = 256
ROPE_BASE = 10000.0
N_RNN_BLOCKS = 8
RNN_BLOCK = D_RNN // N_RNN_BLOCKS
CONV_W = 4
LRU_C = 8.0
N_EXPERTS = 8
TOP_K = 2
EPS = 1e-6

LANES = 128
SUBLANES = 8
MXU_WIDTH = 256
BF16_ROWS = 16
HALF_ROPE = QK_ROPE // 2
V_ROWS = V_HEAD + BF16_ROWS
D_LATENT = Q_LORA + KV_LORA + QK_ROPE
SEQS_PER_STEP = 4
VMEM_LIMIT = 56 * 1024 * 1024
NEG_INF = -1e30


def _rms(x):
    return x * lax.rsqrt(jnp.mean(x * x, axis=-1, keepdims=True) + EPS)


def _rms_rows(x):
    return x * lax.rsqrt(jnp.mean(x * x, axis=0, keepdims=True) + EPS)


def _full(shape):
    return pl.BlockSpec(shape, lambda *_: (0,) * len(shape))


def _pre_kernel(h_ref, cos_ref, sin_ref, gmix_ref, wrnn_ref, wlat_ref, gq_ref, gkv_ref, wuq_ref,
                wuk_ref, wuv_ref, gqh_ref, gkn_ref, gkr_ref, q_ref, k_ref, v_ref, xr_ref, gt_ref):
    for si in range(h_ref.shape[0]):
        u = (_rms(h_ref[si]) * gmix_ref[...]).astype(BF16)
        rnn_in = jnp.dot(u, wrnn_ref[...], preferred_element_type=F32)
        xr_ref[si] = rnn_in[:, :D_RNN].astype(BF16)
        gt_ref[si] = rnn_in[:, D_RNN:].astype(BF16)

        lat = lax.dot_general(wlat_ref[...], u, (((1,), (1,)), ((), ())),
                              preferred_element_type=F32)
        cqn = (_rms_rows(lat[:Q_LORA]) * gq_ref[...]).astype(BF16)
        ckvn = (_rms_rows(lat[Q_LORA:Q_LORA + KV_LORA]) * gkv_ref[...]).astype(BF16)
        kr = lat[Q_LORA + KV_LORA:]
        qt = jnp.dot(wuq_ref[...], cqn, preferred_element_type=F32)
        kt = jnp.dot(wuk_ref[...], ckvn, preferred_element_type=F32)
        vt = jnp.dot(wuv_ref[...], ckvn, preferred_element_type=F32)
        ta = vt.shape[1]
        ones = jnp.ones((V_ROWS - V_HEAD, ta), BF16)
        for hh in range(N_HEADS):
            v_ref[si, hh, :V_HEAD, :] = vt[hh * V_HEAD:(hh + 1) * V_HEAD].astype(BF16)
            v_ref[si, hh, V_HEAD:, :] = ones

        cos, sin = cos_ref[si], sin_ref[si]
        inv_dh = 1.0 / QK_HEAD
        r1, r2 = QK_NOPE, QK_NOPE + HALF_ROPE
        gqh = gqh_ref[...]
        zeros_q = jnp.zeros((LANES - QK_HEAD, ta), BF16)
        for hh in range(N_HEADS):
            x = qt[hh * LANES:(hh + 1) * LANES]
            inv = lax.rsqrt(jnp.sum(x * x, axis=0, keepdims=True) * inv_dh + EPS)
            xn = x[:QK_HEAD] * gqh * inv
            x1, x2 = xn[r1:r2], xn[r2:]
            q_ref[si, hh, :r1, :] = xn[:r1].astype(BF16)
            q_ref[si, hh, r1:r2, :] = (x1 * cos - x2 * sin).astype(BF16)
            q_ref[si, hh, r2:QK_HEAD, :] = (x2 * cos + x1 * sin).astype(BF16)
            q_ref[si, hh, QK_HEAD:, :] = zeros_q

        kr_ss = jnp.sum(kr * kr, axis=0, keepdims=True)
        krg = kr * gkr_ref[...]
        k1, k2 = krg[:HALF_ROPE], krg[HALF_ROPE:]
        kr_rot = jnp.concatenate([k1 * cos - k2 * sin, k2 * cos + k1 * sin,
                                  jnp.zeros((LANES - QK_HEAD, ta), F32)], axis=0)
        gkn = gkn_ref[...]
        for hh in range(N_HEADS):
            x = kt[hh * QK_NOPE:(hh + 1) * QK_NOPE]
            inv = lax.rsqrt((jnp.sum(x * x, axis=0, keepdims=True) + kr_ss) * inv_dh + EPS)
            kh = jnp.concatenate([x * gkn, kr_rot], axis=0) * inv
            k_ref[si, hh] = kh.T.astype(BF16)


def _pre_mixer(h, cos_t, sin_t, lw, ta):
    b, tp, d = h.shape
    ns = 2 if b % 2 == 0 else 1
    tok_rows = lambda r: pl.BlockSpec((ns, r, ta), lambda bi, ti: (bi, 0, ti))
    head_t = lambda r: pl.BlockSpec((ns, N_HEADS, r, ta), lambda bi, ti: (bi, 0, 0, ti))
    rnn_rows = pl.BlockSpec((ns, ta, D_RNN), lambda bi, ti: (bi, ti, 0))
    return pl.pallas_call(
        _pre_kernel,
        grid=(b // ns, tp // ta),
        in_specs=[pl.BlockSpec((ns, ta, d), lambda bi, ti: (bi, ti, 0)),
                  tok_rows(HALF_ROPE), tok_rows(HALF_ROPE),
                  _full((1, d)), _full((d, 2 * D_RNN)), _full((D_LATENT, d)),
                  _full((Q_LORA, ta)), _full((KV_LORA, ta)),
                  _full((N_HEADS * LANES, Q_LORA)), _full((N_HEADS * QK_NOPE, KV_LORA)),
                  _full((N_HEADS * V_HEAD, KV_LORA)),
                  _full((QK_HEAD, ta)), _full((QK_NOPE, ta)), _full((QK_ROPE, ta))],
        out_specs=[head_t(LANES),
                   pl.BlockSpec((ns, N_HEADS, ta, LANES), lambda bi, ti: (bi, 0, ti, 0)),
                   head_t(V_ROWS), rnn_rows, rnn_rows],
        out_shape=[jax.ShapeDtypeStruct((b, N_HEADS, LANES, tp), BF16),
                   jax.ShapeDtypeStruct((b, N_HEADS, tp, LANES), BF16),
                   jax.ShapeDtypeStruct((b, N_HEADS, V_ROWS, tp), BF16),
                   jax.ShapeDtypeStruct((b, tp, D_RNN), BF16),
                   jax.ShapeDtypeStruct((b, tp, D_RNN), BF16)],
        compiler_params=pltpu.CompilerParams(
            dimension_semantics=("parallel", "parallel"), vmem_limit_bytes=VMEM_LIMIT),
        name="pre_mixer",
    )(h, cos_t, sin_t, lw["g_mix"], lw["w_rnn"], lw["w_lat"], lw["g_q"], lw["g_kv"], lw["w_uq"],
      lw["w_uk"], lw["w_uv"], lw["g_qh"], lw["g_kn"], lw["g_kr"])


def _attn_kernel(qi_ref, kj_ref, qt_ref, k_ref, vt_ref, o_ref, m_sc, acc_sc, *, tq):
    qi = qi_ref[pl.program_id(1)]
    kj = kj_ref[pl.program_id(1)]

    @pl.when(kj == 0)
    def _():
        m_sc[...] = jnp.full(m_sc.shape, NEG_INF, F32)
        acc_sc[...] = jnp.zeros(acc_sc.shape, F32)

    nseq = qt_ref.shape[0]

    def step(diagonal):
        if diagonal:
            keep = (lax.broadcasted_iota(jnp.int32, (tq, tq), 0)
                    <= lax.broadcasted_iota(jnp.int32, (tq, tq), 1))
        for si in range(nseq):
            sts = [jnp.dot(k_ref[si, hh], qt_ref[si, hh], preferred_element_type=F32)
                   for hh in range(N_HEADS)]
            for hh in range(N_HEADS):
                st = jnp.where(keep, sts[hh], NEG_INF) if diagonal else sts[hh]
                m_prev = m_sc[si, hh]
                m_new = jnp.maximum(m_prev, jnp.max(st, axis=0, keepdims=True))
                alpha = jnp.exp2(m_prev - m_new)
                p = jnp.exp2(st - m_new).astype(BF16)
                m_sc[si, hh] = m_new
                acc_sc[si, hh] = alpha * acc_sc[si, hh] + jnp.dot(vt_ref[si, hh], p,
                                                                  preferred_element_type=F32)

    @pl.when(kj < qi)
    def _():
        step(False)

    @pl.when(kj == qi)
    def _():
        step(True)
        for si in range(nseq):
            outs = []
            for hh in range(N_HEADS):
                acc = acc_sc[si, hh]
                outs.append(acc[:V_HEAD] / acc[V_HEAD:V_HEAD + 1])
            o_ref[si] = jnp.concatenate(outs, axis=0).T.astype(BF16)


def _attention(qt, k, vt, tq):
    b, nh, tp, _ = k.shape
    nt = tp // tq
    vrows = vt.shape[2]
    ns = SEQS_PER_STEP if b % SEQS_PER_STEP == 0 else 1
    pairs = [(qi, kj) for qi in range(nt) for kj in range(qi + 1)]
    qi_tab = jnp.asarray([p[0] for p in pairs], jnp.int32)
    kj_tab = jnp.asarray([p[1] for p in pairs], jnp.int32)
    grid_spec = pltpu.PrefetchScalarGridSpec(
        num_scalar_prefetch=2,
        grid=(b // ns, len(pairs)),
        in_specs=[pl.BlockSpec((ns, nh, LANES, tq), lambda bi, p, qt_, kt_: (bi, 0, 0, qt_[p])),
                  pl.BlockSpec((ns, nh, tq, LANES), lambda bi, p, qt_, kt_: (bi, 0, kt_[p], 0)),
                  pl.BlockSpec((ns, nh, vrows, tq), lambda bi, p, qt_, kt_: (bi, 0, 0, kt_[p]))],
        out_specs=pl.BlockSpec((ns, tq, D_ATTN), lambda bi, p, qt_, kt_: (bi, qt_[p], 0)),
        scratch_shapes=[pltpu.VMEM((ns, nh, 1, tq), F32), pltpu.VMEM((ns, nh, vrows, tq), F32)],
    )
    return pl.pallas_call(
        functools.partial(_attn_kernel, tq=tq),
        grid_spec=grid_spec,
        out_shape=jax.ShapeDtypeStruct((b, tp, D_ATTN), BF16),
        compiler_params=pltpu.CompilerParams(
            dimension_semantics=("parallel", "arbitrary"), vmem_limit_bytes=VMEM_LIMIT),
        name="attention",
    )(qi_tab, kj_tab, qt, k, vt)


def _sigmoid(x):
    return 0.5 * jnp.tanh(0.5 * x) + 0.5


def _rnn_kernel(xr_ref, gt_ref, perm_ref, permt_ref, cw_ref, cb_ref, wai_ref, bai_ref, lam_ref,
                o_ref, xe_sc, gt_sc, a_sc, b_sc, h_sc, *, tt, nb, rows):
    halo = (CONV_W - 1) * nb
    grp = MXU_WIDTH // nb
    blk = grp * nb

    @pl.when(pl.program_id(0) == 0)
    def _():
        h_sc[...] = jnp.zeros(h_sc.shape, F32)
        xe_sc[pl.ds(0, halo), :] = jnp.zeros((halo, xe_sc.shape[1]), F32)

    def gather_rows(ref, g):
        return jnp.concatenate([ref[bi, g * grp:(g + 1) * grp, :] for bi in range(nb)], axis=0)

    perm = perm_ref[...]
    for g in range(tt // grp):
        xe_sc[pl.ds(halo + g * blk, blk), :] = jnp.dot(perm, gather_rows(xr_ref, g),
                                                      preferred_element_type=F32)
        gt_sc[pl.ds(g * blk, blk), :] = jnp.dot(perm, gather_rows(gt_ref, g),
                                               preferred_element_type=F32)
    lam = lam_ref[...]
    log_a_unit = -LRU_C * (jnp.maximum(-lam, 0.0) + jnp.log1p(jnp.exp(-jnp.abs(lam))))
    cw = cw_ref[...]
    for r0 in range(0, tt * nb, rows):
        xc = cb_ref[...] + xe_sc[pl.ds(r0, rows), :] * cw[0:1]
        for j in range(1, CONV_W):
            xc = xc + xe_sc[pl.ds(r0 + j * nb, rows), :] * cw[j:j + 1]
        gates = jnp.dot(xc.astype(BF16), wai_ref[...], preferred_element_type=F32) + bai_ref[...]
        r = _sigmoid(gates[:, :D_RNN])
        ig = _sigmoid(gates[:, D_RNN:])
        a = jnp.exp(r * log_a_unit)
        a_sc[pl.ds(r0, rows), :] = a
        b_sc[pl.ds(r0, rows), :] = jnp.sqrt(1.0 - a * a) * ig * xc
    xe_sc[pl.ds(0, halo), :] = xe_sc[pl.ds(tt * nb, halo), :]

    def scan_step(t, h):
        sl = pl.ds(pl.multiple_of(t * nb, nb), nb)
        h = a_sc[sl, :] * h + b_sc[sl, :]
        b_sc[sl, :] = h
        return h

    h_sc[...] = lax.fori_loop(0, tt, scan_step, h_sc[...], unroll=8)

    permt = permt_ref[...]
    for g in range(tt // grp):
        sl = pl.ds(g * blk, blk)
        y = (b_sc[sl, :] * jax.nn.gelu(gt_sc[sl, :], approximate=True)).astype(BF16)
        y = jnp.dot(permt, y, preferred_element_type=F32).astype(BF16)
        for bi in range(nb):
            o_ref[bi, g * grp:(g + 1) * grp, :] = y[bi * grp:(bi + 1) * grp]


def _rglru(xr, gt, lw, tt):
    nb, tp, c = xr.shape
    grp = MXU_WIDTH // nb
    src = (jnp.arange(MXU_WIDTH) % nb) * grp + jnp.arange(MXU_WIDTH) // nb
    perm = (src[:, None] == jnp.arange(MXU_WIDTH)[None, :]).astype(BF16)
    blk = pl.BlockSpec((nb, tt, c), lambda ti: (0, ti, 0))
    rows = math.gcd(tt * nb, 512)
    return pl.pallas_call(
        functools.partial(_rnn_kernel, tt=tt, nb=nb, rows=rows),
        grid=(tp // tt,),
        in_specs=[blk, blk, _full((MXU_WIDTH, MXU_WIDTH)), _full((MXU_WIDTH, MXU_WIDTH)),
                  _full((CONV_W, c)), _full((1, c)), _full((c, 2 * c)),
                  _full((1, 2 * c)), _full((1, c))],
        out_specs=blk,
        out_shape=jax.ShapeDtypeStruct((nb, tp, c), BF16),
        scratch_shapes=[pltpu.VMEM(((tt + CONV_W - 1) * nb, c), F32),
                        pltpu.VMEM((tt * nb, c), F32),
                        pltpu.VMEM((tt * nb, c), F32), pltpu.VMEM((tt * nb, c), F32),
                        pltpu.VMEM((nb, c), F32)],
        compiler_params=pltpu.CompilerParams(
            dimension_semantics=("arbitrary",), vmem_limit_bytes=VMEM_LIMIT),
        name="rglru",
    )(xr, gt, perm, perm.T, lw["conv_w"], lw["conv_b"], lw["w_ai"], lw["b_ai"], lw["lam"])


def _post_kernel(att_ref, rnn_ref, h_ref, ga_ref, gr_ref, wo_ref, gf_ref, *rest, with_router):
    if with_router:
        wr_ref, ho_ref, u_ref, lg_ref = rest
    else:
        ho_ref, u_ref = rest
    an = (_rms(att_ref[0].astype(F32)) * ga_ref[...]).astype(BF16)
    rn = (_rms(rnn_ref[0].astype(F32)) * gr_ref[...]).astype(BF16)
    mix = jnp.concatenate([an, rn], axis=-1)
    hn = h_ref[0] + jnp.dot(mix, wo_ref[...], preferred_element_type=F32)
    ho_ref[0] = hn
    u = _rms(hn) * gf_ref[...]
    u_ref[0] = u.astype(BF16)
    if with_router:
        w = wr_ref[...]
        u_hi = u.astype(BF16)
        w_hi = w.astype(BF16)
        u_lo = (u - u_hi.astype(F32)).astype(BF16)
        w_lo = (w - w_hi.astype(F32)).astype(BF16)
        hh_hl = jnp.dot(u_hi, jnp.concatenate([w_hi, w_lo], axis=1), preferred_element_type=F32)
        lg_ref[0] = (hh_hl[:, :LANES] + hh_hl[:, LANES:]
                     + jnp.dot(u_lo, w_hi, preferred_element_type=F32))


def _post_mixer(att, rnn, h, lw, tm, w_router=None):
    b, tp, d = h.shape
    with_router = w_router is not None
    row = lambda c: pl.BlockSpec((1, tm, c), lambda bi, ti: (bi, ti, 0))
    in_specs = [row(D_ATTN), row(D_RNN), row(d),
                _full((1, D_ATTN)), _full((1, D_RNN)), _full((d, d)), _full((1, d))]
    out_specs = [row(d), row(d)]
    out_shape = [jax.ShapeDtypeStruct((b, tp, d), F32), jax.ShapeDtypeStruct((b, tp, d), BF16)]
    args = [att, rnn, h, lw["g_att"], lw["g_rnn"], lw["w_out"], lw["g_ffn"]]
    if with_router:
        in_specs.append(_full((d, LANES)))
        out_specs.append(row(LANES))
        out_shape.append(jax.ShapeDtypeStruct((b, tp, LANES), F32))
        args.append(w_router)
    return pl.pallas_call(
        functools.partial(_post_kernel, with_router=with_router),
        grid=(b, tp // tm),
        in_specs=in_specs, out_specs=out_specs, out_shape=out_shape,
        compiler_params=pltpu.CompilerParams(
            dimension_semantics=("parallel", "parallel"), vmem_limit_bytes=VMEM_LIMIT),
        name="post_mixer",
    )(*args)


def _swiglu_tile(u, wg_ref, wu_ref, wd_ref, act_sc):
    f = act_sc.shape[-1]
    full = f - f % MXU_WIDTH
    for c0 in range(0, full, MXU_WIDTH):
        g = jnp.dot(u, wg_ref[:, c0:c0 + MXU_WIDTH], preferred_element_type=F32)
        up = jnp.dot(u, wu_ref[:, c0:c0 + MXU_WIDTH], preferred_element_type=F32)
        act_sc[:, c0:c0 + MXU_WIDTH] = (jax.nn.silu(g) * up).astype(BF16)
    if full < f:
        w = jnp.concatenate([wg_ref[:, full:], wu_ref[:, full:]], axis=1)
        gu = jnp.dot(u, w, preferred_element_type=F32)
        act_sc[:, full:] = (jax.nn.silu(gu[:, :f - full]) * gu[:, f - full:]).astype(BF16)
    return jnp.dot(act_sc[...], wd_ref[...], preferred_element_type=F32)


def _ffn_kernel(u_ref, h_ref, wg_ref, wu_ref, wd_ref, o_ref, act_sc):
    o_ref[...] = h_ref[...] + _swiglu_tile(u_ref[...], wg_ref, wu_ref, wd_ref, act_sc)


def _dense_ffn(u, h, wg, wu, wd, tm):
    n, d = h.shape
    f = wd.shape[0]
    row = pl.BlockSpec((tm, d), lambda i: (i, 0))
    once = pl.Buffered(1)
    return pl.pallas_call(
        _ffn_kernel,
        grid=(n // tm,),
        in_specs=[row, row,
                  pl.BlockSpec((d, f), lambda i: (0, 0), pipeline_mode=once),
                  pl.BlockSpec((d, f), lambda i: (0, 0), pipeline_mode=once),
                  pl.BlockSpec((f, d), lambda i: (0, 0), pipeline_mode=once)],
        out_specs=row,
        out_shape=jax.ShapeDtypeStruct((n, d), F32),
        scratch_shapes=[pltpu.VMEM((tm, f), BF16)],
        compiler_params=pltpu.CompilerParams(
            dimension_semantics=("parallel",), vmem_limit_bytes=VMEM_LIMIT),
        name="dense_ffn",
    )(u, h, wg, wu, wd)


def _moe_kernel(te_ref, nt_ref, x_ref, wg_ref, wu_ref, wd_ref, o_ref, wg_sc, wu_sc, wd_sc, act_sc):
    i = pl.program_id(0)

    @pl.when((i == 0) | (te_ref[i] != te_ref[jnp.maximum(i - 1, 0)]))
    def _():
        wg_sc[...] = wg_ref[0, 0].astype(BF16)
        wu_sc[...] = wu_ref[0, 0].astype(BF16)
        wd_sc[...] = wd_ref[0, 0].astype(BF16)

    @pl.when(i < nt_ref[0])
    def _():
        o_ref[...] = _swiglu_tile(x_ref[...], wg_sc, wu_sc, wd_sc, act_sc).astype(BF16)

    @pl.when(i >= nt_ref[0])
    def _():
        o_ref[...] = jnp.zeros(o_ref.shape, BF16)


def _grouped_ffn(tile_expert, n_tiles, xs, wg, wu, wd, layer, tm):
    a, d = xs.shape
    f = wd.shape[2]
    by_expert = lambda r, c: pl.BlockSpec((1, 1, r, c), lambda i, te, nt: (layer, te[i], 0, 0))
    grid_spec = pltpu.PrefetchScalarGridSpec(
        num_scalar_prefetch=2,
        grid=(a // tm,),
        in_specs=[pl.BlockSpec((tm, d), lambda i, te, nt: (i, 0)),
                  by_expert(d, f), by_expert(d, f), by_expert(f, d)],
        out_specs=pl.BlockSpec((tm, d), lambda i, te, nt: (i, 0)),
        scratch_shapes=[pltpu.VMEM((d, f), BF16), pltpu.VMEM((d, f), BF16),
                        pltpu.VMEM((f, d), BF16), pltpu.VMEM((tm, f), BF16)],
    )
    return pl.pallas_call(
        _moe_kernel,
        grid_spec=grid_spec,
        out_shape=jax.ShapeDtypeStruct((a, d), BF16),
        compiler_params=pltpu.CompilerParams(
            dimension_semantics=("arbitrary",), vmem_limit_bytes=VMEM_LIMIT),
        name="moe_ffn",
    )(tile_expert, n_tiles, xs, wg, wu, wd)


def _moe_layer(u, h, logits, wg, wu, wd, layer, tm, keep=None):
    n, d = h.shape
    n2 = TOP_K * n
    experts = jnp.arange(N_EXPERTS, dtype=jnp.int32)
    top_val, top_idx = lax.top_k(logits[:, :N_EXPERTS], TOP_K)
    top_w = jax.nn.softmax(top_val, axis=-1)
    e_flat = top_idx.reshape(-1).astype(jnp.int32)
    onehot = (e_flat[:, None] == experts[None, :]).astype(jnp.int32)
    rank = jnp.sum((jnp.cumsum(onehot, axis=0) - onehot) * onehot, axis=-1)
    counts = jnp.sum(onehot, axis=0)
    padded = ((counts + tm - 1) // tm) * tm
    ends = jnp.cumsum(padded)
    starts = ends - padded
    pos = jnp.sum(onehot * starts[None, :], axis=-1) + rank
    order = jnp.sort(e_flat * n2 + jnp.arange(n2, dtype=jnp.int32)) % n2
    a_pad = (n2 // tm + N_EXPERTS) * tm
    slot = jnp.arange(a_pad, dtype=jnp.int32)
    slot_onehot = (jnp.sum((slot[:, None] >= ends[None, :]).astype(jnp.int32), axis=-1,
                           keepdims=True) == experts[None, :]).astype(jnp.int32)
    r = slot - jnp.sum(slot_onehot * starts[None, :], axis=-1)
    valid = r < jnp.sum(slot_onehot * counts[None, :], axis=-1)
    dense = jnp.sum(slot_onehot * (jnp.cumsum(counts) - counts)[None, :], axis=-1) + r
    slot_tok = jnp.where(valid, jnp.take(order, dense, mode="clip") // TOP_K, slot % n)
    tile_start = jnp.arange(a_pad // tm, dtype=jnp.int32) * tm
    tile_expert = jnp.minimum(jnp.sum((tile_start[:, None] >= ends[None, :]).astype(jnp.int32),
                                      axis=-1), N_EXPERTS - 1).astype(jnp.int32)
    n_tiles = (ends[-1] // tm).astype(jnp.int32).reshape(1)
    xs = jnp.take(u, slot_tok, axis=0, mode="clip")
    y = _grouped_ffn(tile_expert, n_tiles, xs, wg, wu, wd, layer, tm)
    pos2 = pos.reshape(n, TOP_K)
    if keep is not None:
        b, tp, lo, hi = keep
        rows = lambda v: v.reshape(b, tp, v.shape[-1])[:, lo:hi].reshape(b * (hi - lo), v.shape[-1])
        h, pos2, top_w = rows(h), rows(pos2), rows(top_w)
    return (h + top_w[:, 0:1] * jnp.take(y, pos2[:, 0], axis=0, mode="clip").astype(F32)
            + top_w[:, 1:2] * jnp.take(y, pos2[:, 1], axis=0, mode="clip").astype(F32))


def _block_diag(w):
    nb, c, _ = w.shape
    eye = jnp.eye(nb, dtype=w.dtype)
    return (eye[:, None, :, None] * w[:, :, None, :]).reshape(nb * c, nb * c)


def _lane_table(g, ta):
    return jnp.broadcast_to(g[:, None], (g.shape[0], ta))


def _layer_weights(l, p, ta):
    w_in = p["w_in"][l]
    w_uq = p["w_uq"][l].reshape(Q_LORA, N_HEADS, QK_HEAD)
    w_uq = jnp.pad(w_uq, ((0, 0), (0, 0), (0, LANES - QK_HEAD))).reshape(Q_LORA, N_HEADS * LANES)
    w_ukv = p["w_ukv"][l].reshape(KV_LORA, N_HEADS, QK_NOPE + V_HEAD)
    w_uk = w_ukv[:, :, :QK_NOPE].reshape(KV_LORA, N_HEADS * QK_NOPE)
    w_uv = w_ukv[:, :, QK_NOPE:].reshape(KV_LORA, N_HEADS * V_HEAD)
    q_gain = p["q_head_g"][l] * (QK_HEAD ** -0.5 * math.log2(math.e))
    return {
        "g_mix": p["norm_mix_g"][l].reshape(1, -1),
        "w_rnn": w_in[:, D_LATENT:].astype(BF16),
        "w_lat": w_in[:, :D_LATENT].T.astype(BF16),
        "g_q": _lane_table(p["q_norm_g"][l], ta),
        "g_kv": _lane_table(p["kv_norm_g"][l], ta),
        "w_uq": w_uq.T.astype(BF16),
        "w_uk": w_uk.T.astype(BF16),
        "w_uv": w_uv.T.astype(BF16),
        "g_qh": _lane_table(q_gain, ta),
        "g_kn": _lane_table(p["k_head_g"][l][:QK_NOPE], ta),
        "g_kr": _lane_table(p["k_head_g"][l][QK_NOPE:], ta),
        "conv_w": p["conv_w"][l],
        "conv_b": p["conv_b"][l].reshape(1, -1),
        "w_ai": jnp.concatenate([_block_diag(p["w_a"][l]), _block_diag(p["w_i"][l])],
                                axis=1).astype(BF16),
        "b_ai": jnp.concatenate([p["b_a"][l], p["b_i"][l]]).reshape(1, -1),
        "lam": p["lru_lambda"][l].reshape(1, -1),
        "g_att": p["attn_out_g"][l].reshape(1, -1),
        "g_rnn": p["rnn_out_g"][l].reshape(1, -1),
        "w_out": p["w_out"][l].astype(BF16),
        "g_ffn": p["norm_ffn_g"][l].reshape(1, -1),
    }


def _rope_tables(pos_full):
    inv_freq = ROPE_BASE ** (-jnp.arange(0, QK_ROPE, 2, dtype=F32) / QK_ROPE)
    ang = pos_full.astype(F32)[:, None, :] * inv_freq[None, :, None]
    return jnp.cos(ang), jnp.sin(ang)


def _largest_tile(total, cap, unit):
    best = unit
    for cand in range(unit, cap + 1, unit):
        if total % cand == 0:
            best = cand
    return best


def _pick_tiles(t, b):
    tq = 384 if t > 1024 else LANES
    tp = -(-t // tq) * tq
    tm_post = _largest_tile(tp, 768, 4 * BF16_ROWS)
    tm_ffn = _largest_tile(b * tp, 1024, 256)
    return tp, tq, tm_post, tm_ffn


def kernel(x, positions, meta_tokens, norm_mix_g, w_in, q_norm_g, w_uq, kv_norm_g, w_ukv, q_head_g, k_head_g, conv_w, conv_b, w_a, b_a, w_i, b_i, lru_lambda, attn_out_g, rnn_out_g, w_out, norm_ffn_g, ffn_w_gate, ffn_w_up, ffn_w_down, router_w, moe_w_gate, moe_w_up, moe_w_down):
    p = dict(norm_mix_g=norm_mix_g, w_in=w_in, q_norm_g=q_norm_g, w_uq=w_uq, kv_norm_g=kv_norm_g,
             w_ukv=w_ukv, q_head_g=q_head_g, k_head_g=k_head_g, conv_w=conv_w, conv_b=conv_b,
             w_a=w_a, b_a=b_a, w_i=w_i, b_i=b_i, lru_lambda=lru_lambda, attn_out_g=attn_out_g,
             rnn_out_g=rnn_out_g, w_out=w_out, norm_ffn_g=norm_ffn_g)
    b, seq, d = x.shape
    depth = w_in.shape[0]
    t = N_META + seq
    tp, tq, tm_post, tm_ffn = _pick_tiles(t, b)
    n = b * tp

    meta = jnp.broadcast_to(meta_tokens[None].astype(x.dtype), (b, N_META, d))
    h = jnp.concatenate([meta, x, jnp.zeros((b, tp - t, d), x.dtype)], axis=1)
    meta_pos = jnp.broadcast_to(jnp.arange(N_META, dtype=jnp.int32)[None, :], (b, N_META))
    pos_full = jnp.concatenate([meta_pos, positions.astype(jnp.int32) + N_META,
                                jnp.zeros((b, tp - t), jnp.int32)], axis=1)
    cos_t, sin_t = _rope_tables(pos_full)

    for l in range(depth):
        lw = _layer_weights(l, p, tq)
        qt, k, vt, xr, gt = _pre_mixer(h, cos_t, sin_t, lw, tq)
        att = _attention(qt, k, vt, tq)
        rnn = _rglru(xr, gt, lw, tq)
        j = l // 2
        if l % 2 == 0:
            h, u = _post_mixer(att, rnn, h, lw, tm_post)
            h = _dense_ffn(u.reshape(n, d), h.reshape(n, d), ffn_w_gate[j].astype(BF16),
                           ffn_w_up[j].astype(BF16), ffn_w_down[j].astype(BF16), tm_ffn)
        else:
            w_r = jnp.pad(router_w[j], ((0, 0), (0, LANES - N_EXPERTS)))
            h, u, logits = _post_mixer(att, rnn, h, lw, tm_post, w_router=w_r)
            last = l == depth - 1
            h = _moe_layer(u.reshape(n, d), h.reshape(n, d), logits.reshape(n, LANES),
                           moe_w_gate, moe_w_up, moe_w_down, j, min(tm_ffn, 512),
                           keep=(b, tp, N_META, t) if last else None)
            if last:
                return h.reshape(b, seq, d)
        h = h.reshape(b, tp, d)
    return h[:, N_META:t]
```

```python
import functools
import math

import jax
import jax.numpy as jnp
from jax import lax
from jax.experimental import pallas as pl
from jax.experimental.pallas import tpu as pltpu

F32 = jnp.float32
BF16 = jnp.bfloat16

D_MODEL = 1024
N_META = 16
D_ATTN = D_MODEL // 2
D_RNN = D_MODEL - D_ATTN
N_HEADS = 8
QK_NOPE = 64
QK_ROPE = 32
QK_HEAD = QK_NOPE + QK_ROPE
V_HEAD = D_ATTN // N_HEADS
Q_LORA = 384
KV_LORA = 256
ROPE_BASE = 10000.0
N_RNN_BLOCKS = 8
RNN_BLOCK = D_RNN // N_RNN_BLOCKS
CONV_W = 4
LRU_C = 8.0
N_EXPERTS = 8
TOP_K = 2
EPS = 1e-6

LANES = 128
SUBLANES = 8
MXU_WIDTH = 256
BF16_ROWS = 16
HALF_ROPE = QK_ROPE // 2
V_ROWS = V_HEAD + BF16_ROWS
D_LATENT = Q_LORA + KV_LORA + QK_ROPE
SEQS_PER_STEP = 4
VMEM_LIMIT = 56 * 1024 * 1024
NEG_INF = -1e30


def _rms(x):
    return x * lax.rsqrt(jnp.mean(x * x, axis=-1, keepdims=True) + EPS)


def _rms_rows(x):
    return x * lax.rsqrt(jnp.mean(x * x, axis=0, keepdims=True) + EPS)


def _full(shape):
    return pl.BlockSpec(shape, lambda *_: (0,) * len(shape))


def _pre_kernel(h_ref, cos_ref, sin_ref, gmix_ref, wrnn_ref, wlat_ref, gq_ref, gkv_ref, wuq_ref,
                wuk_ref, wuv_ref, gqh_ref, gkn_ref, gkr_ref, q_ref, k_ref, v_ref, xr_ref, gt_ref):
    for si in range(h_ref.shape[0]):
        u = (_rms(h_ref[si]) * gmix_ref[...]).astype(BF16)
        rnn_in = jnp.dot(u, wrnn_ref[...], preferred_element_type=F32)
        xr_ref[si] = rnn_in[:, :D_RNN].astype(BF16)
        gt_ref[si] = rnn_in[:, D_RNN:].astype(BF16)

        lat = lax.dot_general(wlat_ref[...], u, (((1,), (1,)), ((), ())),
                              preferred_element_type=F32)
        cqn = (_rms_rows(lat[:Q_LORA]) * gq_ref[...]).astype(BF16)
        ckvn = (_rms_rows(lat[Q_LORA:Q_LORA + KV_LORA]) * gkv_ref[...]).astype(BF16)
        kr = lat[Q_LORA + KV_LORA:]
        qt = jnp.dot(wuq_ref[...], cqn, preferred_element_type=F32)
        kt = jnp.dot(wuk_ref[...], ckvn, preferred_element_type=F32)
        vt = jnp.dot(wuv_ref[...], ckvn, preferred_element_type=F32)
        ta = vt.shape[1]
        ones = jnp.ones((V_ROWS - V_HEAD, ta), BF16)
        for hh in range(N_HEADS):
            v_ref[si, hh, :V_HEAD, :] = vt[hh * V_HEAD:(hh + 1) * V_HEAD].astype(BF16)
            v_ref[si, hh, V_HEAD:, :] = ones

        cos, sin = cos_ref[si], sin_ref[si]
        inv_dh = 1.0 / QK_HEAD
        r1, r2 = QK_NOPE, QK_NOPE + HALF_ROPE
        gqh = gqh_ref[...]
        zeros_q = jnp.zeros((LANES - QK_HEAD, ta), BF16)
        for hh in range(N_HEADS):
            x = qt[hh * LANES:(hh + 1) * LANES]
            inv = lax.rsqrt(jnp.sum(x * x, axis=0, keepdims=True) * inv_dh + EPS)
            xn = x[:QK_HEAD] * gqh * inv
            x1, x2 = xn[r1:r2], xn[r2:]
            q_ref[si, hh, :r1, :] = xn[:r1].astype(BF16)
            q_ref[si, hh, r1:r2, :] = (x1 * cos - x2 * sin).astype(BF16)
            q_ref[si, hh, r2:QK_HEAD, :] = (x2 * cos + x1 * sin).astype(BF16)
            q_ref[si, hh, QK_HEAD:, :] = zeros_q

        kr_ss = jnp.sum(kr * kr, axis=0, keepdims=True)
        krg = kr * gkr_ref[...]
        k1, k2 = krg[:HALF_ROPE], krg[HALF_ROPE:]
        kr_rot = jnp.concatenate([k1 * cos - k2 * sin, k2 * cos + k1 * sin,
                                  jnp.zeros((LANES - QK_HEAD, ta), F32)], axis=0)
        gkn = gkn_ref[...]
        for hh in range(N_HEADS):
            x = kt[hh * QK_NOPE:(hh + 1) * QK_NOPE]
            inv = lax.rsqrt((jnp.sum(x * x, axis=0, keepdims=True) + kr_ss) * inv_dh + EPS)
            kh = jnp.concatenate([x * gkn, kr_rot], axis=0) * inv
            k_ref[si, hh] = kh.T.astype(BF16)


def _pre_mixer(h, cos_t, sin_t, lw, ta):
    b, tp, d = h.shape
    ns = 2 if b % 2 == 0 else 1
    tok_rows = lambda r: pl.BlockSpec((ns, r, ta), lambda bi, ti: (bi, 0, ti))
    head_t = lambda r: pl.BlockSpec((ns, N_HEADS, r, ta), lambda bi, ti: (bi, 0, 0, ti))
    rnn_rows = pl.BlockSpec((ns, ta, D_RNN), lambda bi, ti: (bi, ti, 0))
    return pl.pallas_call(
        _pre_kernel,
        grid=(b // ns, tp // ta),
        in_specs=[pl.BlockSpec((ns, ta, d), lambda bi, ti: (bi, ti, 0)),
                  tok_rows(HALF_ROPE), tok_rows(HALF_ROPE),
                  _full((1, d)), _full((d, 2 * D_RNN)), _full((D_LATENT, d)),
                  _full((Q_LORA, ta)), _full((KV_LORA, ta)),
                  _full((N_HEADS * LANES, Q_LORA)), _full((N_HEADS * QK_NOPE, KV_LORA)),
                  _full((N_HEADS * V_HEAD, KV_LORA)),
                  _full((QK_HEAD, ta)), _full((QK_NOPE, ta)), _full((QK_ROPE, ta))],
        out_specs=[head_t(LANES),
                   pl.BlockSpec((ns, N_HEADS, ta, LANES), lambda bi, ti: (bi, 0, ti, 0)),
                   head_t(V_ROWS), rnn_rows, rnn_rows],
        out_shape=[jax.ShapeDtypeStruct((b, N_HEADS, LANES, tp), BF16),
                   jax.ShapeDtypeStruct((b, N_HEADS, tp, LANES), BF16),
                   jax.ShapeDtypeStruct((b, N_HEADS, V_ROWS, tp), BF16),
                   jax.ShapeDtypeStruct((b, tp, D_RNN), BF16),
                   jax.ShapeDtypeStruct((b, tp, D_RNN), BF16)],
        compiler_params=pltpu.CompilerParams(
            dimension_semantics=("parallel", "parallel"), vmem_limit_bytes=VMEM_LIMIT),
        name="pre_mixer",
    )(h, cos_t, sin_t, lw["g_mix"], lw["w_rnn"], lw["w_lat"], lw["g_q"], lw["g_kv"], lw["w_uq"],
      lw["w_uk"], lw["w_uv"], lw["g_qh"], lw["g_kn"], lw["g_kr"])


def _attn_kernel(qi_ref, kj_ref, qt_ref, k_ref, vt_ref, o_ref, m_sc, acc_sc, *, tq):
    qi = qi_ref[pl.program_id(1)]
    kj = kj_ref[pl.program_id(1)]

    @pl.when(kj == 0)
    def _():
        m_sc[...] = jnp.full(m_sc.shape, NEG_INF, F32)
        acc_sc[...] = jnp.zeros(acc_sc.shape, F32)

    nseq = qt_ref.shape[0]

    def step(diagonal):
        if diagonal:
            keep = (lax.broadcasted_iota(jnp.int32, (tq, tq), 0)
                    <= lax.broadcasted_iota(jnp.int32, (tq, tq), 1))
        for si in range(nseq):
            sts = [jnp.dot(k_ref[si, hh], qt_ref[si, hh], preferred_element_type=F32)
                   for hh in range(N_HEADS)]
            for hh in range(N_HEADS):
                st = jnp.where(keep, sts[hh], NEG_INF) if diagonal else sts[hh]
                m_prev = m_sc[si, hh]
                m_new = jnp.maximum(m_prev, jnp.max(st, axis=0, keepdims=True))
                alpha = jnp.exp2(m_prev - m_new)
                p = jnp.exp2(st - m_new).astype(BF16)
                m_sc[si, hh] = m_new
                acc_sc[si, hh] = alpha * acc_sc[si, hh] + jnp.dot(vt_ref[si, hh], p,
                                                                  preferred_element_type=F32)

    @pl.when(kj < qi)
    def _():
        step(False)

    @pl.when(kj == qi)
    def _():
        step(True)
        for si in range(nseq):
            outs = []
            for hh in range(N_HEADS):
                acc = acc_sc[si, hh]
                outs.append(acc[:V_HEAD] / acc[V_HEAD:V_HEAD + 1])
            o_ref[si] = jnp.concatenate(outs, axis=0).T.astype(BF16)


def _attention(qt, k, vt, tq):
    b, nh, tp, _ = k.shape
    nt = tp // tq
    vrows = vt.shape[2]
    ns = SEQS_PER_STEP if b % SEQS_PER_STEP == 0 else 1
    pairs = [(qi, kj) for qi in range(nt) for kj in range(qi + 1)]
    qi_tab = jnp.asarray([p[0] for p in pairs], jnp.int32)
    kj_tab = jnp.asarray([p[1] for p in pairs], jnp.int32)
    grid_spec = pltpu.PrefetchScalarGridSpec(
        num_scalar_prefetch=2,
        grid=(b // ns, len(pairs)),
        in_specs=[pl.BlockSpec((ns, nh, LANES, tq), lambda bi, p, qt_, kt_: (bi, 0, 0, qt_[p])),
                  pl.BlockSpec((ns, nh, tq, LANES), lambda bi, p, qt_, kt_: (bi, 0, kt_[p], 0)),
                  pl.BlockSpec((ns, nh, vrows, tq), lambda bi, p, qt_, kt_: (bi, 0, 0, kt_[p]))],
        out_specs=pl.BlockSpec((ns, tq, D_ATTN), lambda bi, p, qt_, kt_: (bi, qt_[p], 0)),
        scratch_shapes=[pltpu.VMEM((ns, nh, 1, tq), F32), pltpu.VMEM((ns, nh, vrows, tq), F32)],
    )
    return pl.pallas_call(
        functools.partial(_attn_kernel, tq=tq),
        grid_spec=grid_spec,
        out_shape=jax.ShapeDtypeStruct((b, tp, D_ATTN), BF16),
        compiler_params=pltpu.CompilerParams(
            dimension_semantics=("parallel", "arbitrary"), vmem_limit_bytes=VMEM_LIMIT),
        name="attention",
    )(qi_tab, kj_tab, qt, k, vt)


def _sigmoid(x):
    return 0.5 * jnp.tanh(0.5 * x) + 0.5


def _rnn_kernel(xr_ref, gt_ref, perm_ref, permt_ref, cw_ref, cb_ref, wai_ref, bai_ref, lam_ref,
                o_ref, xe_sc, gt_sc, a_sc, b_sc, h_sc, *, tt, nb, rows):
    halo = (CONV_W - 1) * nb
    grp = MXU_WIDTH // nb
    blk = grp * nb

    @pl.when(pl.program_id(0) == 0)
    def _():
        h_sc[...] = jnp.zeros(h_sc.shape, F32)
        xe_sc[pl.ds(0, halo), :] = jnp.zeros((halo, xe_sc.shape[1]), F32)

    def gather_rows(ref, g):
        return jnp.concatenate([ref[bi, g * grp:(g + 1) * grp, :] for bi in range(nb)], axis=0)

    perm = perm_ref[...]
    for g in range(tt // grp):
        xe_sc[pl.ds(halo + g * blk, blk), :] = jnp.dot(perm, gather_rows(xr_ref, g),
                                                      preferred_element_type=F32)
        gt_sc[pl.ds(g * blk, blk), :] = jnp.dot(perm, gather_rows(gt_ref, g),
                                               preferred_element_type=F32)
    lam = lam_ref[...]
    log_a_unit = -LRU_C * (jnp.maximum(-lam, 0.0) + jnp.log1p(jnp.exp(-jnp.abs(lam))))
    cw = cw_ref[...]
    for r0 in range(0, tt * nb, rows):
        xc = cb_ref[...] + xe_sc[pl.ds(r0, rows), :] * cw[0:1]
        for j in range(1, CONV_W):
            xc = xc + xe_sc[pl.ds(r0 + j * nb, rows), :] * cw[j:j + 1]
        gates = jnp.dot(xc.astype(BF16), wai_ref[...], preferred_element_type=F32) + bai_ref[...]
        r = _sigmoid(gates[:, :D_RNN])
        ig = _sigmoid(gates[:, D_RNN:])
        a = jnp.exp(r * log_a_unit)
        a_sc[pl.ds(r0, rows), :] = a
        b_sc[pl.ds(r0, rows), :] = jnp.sqrt(1.0 - a * a) * ig * xc
    xe_sc[pl.ds(0, halo), :] = xe_sc[pl.ds(tt * nb, halo), :]

    def scan_step(t, h):
        sl = pl.ds(pl.multiple_of(t * nb, nb), nb)
        h = a_sc[sl, :] * h + b_sc[sl, :]
        b_sc[sl, :] = h
        return h

    h_sc[...] = lax.fori_loop(0, tt, scan_step, h_sc[...], unroll=8)

    permt = permt_ref[...]
    for g in range(tt // grp):
        sl = pl.ds(g * blk, blk)
        y = (b_sc[sl, :] * jax.nn.gelu(gt_sc[sl, :], approximate=True)).astype(BF16)
        y = jnp.dot(permt, y, preferred_element_type=F32).astype(BF16)
        for bi in range(nb):
            o_ref[bi, g * grp:(g + 1) * grp, :] = y[bi * grp:(bi + 1) * grp]


def _rglru(xr, gt, lw, tt):
    nb, tp, c = xr.shape
    grp = MXU_WIDTH // nb
    src = (jnp.arange(MXU_WIDTH) % nb) * grp + jnp.arange(MXU_WIDTH) // nb
    perm = (src[:, None] == jnp.arange(MXU_WIDTH)[None, :]).astype(BF16)
    blk = pl.BlockSpec((nb, tt, c), lambda ti: (0, ti, 0))
    rows = math.gcd(tt * nb, 512)
    return pl.pallas_call(
        functools.partial(_rnn_kernel, tt=tt, nb=nb, rows=rows),
        grid=(tp // tt,),
        in_specs=[blk, blk, _full((MXU_WIDTH, MXU_WIDTH)), _full((MXU_WIDTH, MXU_WIDTH)),
                  _full((CONV_W, c)), _full((1, c)), _full((c, 2 * c)),
                  _full((1, 2 * c)), _full((1, c))],
        out_specs=blk,
        out_shape=jax.ShapeDtypeStruct((nb, tp, c), BF16),
        scratch_shapes=[pltpu.VMEM(((tt + CONV_W - 1) * nb, c), F32),
                        pltpu.VMEM((tt * nb, c), F32),
                        pltpu.VMEM((tt * nb, c), F32), pltpu.VMEM((tt * nb, c), F32),
                        pltpu.VMEM((nb, c), F32)],
        compiler_params=pltpu.CompilerParams(
            dimension_semantics=("arbitrary",), vmem_limit_bytes=VMEM_LIMIT),
        name="rglru",
    )(xr, gt, perm, perm.T, lw["conv_w"], lw["conv_b"], lw["w_ai"], lw["b_ai"], lw["lam"])


def _post_kernel(att_ref, rnn_ref, h_ref, ga_ref, gr_ref, wo_ref, gf_ref, *rest, with_router):
    if with_router:
        wr_ref, ho_ref, u_ref, lg_ref = rest
    else:
        ho_ref, u_ref = rest
    an = (_rms(att_ref[0].astype(F32)) * ga_ref[...]).astype(BF16)
    rn = (_rms(rnn_ref[0].astype(F32)) * gr_ref[...]).astype(BF16)
    mix = jnp.concatenate([an, rn], axis=-1)
    hn = h_ref[0] + jnp.dot(mix, wo_ref[...], preferred_element_type=F32)
    ho_ref[0] = hn
    u = _rms(hn) * gf_ref[...]
    u_ref[0] = u.astype(BF16)
    if with_router:
        w = wr_ref[...]
        u_hi = u.astype(BF16)
        w_hi = w.astype(BF16)
        u_lo = (u - u_hi.astype(F32)).astype(BF16)
        w_lo = (w - w_hi.astype(F32)).astype(BF16)
        hh_hl = jnp.dot(u_hi, jnp.concatenate([w_hi, w_lo], axis=1), preferred_element_type=F32)
        lg_ref[0] = (hh_hl[:, :LANES] + hh_hl[:, LANES:]
                     + jnp.dot(u_lo, w_hi, preferred_element_type=F32))


def _post_mixer(att, rnn, h, lw, tm, w_router=None):
    b, tp, d = h.shape
    with_router = w_router is not None
    row = lambda c: pl.BlockSpec((1, tm, c), lambda bi, ti: (bi, ti, 0))
    in_specs = [row(D_ATTN), row(D_RNN), row(d),
                _full((1, D_ATTN)), _full((1, D_RNN)), _full((d, d)), _full((1, d))]
    out_specs = [row(d), row(d)]
    out_shape = [jax.ShapeDtypeStruct((b, tp, d), F32), jax.ShapeDtypeStruct((b, tp, d), BF16)]
    args = [att, rnn, h, lw["g_att"], lw["g_rnn"], lw["w_out"], lw["g_ffn"]]
    if with_router:
        in_specs.append(_full((d, LANES)))
        out_specs.append(row(LANES))
        out_shape.append(jax.ShapeDtypeStruct((b, tp, LANES), F32))
        args.append(w_router)
    return pl.pallas_call(
        functools.partial(_post_kernel, with_router=with_router),
        grid=(b, tp // tm),
        in_specs=in_specs, out_specs=out_specs, out_shape=out_shape,
        compiler_params=pltpu.CompilerParams(
            dimension_semantics=("parallel", "parallel"), vmem_limit_bytes=VMEM_LIMIT),
        name="post_mixer",
    )(*args)


def _swiglu_tile(u, wg_ref, wu_ref, wd_ref, act_sc):
    f = act_sc.shape[-1]
    full = f - f % MXU_WIDTH
    for c0 in range(0, full, MXU_WIDTH):
        g = jnp.dot(u, wg_ref[:, c0:c0 + MXU_WIDTH], preferred_element_type=F32)
        up = jnp.dot(u, wu_ref[:, c0:c0 + MXU_WIDTH], preferred_element_type=F32)
        act_sc[:, c0:c0 + MXU_WIDTH] = (jax.nn.silu(g) * up).astype(BF16)
    if full < f:
        w = jnp.concatenate([wg_ref[:, full:], wu_ref[:, full:]], axis=1)
        gu = jnp.dot(u, w, preferred_element_type=F32)
        act_sc[:, full:] = (jax.nn.silu(gu[:, :f - full]) * gu[:, f - full:]).astype(BF16)
    return jnp.dot(act_sc[...], wd_ref[...], preferred_element_type=F32)


def _ffn_kernel(u_ref, h_ref, wg_ref, wu_ref, wd_ref, o_ref, act_sc):
    o_ref[...] = h_ref[...] + _swiglu_tile(u_ref[...], wg_ref, wu_ref, wd_ref, act_sc)


def _dense_ffn(u, h, wg, wu, wd, tm):
    n, d = h.shape
    f = wd.shape[0]
    row = pl.BlockSpec((tm, d), lambda i: (i, 0))
    once = pl.Buffered(1)
    return pl.pallas_call(
        _ffn_kernel,
        grid=(n // tm,),
        in_specs=[row, row,
                  pl.BlockSpec((d, f), lambda i: (0, 0), pipeline_mode=once),
                  pl.BlockSpec((d, f), lambda i: (0, 0), pipeline_mode=once),
                  pl.BlockSpec((f, d), lambda i: (0, 0), pipeline_mode=once)],
        out_specs=row,
        out_shape=jax.ShapeDtypeStruct((n, d), F32),
        scratch_shapes=[pltpu.VMEM((tm, f), BF16)],
        compiler_params=pltpu.CompilerParams(
            dimension_semantics=("parallel",), vmem_limit_bytes=VMEM_LIMIT),
        name="dense_ffn",
    )(u, h, wg, wu, wd)


def _moe_kernel(te_ref, nt_ref, x_ref, wg_ref, wu_ref, wd_ref, o_ref, wg_sc, wu_sc, wd_sc, act_sc):
    i = pl.program_id(0)

    @pl.when((i == 0) | (te_ref[i] != te_ref[jnp.maximum(i - 1, 0)]))
    def _():
        wg_sc[...] = wg_ref[0, 0].astype(BF16)
        wu_sc[...] = wu_ref[0, 0].astype(BF16)
        wd_sc[...] = wd_ref[0, 0].astype(BF16)

    @pl.when(i < nt_ref[0])
    def _():
        o_ref[...] = _swiglu_tile(x_ref[...], wg_sc, wu_sc, wd_sc, act_sc).astype(BF16)

    @pl.when(i >= nt_ref[0])
    def _():
        o_ref[...] = jnp.zeros(o_ref.shape, BF16)


def _grouped_ffn(tile_expert, n_tiles, xs, wg, wu, wd, layer, tm):
    a, d = xs.shape
    f = wd.shape[2]
    by_expert = lambda r, c: pl.BlockSpec((1, 1, r, c), lambda i, te, nt: (layer, te[i], 0, 0))
    grid_spec = pltpu.PrefetchScalarGridSpec(
        num_scalar_prefetch=2,
        grid=(a // tm,),
        in_specs=[pl.BlockSpec((tm, d), lambda i, te, nt: (i, 0)),
                  by_expert(d, f), by_expert(d, f), by_expert(f, d)],
        out_specs=pl.BlockSpec((tm, d), lambda i, te, nt: (i, 0)),
        scratch_shapes=[pltpu.VMEM((d, f), BF16), pltpu.VMEM((d, f), BF16),
                        pltpu.VMEM((f, d), BF16), pltpu.VMEM((tm, f), BF16)],
    )
    return pl.pallas_call(
        _moe_kernel,
        grid_spec=grid_spec,
        out_shape=jax.ShapeDtypeStruct((a, d), BF16),
        compiler_params=pltpu.CompilerParams(
            dimension_semantics=("arbitrary",), vmem_limit_bytes=VMEM_LIMIT),
        name="moe_ffn",
    )(tile_expert, n_tiles, xs, wg, wu, wd)


def _moe_layer(u, h, logits, wg, wu, wd, layer, tm, keep=None):
    n, d = h.shape
    n2 = TOP_K * n
    experts = jnp.arange(N_EXPERTS, dtype=jnp.int32)
    top_val, top_idx = lax.top_k(logits[:, :N_EXPERTS], TOP_K)
    top_w = jax.nn.softmax(top_val, axis=-1)
    e_flat = top_idx.reshape(-1).astype(jnp.int32)
    onehot = (e_flat[:, None] == experts[None, :]).astype(jnp.int32)
    rank = jnp.sum((jnp.cumsum(onehot, axis=0) - onehot) * onehot, axis=-1)
    counts = jnp.sum(onehot, axis=0)
    padded = ((counts + tm - 1) // tm) * tm
    ends = jnp.cumsum(padded)
    starts = ends - padded
    pos = jnp.sum(onehot * starts[None, :], axis=-1) + rank
    order = jnp.sort(e_flat * n2 + jnp.arange(n2, dtype=jnp.int32)) % n2
    a_pad = (n2 // tm + N_EXPERTS) * tm
    slot = jnp.arange(a_pad, dtype=jnp.int32)
    slot_onehot = (jnp.sum((slot[:, None] >= ends[None, :]).astype(jnp.int32), axis=-1,
                           keepdims=True) == experts[None, :]).astype(jnp.int32)
    r = slot - jnp.sum(slot_onehot * starts[None, :], axis=-1)
    valid = r < jnp.sum(slot_onehot * counts[None, :], axis=-1)
    dense = jnp.sum(slot_onehot * (jnp.cumsum(counts) - counts)[None, :], axis=-1) + r
    slot_tok = jnp.where(valid, jnp.take(order, dense, mode="clip") // TOP_K, slot % n)
    tile_start = jnp.arange(a_pad // tm, dtype=jnp.int32) * tm
    tile_expert = jnp.minimum(jnp.sum((tile_start[:, None] >= ends[None, :]).astype(jnp.int32),
                                      axis=-1), N_EXPERTS - 1).astype(jnp.int32)
    n_tiles = (ends[-1] // tm).astype(jnp.int32).reshape(1)
    xs = jnp.take(u, slot_tok, axis=0, mode="clip")
    y = _grouped_ffn(tile_expert, n_tiles, xs, wg, wu, wd, layer, tm)
    pos2 = pos.reshape(n, TOP_K)
    if keep is not None:
        b, tp, lo, hi = keep
        rows = lambda v: v.reshape(b, tp, v.shape[-1])[:, lo:hi].reshape(b * (hi - lo), v.shape[-1])
        h, pos2, top_w = rows(h), rows(pos2), rows(top_w)
    return (h + top_w[:, 0:1] * jnp.take(y, pos2[:, 0], axis=0).astype(F32)
            + top_w[:, 1:2] * jnp.take(y, pos2[:, 1], axis=0).astype(F32))


def _block_diag(w):
    nb, c, _ = w.shape
    eye = jnp.eye(nb, dtype=w.dtype)
    return (eye[:, None, :, None] * w[:, :, None, :]).reshape(nb * c, nb * c)


def _lane_table(g, ta):
    return jnp.broadcast_to(g[:, None], (g.shape[0], ta))


def _layer_weights(l, p, ta):
    w_in = p["w_in"][l]
    w_uq = p["w_uq"][l].reshape(Q_LORA, N_HEADS, QK_HEAD)
    w_uq = jnp.pad(w_uq, ((0, 0), (0, 0), (0, LANES - QK_HEAD))).reshape(Q_LORA, N_HEADS * LANES)
    w_ukv = p["w_ukv"][l].reshape(KV_LORA, N_HEADS, QK_NOPE + V_HEAD)
    w_uk = w_ukv[:, :, :QK_NOPE].reshape(KV_LORA, N_HEADS * QK_NOPE)
    w_uv = w_ukv[:, :, QK_NOPE:].reshape(KV_LORA, N_HEADS * V_HEAD)
    q_gain = p["q_head_g"][l] * (QK_HEAD ** -0.5 * math.log2(math.e))
    return {
        "g_mix": p["norm_mix_g"][l].reshape(1, -1),
        "w_rnn": w_in[:, D_LATENT:].astype(BF16),
        "w_lat": w_in[:, :D_LATENT].T.astype(BF16),
        "g_q": _lane_table(p["q_norm_g"][l], ta),
        "g_kv": _lane_table(p["kv_norm_g"][l], ta),
        "w_uq": w_uq.T.astype(BF16),
        "w_uk": w_uk.T.astype(BF16),
        "w_uv": w_uv.T.astype(BF16),
        "g_qh": _lane_table(q_gain, ta),
        "g_kn": _lane_table(p["k_head_g"][l][:QK_NOPE], ta),
        "g_kr": _lane_table(p["k_head_g"][l][QK_NOPE:], ta),
        "conv_w": p["conv_w"][l],
        "conv_b": p["conv_b"][l].reshape(1, -1),
        "w_ai": jnp.concatenate([_block_diag(p["w_a"][l]), _block_diag(p["w_i"][l])],
                                axis=1).astype(BF16),
        "b_ai": jnp.concatenate([p["b_a"][l], p["b_i"][l]]).reshape(1, -1),
        "lam": p["lru_lambda"][l].reshape(1, -1),
        "g_att": p["attn_out_g"][l].reshape(1, -1),
        "g_rnn": p["rnn_out_g"][l].reshape(1, -1),
        "w_out": p["w_out"][l].astype(BF16),
        "g_ffn": p["norm_ffn_g"][l].reshape(1, -1),
    }


def _rope_tables(pos_full):
    inv_freq = ROPE_BASE ** (-jnp.arange(0, QK_ROPE, 2, dtype=F32) / QK_ROPE)
    ang = pos_full.astype(F32)[:, None, :] * inv_freq[None, :, None]
    return jnp.cos(ang), jnp.sin(ang)


def _largest_tile(total, cap, unit):
    best = unit
    for cand in range(unit, cap + 1, unit):
        if total % cand == 0:
            best = cand
    return best


def _pick_tiles(t, b):
    tq = 384 if t > 1024 else LANES
    tp = -(-t // tq) * tq
    tm_post = _largest_tile(tp, 768, 4 * BF16_ROWS)
    tm_ffn = _largest_tile(b * tp, 1024, 256)
    return tp, tq, tm_post, tm_ffn


def kernel(x, positions, meta_tokens, norm_mix_g, w_in, q_norm_g, w_uq, kv_norm_g, w_ukv, q_head_g, k_head_g, conv_w, conv_b, w_a, b_a, w_i, b_i, lru_lambda, attn_out_g, rnn_out_g, w_out, norm_ffn_g, ffn_w_gate, ffn_w_up, ffn_w_down, router_w, moe_w_gate, moe_w_up, moe_w_down):
    p = dict(norm_mix_g=norm_mix_g, w_in=w_in, q_norm_g=q_norm_g, w_uq=w_uq, kv_norm_g=kv_norm_g,
             w_ukv=w_ukv, q_head_g=q_head_g, k_head_g=k_head_g, conv_w=conv_w, conv_b=conv_b,
             w_a=w_a, b_a=b_a, w_i=w_i, b_i=b_i, lru_lambda=lru_lambda, attn_out_g=attn_out_g,
             rnn_out_g=rnn_out_g, w_out=w_out, norm_ffn_g=norm_ffn_g)
    b, seq, d = x.shape
    depth = w_in.shape[0]
    t = N_META + seq
    tp, tq, tm_post, tm_ffn = _pick_tiles(t, b)
    n = b * tp

    meta = jnp.broadcast_to(meta_tokens[None].astype(x.dtype), (b, N_META, d))
    h = jnp.concatenate([meta, x, jnp.zeros((b, tp - t, d), x.dtype)], axis=1)
    meta_pos = jnp.broadcast_to(jnp.arange(N_META, dtype=jnp.int32)[None, :], (b, N_META))
    pos_full = jnp.concatenate([meta_pos, positions.astype(jnp.int32) + N_META,
                                jnp.zeros((b, tp - t), jnp.int32)], axis=1)
    cos_t, sin_t = _rope_tables(pos_full)

    for l in range(depth):
        lw = _layer_weights(l, p, tq)
        qt, k, vt, xr, gt = _pre_mixer(h, cos_t, sin_t, lw, tq)
        att = _attention(qt, k, vt, tq)
        rnn = _rglru(xr, gt, lw, tq)
        j = l // 2
        if l % 2 == 0:
            h, u = _post_mixer(att, rnn, h, lw, tm_post)
            h = _dense_ffn(u.reshape(n, d), h.reshape(n, d), ffn_w_gate[j].astype(BF16),
                           ffn_w_up[j].astype(BF16), ffn_w_down[j].astype(BF16), tm_ffn)
        else:
            w_r = jnp.pad(router_w[j], ((0, 0), (0, LANES - N_EXPERTS)))
            h, u, logits = _post_mixer(att, rnn, h, lw, tm_post, w_router=w_r)
            last = l == depth - 1
            h = _moe_layer(u.reshape(n, d), h.reshape(n, d), logits.reshape(n, LANES),
                           moe_w_gate, moe_w_up, moe_w_down, j, min(tm_ffn, 512),
                           keep=(b, tp, N_META, t) if last else None)
            if last:
                return h.reshape(b, seq, d)
        h = h.reshape(b, tp, d)
    return h[:, N_META:t]
```

```python
import functools
import math

import jax
import jax.numpy as jnp
from jax import lax
from jax.experimental import pallas as pl
from jax.experimental.pallas import tpu as pltpu

F32 = jnp.float32
BF16 = jnp.bfloat16

D_MODEL = 1024
N_META = 16
D_ATTN = D_MODEL // 2
D_RNN = D_MODEL - D_ATTN
N_HEADS = 8
QK_NOPE = 64
QK_ROPE = 32
QK_HEAD = QK_NOPE + QK_ROPE
V_HEAD = D_ATTN // N_HEADS
Q_LORA = 384
KV_LORA = 256
ROPE_BASE = 10000.0
N_RNN_BLOCKS = 8
RNN_BLOCK = D_RNN // N_RNN_BLOCKS
CONV_W = 4
LRU_C = 8.0
N_EXPERTS = 8
TOP_K = 2
EPS = 1e-6

LANES = 128
SUBLANES = 8
MXU_WIDTH = 256
BF16_ROWS = 16
HALF_ROPE = QK_ROPE // 2
V_ROWS = V_HEAD + BF16_ROWS
D_LATENT = Q_LORA + KV_LORA + QK_ROPE
SEQS_PER_STEP = 4
VMEM_LIMIT = 56 * 1024 * 1024
NEG_INF = -1e30


def _rms(x):
    return x * lax.rsqrt(jnp.mean(x * x, axis=-1, keepdims=True) + EPS)


def _rms_rows(x):
    return x * lax.rsqrt(jnp.mean(x * x, axis=0, keepdims=True) + EPS)


def _full(shape):
    return pl.BlockSpec(shape, lambda *_: (0,) * len(shape))


def _pre_kernel(h_ref, cos_ref, sin_ref, gmix_ref, wrnn_ref, wlat_ref, gq_ref, gkv_ref, wuq_ref,
                wuk_ref, wuv_ref, gqh_ref, gkn_ref, gkr_ref, q_ref, k_ref, v_ref, xr_ref, gt_ref):
    for si in range(h_ref.shape[0]):
        u = (_rms(h_ref[si]) * gmix_ref[...]).astype(BF16)
        rnn_in = jnp.dot(u, wrnn_ref[...], preferred_element_type=F32)
        xr_ref[si] = rnn_in[:, :D_RNN].astype(BF16)
        gt_ref[si] = rnn_in[:, D_RNN:].astype(BF16)

        lat = lax.dot_general(wlat_ref[...], u, (((1,), (1,)), ((), ())),
                              preferred_element_type=F32)
        cqn = (_rms_rows(lat[:Q_LORA]) * gq_ref[...]).astype(BF16)
        ckvn = (_rms_rows(lat[Q_LORA:Q_LORA + KV_LORA]) * gkv_ref[...]).astype(BF16)
        kr = lat[Q_LORA + KV_LORA:]
        qt = jnp.dot(wuq_ref[...], cqn, preferred_element_type=F32)
        kt = jnp.dot(wuk_ref[...], ckvn, preferred_element_type=F32)
        vt = jnp.dot(wuv_ref[...], ckvn, preferred_element_type=F32)
        ta = vt.shape[1]
        ones = jnp.ones((V_ROWS - V_HEAD, ta), BF16)
        for hh in range(N_HEADS):
            v_ref[si, hh, :V_HEAD, :] = vt[hh * V_HEAD:(hh + 1) * V_HEAD].astype(BF16)
            v_ref[si, hh, V_HEAD:, :] = ones

        cos, sin = cos_ref[si], sin_ref[si]
        inv_dh = 1.0 / QK_HEAD
        r1, r2 = QK_NOPE, QK_NOPE + HALF_ROPE
        gqh = gqh_ref[...]
        zeros_q = jnp.zeros((LANES - QK_HEAD, ta), BF16)
        for hh in range(N_HEADS):
            x = qt[hh * LANES:(hh + 1) * LANES]
            inv = lax.rsqrt(jnp.sum(x * x, axis=0, keepdims=True) * inv_dh + EPS)
            xn = x[:QK_HEAD] * gqh * inv
            x1, x2 = xn[r1:r2], xn[r2:]
            q_ref[si, hh, :r1, :] = xn[:r1].astype(BF16)
            q_ref[si, hh, r1:r2, :] = (x1 * cos - x2 * sin).astype(BF16)
            q_ref[si, hh, r2:QK_HEAD, :] = (x2 * cos + x1 * sin).astype(BF16)
            q_ref[si, hh, QK_HEAD:, :] = zeros_q

        kr_ss = jnp.sum(kr * kr, axis=0, keepdims=True)
        krg = kr * gkr_ref[...]
        k1, k2 = krg[:HALF_ROPE], krg[HALF_ROPE:]
        kr_rot = jnp.concatenate([k1 * cos - k2 * sin, k2 * cos + k1 * sin,
                                  jnp.zeros((LANES - QK_HEAD, ta), F32)], axis=0)
        gkn = gkn_ref[...]
        for hh in range(N_HEADS):
            x = kt[hh * QK_NOPE:(hh + 1) * QK_NOPE]
            inv = lax.rsqrt((jnp.sum(x * x, axis=0, keepdims=True) + kr_ss) * inv_dh + EPS)
            kh = jnp.concatenate([x * gkn, kr_rot], axis=0) * inv
            k_ref[si, hh] = kh.T.astype(BF16)


def _pre_mixer(h, cos_t, sin_t, lw, ta):
    b, tp, d = h.shape
    ns = 2 if b % 2 == 0 else 1
    tok_rows = lambda r: pl.BlockSpec((ns, r, ta), lambda bi, ti: (bi, 0, ti))
    head_t = lambda r: pl.BlockSpec((ns, N_HEADS, r, ta), lambda bi, ti: (bi, 0, 0, ti))
    rnn_rows = pl.BlockSpec((ns, ta, D_RNN), lambda bi, ti: (bi, ti, 0))
    return pl.pallas_call(
        _pre_kernel,
        grid=(b // ns, tp // ta),
        in_specs=[pl.BlockSpec((ns, ta, d), lambda bi, ti: (bi, ti, 0)),
                  tok_rows(HALF_ROPE), tok_rows(HALF_ROPE),
                  _full((1, d)), _full((d, 2 * D_RNN)), _full((D_LATENT, d)),
                  _full((Q_LORA, ta)), _full((KV_LORA, ta)),
                  _full((N_HEADS * LANES, Q_LORA)), _full((N_HEADS * QK_NOPE, KV_LORA)),
                  _full((N_HEADS * V_HEAD, KV_LORA)),
                  _full((QK_HEAD, ta)), _full((QK_NOPE, ta)), _full((QK_ROPE, ta))],
        out_specs=[head_t(LANES),
                   pl.BlockSpec((ns, N_HEADS, ta, LANES), lambda bi, ti: (bi, 0, ti, 0)),
                   head_t(V_ROWS), rnn_rows, rnn_rows],
        out_shape=[jax.ShapeDtypeStruct((b, N_HEADS, LANES, tp), BF16),
                   jax.ShapeDtypeStruct((b, N_HEADS, tp, LANES), BF16),
                   jax.ShapeDtypeStruct((b, N_HEADS, V_ROWS, tp), BF16),
                   jax.ShapeDtypeStruct((b, tp, D_RNN), BF16),
                   jax.ShapeDtypeStruct((b, tp, D_RNN), BF16)],
        compiler_params=pltpu.CompilerParams(
            dimension_semantics=("parallel", "parallel"), vmem_limit_bytes=VMEM_LIMIT),
        name="pre_mixer",
    )(h, cos_t, sin_t, lw["g_mix"], lw["w_rnn"], lw["w_lat"], lw["g_q"], lw["g_kv"], lw["w_uq"],
      lw["w_uk"], lw["w_uv"], lw["g_qh"], lw["g_kn"], lw["g_kr"])


def _attn_kernel(qi_ref, kj_ref, qt_ref, k_ref, vt_ref, o_ref, m_sc, acc_sc, *, tq):
    qi = qi_ref[pl.program_id(1)]
    kj = kj_ref[pl.program_id(1)]

    @pl.when(kj == 0)
    def _():
        m_sc[...] = jnp.full(m_sc.shape, NEG_INF, F32)
        acc_sc[...] = jnp.zeros(acc_sc.shape, F32)

    nseq = qt_ref.shape[0]

    def step(diagonal):
        if diagonal:
            keep = (lax.broadcasted_iota(jnp.int32, (tq, tq), 0)
                    <= lax.broadcasted_iota(jnp.int32, (tq, tq), 1))
        for si in range(nseq):
            sts = [jnp.dot(k_ref[si, hh], qt_ref[si, hh], preferred_element_type=F32)
                   for hh in range(N_HEADS)]
            for hh in range(N_HEADS):
                st = jnp.where(keep, sts[hh], NEG_INF) if diagonal else sts[hh]
                m_prev = m_sc[si, hh]
                m_new = jnp.maximum(m_prev, jnp.max(st, axis=0, keepdims=True))
                alpha = jnp.exp2(m_prev - m_new)
                p = jnp.exp2(st - m_new).astype(BF16)
                m_sc[si, hh] = m_new
                acc_sc[si, hh] = alpha * acc_sc[si, hh] + jnp.dot(vt_ref[si, hh], p,
                                                                  preferred_element_type=F32)

    @pl.when(kj < qi)
    def _():
        step(False)

    @pl.when(kj == qi)
    def _():
        step(True)
        for si in range(nseq):
            outs = []
            for hh in range(N_HEADS):
                acc = acc_sc[si, hh]
                outs.append(acc[:V_HEAD] / acc[V_HEAD:V_HEAD + 1])
            o_ref[si] = jnp.concatenate(outs, axis=0).T.astype(BF16)


def _attention(qt, k, vt, tq):
    b, nh, tp, _ = k.shape
    nt = tp // tq
    vrows = vt.shape[2]
    ns = SEQS_PER_STEP if b % SEQS_PER_STEP == 0 else 1
    pairs = [(qi, kj) for qi in range(nt) for kj in range(qi + 1)]
    qi_tab = jnp.asarray([p[0] for p in pairs], jnp.int32)
    kj_tab = jnp.asarray([p[1] for p in pairs], jnp.int32)
    grid_spec = pltpu.PrefetchScalarGridSpec(
        num_scalar_prefetch=2,
        grid=(b // ns, len(pairs)),
        in_specs=[pl.BlockSpec((ns, nh, LANES, tq), lambda bi, p, qt_, kt_: (bi, 0, 0, qt_[p])),
                  pl.BlockSpec((ns, nh, tq, LANES), lambda bi, p, qt_, kt_: (bi, 0, kt_[p], 0)),
                  pl.BlockSpec((ns, nh, vrows, tq), lambda bi, p, qt_, kt_: (bi, 0, 0, kt_[p]))],
        out_specs=pl.BlockSpec((ns, tq, D_ATTN), lambda bi, p, qt_, kt_: (bi, qt_[p], 0)),
        scratch_shapes=[pltpu.VMEM((ns, nh, 1, tq), F32), pltpu.VMEM((ns, nh, vrows, tq), F32)],
    )
    return pl.pallas_call(
        functools.partial(_attn_kernel, tq=tq),
        grid_spec=grid_spec,
        out_shape=jax.ShapeDtypeStruct((b, tp, D_ATTN), BF16),
        compiler_params=pltpu.CompilerParams(
            dimension_semantics=("parallel", "arbitrary"), vmem_limit_bytes=VMEM_LIMIT),
        name="attention",
    )(qi_tab, kj_tab, qt, k, vt)


def _sigmoid(x):
    return 0.5 * jnp.tanh(0.5 * x) + 0.5


def _rnn_kernel(xr_ref, gt_ref, perm_ref, permt_ref, cw_ref, cb_ref, wai_ref, bai_ref, lam_ref,
                o_ref, xe_sc, gt_sc, a_sc, b_sc, h_sc, *, tt, nb, rows):
    halo = (CONV_W - 1) * nb
    grp = MXU_WIDTH // nb
    blk = grp * nb

    @pl.when(pl.program_id(0) == 0)
    def _():
        h_sc[...] = jnp.zeros(h_sc.shape, F32)
        xe_sc[pl.ds(0, halo), :] = jnp.zeros((halo, xe_sc.shape[1]), F32)

    def gather_rows(ref, g):
        return jnp.concatenate([ref[bi, g * grp:(g + 1) * grp, :] for bi in range(nb)], axis=0)

    perm = perm_ref[...]
    for g in range(tt // grp):
        xe_sc[pl.ds(halo + g * blk, blk), :] = jnp.dot(perm, gather_rows(xr_ref, g),
                                                      preferred_element_type=F32)
        gt_sc[pl.ds(g * blk, blk), :] = jnp.dot(perm, gather_rows(gt_ref, g),
                                               preferred_element_type=F32)
    lam = lam_ref[...]
    log_a_unit = -LRU_C * (jnp.maximum(-lam, 0.0) + jnp.log1p(jnp.exp(-jnp.abs(lam))))
    cw = cw_ref[...]
    for r0 in range(0, tt * nb, rows):
        xc = cb_ref[...] + xe_sc[pl.ds(r0, rows), :] * cw[0:1]
        for j in range(1, CONV_W):
            xc = xc + xe_sc[pl.ds(r0 + j * nb, rows), :] * cw[j:j + 1]
        gates = jnp.dot(xc.astype(BF16), wai_ref[...], preferred_element_type=F32) + bai_ref[...]
        r = _sigmoid(gates[:, :D_RNN])
        ig = _sigmoid(gates[:, D_RNN:])
        a = jnp.exp(r * log_a_unit)
        a_sc[pl.ds(r0, rows), :] = a
        b_sc[pl.ds(r0, rows), :] = jnp.sqrt(1.0 - a * a) * ig * xc
    xe_sc[pl.ds(0, halo), :] = xe_sc[pl.ds(tt * nb, halo), :]

    def scan_step(t, h):
        sl = pl.ds(pl.multiple_of(t * nb, nb), nb)
        h = a_sc[sl, :] * h + b_sc[sl, :]
        b_sc[sl, :] = h
        return h

    h_sc[...] = lax.fori_loop(0, tt, scan_step, h_sc[...], unroll=8)

    permt = permt_ref[...]
    for g in range(tt // grp):
        sl = pl.ds(g * blk, blk)
        y = (b_sc[sl, :] * jax.nn.gelu(gt_sc[sl, :], approximate=True)).astype(BF16)
        y = jnp.dot(permt, y, preferred_element_type=F32).astype(BF16)
        for bi in range(nb):
            o_ref[bi, g * grp:(g + 1) * grp, :] = y[bi * grp:(bi + 1) * grp]


def _rglru(xr, gt, lw, tt):
    nb, tp, c = xr.shape
    grp = MXU_WIDTH // nb
    src = (jnp.arange(MXU_WIDTH) % nb) * grp + jnp.arange(MXU_WIDTH) // nb
    perm = (src[:, None] == jnp.arange(MXU_WIDTH)[None, :]).astype(BF16)
    blk = pl.BlockSpec((nb, tt, c), lambda ti: (0, ti, 0))
    rows = math.gcd(tt * nb, 512)
    return pl.pallas_call(
        functools.partial(_rnn_kernel, tt=tt, nb=nb, rows=rows),
        grid=(tp // tt,),
        in_specs=[blk, blk, _full((MXU_WIDTH, MXU_WIDTH)), _full((MXU_WIDTH, MXU_WIDTH)),
                  _full((CONV_W, c)), _full((1, c)), _full((c, 2 * c)),
                  _full((1, 2 * c)), _full((1, c))],
        out_specs=blk,
        out_shape=jax.ShapeDtypeStruct((nb, tp, c), BF16),
        scratch_shapes=[pltpu.VMEM(((tt + CONV_W - 1) * nb, c), F32),
                        pltpu.VMEM((tt * nb, c), F32),
                        pltpu.VMEM((tt * nb, c), F32), pltpu.VMEM((tt * nb, c), F32),
                        pltpu.VMEM((nb, c), F32)],
        compiler_params=pltpu.CompilerParams(
            dimension_semantics=("arbitrary",), vmem_limit_bytes=VMEM_LIMIT),
        name="rglru",
    )(xr, gt, perm, perm.T, lw["conv_w"], lw["conv_b"], lw["w_ai"], lw["b_ai"], lw["lam"])


def _mix_residual(att, rnn, h, ga_ref, gr_ref, wo_ref):
    an = (_rms(att.astype(F32)) * ga_ref[...]).astype(BF16)
    rn = (_rms(rnn.astype(F32)) * gr_ref[...]).astype(BF16)
    mix = jnp.concatenate([an, rn], axis=-1)
    return h + jnp.dot(mix, wo_ref[...], preferred_element_type=F32)


def _post_router_kernel(att_ref, rnn_ref, h_ref, ga_ref, gr_ref, wo_ref, gf_ref, wr_ref,
                        ho_ref, u_ref, lg_ref):
    hn = _mix_residual(att_ref[...], rnn_ref[...], h_ref[...], ga_ref, gr_ref, wo_ref)
    ho_ref[...] = hn
    u = _rms(hn) * gf_ref[...]
    u_ref[...] = u.astype(BF16)
    w = wr_ref[...]
    u_hi = u.astype(BF16)
    w_hi = w.astype(BF16)
    u_lo = (u - u_hi.astype(F32)).astype(BF16)
    w_lo = (w - w_hi.astype(F32)).astype(BF16)
    hh_hl = jnp.dot(u_hi, jnp.concatenate([w_hi, w_lo], axis=1), preferred_element_type=F32)
    lg_ref[...] = (hh_hl[:, :LANES] + hh_hl[:, LANES:]
                   + jnp.dot(u_lo, w_hi, preferred_element_type=F32))


def _post_router(att, rnn, h, lw, w_router, tm):
    n, d = h.shape
    row = lambda c: pl.BlockSpec((tm, c), lambda i: (i, 0))
    return pl.pallas_call(
        _post_router_kernel,
        grid=(n // tm,),
        in_specs=[row(D_ATTN), row(D_RNN), row(d), _full((1, D_ATTN)), _full((1, D_RNN)),
                  _full((d, d)), _full((1, d)), _full((d, LANES))],
        out_specs=[row(d), row(d), row(LANES)],
        out_shape=[jax.ShapeDtypeStruct((n, d), F32), jax.ShapeDtypeStruct((n, d), BF16),
                   jax.ShapeDtypeStruct((n, LANES), F32)],
        compiler_params=pltpu.CompilerParams(
            dimension_semantics=("parallel",), vmem_limit_bytes=VMEM_LIMIT),
        name="post_router",
    )(att, rnn, h, lw["g_att"], lw["g_rnn"], lw["w_out"], lw["g_ffn"], w_router)


def _swiglu_tile(u, wg_ref, wu_ref, wd_ref, act_sc):
    f = act_sc.shape[-1]
    full = f - f % MXU_WIDTH
    for c0 in range(0, full, MXU_WIDTH):
        g = jnp.dot(u, wg_ref[:, c0:c0 + MXU_WIDTH], preferred_element_type=F32)
        up = jnp.dot(u, wu_ref[:, c0:c0 + MXU_WIDTH], preferred_element_type=F32)
        act_sc[:, c0:c0 + MXU_WIDTH] = (jax.nn.silu(g) * up).astype(BF16)
    if full < f:
        w = jnp.concatenate([wg_ref[:, full:], wu_ref[:, full:]], axis=1)
        gu = jnp.dot(u, w, preferred_element_type=F32)
        act_sc[:, full:] = (jax.nn.silu(gu[:, :f - full]) * gu[:, f - full:]).astype(BF16)
    return jnp.dot(act_sc[...], wd_ref[...], preferred_element_type=F32)


def _post_ffn_kernel(att_ref, rnn_ref, h_ref, ga_ref, gr_ref, wo_ref, gf_ref, wg_ref, wu_ref,
                     wd_ref, o_ref, act_sc):
    hn = _mix_residual(att_ref[...], rnn_ref[...], h_ref[...], ga_ref, gr_ref, wo_ref)
    u = (_rms(hn) * gf_ref[...]).astype(BF16)
    o_ref[...] = hn + _swiglu_tile(u, wg_ref, wu_ref, wd_ref, act_sc)


def _post_dense_ffn(att, rnn, h, lw, wg, wu, wd, tm):
    n, d = h.shape
    f = wd.shape[0]
    row = lambda c: pl.BlockSpec((tm, c), lambda i: (i, 0))
    once = lambda r, c: pl.BlockSpec((r, c), lambda i: (0, 0), pipeline_mode=pl.Buffered(1))
    return pl.pallas_call(
        _post_ffn_kernel,
        grid=(n // tm,),
        in_specs=[row(D_ATTN), row(D_RNN), row(d), _full((1, D_ATTN)), _full((1, D_RNN)),
                  once(d, d), _full((1, d)), once(d, f), once(d, f), once(f, d)],
        out_specs=row(d),
        out_shape=jax.ShapeDtypeStruct((n, d), F32),
        scratch_shapes=[pltpu.VMEM((tm, f), BF16)],
        compiler_params=pltpu.CompilerParams(
            dimension_semantics=("parallel",), vmem_limit_bytes=VMEM_LIMIT),
        name="post_dense_ffn",
    )(att, rnn, h, lw["g_att"], lw["g_rnn"], lw["w_out"], lw["g_ffn"], wg, wu, wd)


def _moe_kernel(te_ref, nt_ref, x_ref, wg_ref, wu_ref, wd_ref, o_ref, wg_sc, wu_sc, wd_sc, act_sc):
    i = pl.program_id(0)

    @pl.when((i == 0) | (te_ref[i] != te_ref[jnp.maximum(i - 1, 0)]))
    def _():
        wg_sc[...] = wg_ref[0, 0].astype(BF16)
        wu_sc[...] = wu_ref[0, 0].astype(BF16)
        wd_sc[...] = wd_ref[0, 0].astype(BF16)

    @pl.when(i < nt_ref[0])
    def _():
        o_ref[...] = _swiglu_tile(x_ref[...], wg_sc, wu_sc, wd_sc, act_sc).astype(BF16)

    @pl.when(i >= nt_ref[0])
    def _():
        o_ref[...] = jnp.zeros(o_ref.shape, BF16)


def _grouped_ffn(tile_expert, n_tiles, xs, wg, wu, wd, layer, tm):
    a, d = xs.shape
    f = wd.shape[2]
    by_expert = lambda r, c: pl.BlockSpec((1, 1, r, c), lambda i, te, nt: (layer, te[i], 0, 0))
    grid_spec = pltpu.PrefetchScalarGridSpec(
        num_scalar_prefetch=2,
        grid=(a // tm,),
        in_specs=[pl.BlockSpec((tm, d), lambda i, te, nt: (i, 0)),
                  by_expert(d, f), by_expert(d, f), by_expert(f, d)],
        out_specs=pl.BlockSpec((tm, d), lambda i, te, nt: (i, 0)),
        scratch_shapes=[pltpu.VMEM((d, f), BF16), pltpu.VMEM((d, f), BF16),
                        pltpu.VMEM((f, d), BF16), pltpu.VMEM((tm, f), BF16)],
    )
    return pl.pallas_call(
        _moe_kernel,
        grid_spec=grid_spec,
        out_shape=jax.ShapeDtypeStruct((a, d), BF16),
        compiler_params=pltpu.CompilerParams(
            dimension_semantics=("arbitrary",), vmem_limit_bytes=VMEM_LIMIT),
        name="moe_ffn",
    )(tile_expert, n_tiles, xs, wg, wu, wd)


def _moe_layer(u, h, logits, wg, wu, wd, layer, tm, keep=None):
    n, d = h.shape
    n2 = TOP_K * n
    experts = jnp.arange(N_EXPERTS, dtype=jnp.int32)
    top_val, top_idx = lax.top_k(logits[:, :N_EXPERTS], TOP_K)
    top_w = jax.nn.softmax(top_val, axis=-1)
    e_flat = top_idx.reshape(-1).astype(jnp.int32)
    onehot = (e_flat[:, None] == experts[None, :]).astype(jnp.int32)
    rank = jnp.sum((jnp.cumsum(onehot, axis=0) - onehot) * onehot, axis=-1)
    counts = jnp.sum(onehot, axis=0)
    padded = ((counts + tm - 1) // tm) * tm
    ends = jnp.cumsum(padded)
    starts = ends - padded
    pos = jnp.sum(onehot * starts[None, :], axis=-1) + rank
    order = jnp.sort(e_flat * n2 + jnp.arange(n2, dtype=jnp.int32)) % n2
    a_pad = (n2 // tm + N_EXPERTS) * tm
    slot = jnp.arange(a_pad, dtype=jnp.int32)
    slot_onehot = (jnp.sum((slot[:, None] >= ends[None, :]).astype(jnp.int32), axis=-1,
                           keepdims=True) == experts[None, :]).astype(jnp.int32)
    r = slot - jnp.sum(slot_onehot * starts[None, :], axis=-1)
    valid = r < jnp.sum(slot_onehot * counts[None, :], axis=-1)
    dense = jnp.sum(slot_onehot * (jnp.cumsum(counts) - counts)[None, :], axis=-1) + r
    slot_tok = jnp.where(valid, jnp.take(order, dense, mode="clip") // TOP_K, slot % n)
    tile_start = jnp.arange(a_pad // tm, dtype=jnp.int32) * tm
    tile_expert = jnp.minimum(jnp.sum((tile_start[:, None] >= ends[None, :]).astype(jnp.int32),
                                      axis=-1), N_EXPERTS - 1).astype(jnp.int32)
    n_tiles = (ends[-1] // tm).astype(jnp.int32).reshape(1)
    xs = jnp.take(u, slot_tok, axis=0, mode="clip")
    y = _grouped_ffn(tile_expert, n_tiles, xs, wg, wu, wd, layer, tm)
    pos2 = pos.reshape(n, TOP_K)
    lead = (n,)
    if keep is not None:
        b, tp, lo, hi = keep
        rows = lambda v: v.reshape(b, tp, v.shape[-1])[:, lo:hi]
        h, pos2, top_w = rows(h), rows(pos2), rows(top_w)
        lead = (b, hi - lo)
    picked = lambda k: jnp.take(y, pos2[..., k].reshape(-1), axis=0).reshape(lead + (d,))
    return (h + top_w[..., 0:1] * picked(0).astype(F32) + top_w[..., 1:2] * picked(1).astype(F32))


def _block_diag(w):
    nb, c, _ = w.shape
    eye = jnp.eye(nb, dtype=w.dtype)
    return (eye[:, None, :, None] * w[:, :, None, :]).reshape(nb * c, nb * c)


def _lane_table(g, ta):
    return jnp.broadcast_to(g[:, None], (g.shape[0], ta))


def _layer_weights(l, p, ta):
    w_in = p["w_in"][l]
    w_uq = p["w_uq"][l].reshape(Q_LORA, N_HEADS, QK_HEAD)
    w_uq = jnp.pad(w_uq, ((0, 0), (0, 0), (0, LANES - QK_HEAD))).reshape(Q_LORA, N_HEADS * LANES)
    w_ukv = p["w_ukv"][l].reshape(KV_LORA, N_HEADS, QK_NOPE + V_HEAD)
    w_uk = w_ukv[:, :, :QK_NOPE].reshape(KV_LORA, N_HEADS * QK_NOPE)
    w_uv = w_ukv[:, :, QK_NOPE:].reshape(KV_LORA, N_HEADS * V_HEAD)
    q_gain = p["q_head_g"][l] * (QK_HEAD ** -0.5 * math.log2(math.e))
    return {
        "g_mix": p["norm_mix_g"][l].reshape(1, -1),
        "w_rnn": w_in[:, D_LATENT:].astype(BF16),
        "w_lat": w_in[:, :D_LATENT].T.astype(BF16),
        "g_q": _lane_table(p["q_norm_g"][l], ta),
        "g_kv": _lane_table(p["kv_norm_g"][l], ta),
        "w_uq": w_uq.T.astype(BF16),
        "w_uk": w_uk.T.astype(BF16),
        "w_uv": w_uv.T.astype(BF16),
        "g_qh": _lane_table(q_gain, ta),
        "g_kn": _lane_table(p["k_head_g"][l][:QK_NOPE], ta),
        "g_kr": _lane_table(p["k_head_g"][l][QK_NOPE:], ta),
        "conv_w": p["conv_w"][l],
        "conv_b": p["conv_b"][l].reshape(1, -1),
        "w_ai": jnp.concatenate([_block_diag(p["w_a"][l]), _block_diag(p["w_i"][l])],
                                axis=1).astype(BF16),
        "b_ai": jnp.concatenate([p["b_a"][l], p["b_i"][l]]).reshape(1, -1),
        "lam": p["lru_lambda"][l].reshape(1, -1),
        "g_att": p["attn_out_g"][l].reshape(1, -1),
        "g_rnn": p["rnn_out_g"][l].reshape(1, -1),
        "w_out": p["w_out"][l].astype(BF16),
        "g_ffn": p["norm_ffn_g"][l].reshape(1, -1),
    }


def _rope_tables(pos_full):
    inv_freq = ROPE_BASE ** (-jnp.arange(0, QK_ROPE, 2, dtype=F32) / QK_ROPE)
    ang = pos_full.astype(F32)[:, None, :] * inv_freq[None, :, None]
    return jnp.cos(ang), jnp.sin(ang)


def _largest_tile(total, cap, unit):
    best = unit
    for cand in range(unit, cap + 1, unit):
        if total % cand == 0:
            best = cand
    return best


def _pick_tiles(t, b):
    tq = 384 if t > 1024 else LANES
    tp = -(-t // tq) * tq
    tm_row = _largest_tile(b * tp, 512, 256)
    return tp, tq, tm_row


def kernel(x, positions, meta_tokens, norm_mix_g, w_in, q_norm_g, w_uq, kv_norm_g, w_ukv, q_head_g, k_head_g, conv_w, conv_b, w_a, b_a, w_i, b_i, lru_lambda, attn_out_g, rnn_out_g, w_out, norm_ffn_g, ffn_w_gate, ffn_w_up, ffn_w_down, router_w, moe_w_gate, moe_w_up, moe_w_down):
    p = dict(norm_mix_g=norm_mix_g, w_in=w_in, q_norm_g=q_norm_g, w_uq=w_uq, kv_norm_g=kv_norm_g,
             w_ukv=w_ukv, q_head_g=q_head_g, k_head_g=k_head_g, conv_w=conv_w, conv_b=conv_b,
             w_a=w_a, b_a=b_a, w_i=w_i, b_i=b_i, lru_lambda=lru_lambda, attn_out_g=attn_out_g,
             rnn_out_g=rnn_out_g, w_out=w_out, norm_ffn_g=norm_ffn_g)
    b, seq, d = x.shape
    depth = w_in.shape[0]
    t = N_META + seq
    tp, tq, tm_row = _pick_tiles(t, b)
    n = b * tp

    meta = jnp.broadcast_to(meta_tokens[None].astype(x.dtype), (b, N_META, d))
    h = jnp.concatenate([meta, x, jnp.zeros((b, tp - t, d), x.dtype)], axis=1)
    meta_pos = jnp.broadcast_to(jnp.arange(N_META, dtype=jnp.int32)[None, :], (b, N_META))
    pos_full = jnp.concatenate([meta_pos, positions.astype(jnp.int32) + N_META,
                                jnp.zeros((b, tp - t), jnp.int32)], axis=1)
    cos_t, sin_t = _rope_tables(pos_full)

    for l in range(depth):
        lw = _layer_weights(l, p, tq)
        qt, k, vt, xr, gt = _pre_mixer(h, cos_t, sin_t, lw, tq)
        att = _attention(qt, k, vt, tq)
        rnn = _rglru(xr, gt, lw, tq)
        j = l // 2
        att, rnn, h = att.reshape(n, D_ATTN), rnn.reshape(n, D_RNN), h.reshape(n, d)
        if l % 2 == 0:
            h = _post_dense_ffn(att, rnn, h, lw, ffn_w_gate[j].astype(BF16),
                                ffn_w_up[j].astype(BF16), ffn_w_down[j].astype(BF16), tm_row)
        else:
            w_r = jnp.pad(router_w[j], ((0, 0), (0, LANES - N_EXPERTS)))
            h, u, logits = _post_router(att, rnn, h, lw, w_r, tm_row)
            last = l == depth - 1
            h = _moe_layer(u, h, logits, moe_w_gate, moe_w_up, moe_w_down, j, tm_row,
                           keep=(b, tp, N_META, t) if last else None)
            if last:
                return h
        h = h.reshape(b, tp, d)
    return h[:, N_META:t]
```

```python
import functools
import math

import jax
import jax.numpy as jnp
from jax import lax
from jax.experimental import pallas as pl
from jax.experimental.pallas import tpu as pltpu

F32 = jnp.float32
BF16 = jnp.bfloat16

D_MODEL = 1024
N_META = 16
D_ATTN = D_MODEL // 2
D_RNN = D_MODEL - D_ATTN
N_HEADS = 8
QK_NOPE = 64
QK_ROPE = 32
QK_HEAD = QK_NOPE + QK_ROPE
V_HEAD = D_ATTN // N_HEADS
Q_LORA = 384
KV_LORA = 256
ROPE_BASE = 10000.0
N_RNN_BLOCKS = 8
RNN_BLOCK = D_RNN // N_RNN_BLOCKS
CONV_W = 4
LRU_C = 8.0
N_EXPERTS = 8
TOP_K = 2
EPS = 1e-6

LANES = 128
SUBLANES = 8
MXU_WIDTH = 256
BF16_ROWS = 16
HALF_ROPE = QK_ROPE // 2
V_ROWS = V_HEAD + BF16_ROWS
D_LATENT = Q_LORA + KV_LORA + QK_ROPE
SEQS_PER_STEP = 4
VMEM_LIMIT = 56 * 1024 * 1024
NEG_INF = -1e30


def _rms(x):
    return x * lax.rsqrt(jnp.mean(x * x, axis=-1, keepdims=True) + EPS)


def _rms_rows(x):
    return x * lax.rsqrt(jnp.mean(x * x, axis=0, keepdims=True) + EPS)


def _full(shape):
    return pl.BlockSpec(shape, lambda *_: (0,) * len(shape))


def _pre_kernel(h_ref, cos_ref, sin_ref, gmix_ref, wrnn_ref, wlat_ref, gq_ref, gkv_ref, wuq_ref,
                wuk_ref, wuv_ref, gqh_ref, gkn_ref, gkr_ref, q_ref, k_ref, v_ref, xr_ref, gt_ref):
    for si in range(h_ref.shape[0]):
        u = (_rms(h_ref[si]) * gmix_ref[...]).astype(BF16)
        rnn_in = jnp.dot(u, wrnn_ref[...], preferred_element_type=F32)
        xr_ref[si] = rnn_in[:, :D_RNN].astype(BF16)
        gt_ref[si] = rnn_in[:, D_RNN:].astype(BF16)

        lat = lax.dot_general(wlat_ref[...], u, (((1,), (1,)), ((), ())),
                              preferred_element_type=F32)
        cqn = (_rms_rows(lat[:Q_LORA]) * gq_ref[...]).astype(BF16)
        ckvn = (_rms_rows(lat[Q_LORA:Q_LORA + KV_LORA]) * gkv_ref[...]).astype(BF16)
        kr = lat[Q_LORA + KV_LORA:]
        qt = jnp.dot(wuq_ref[...], cqn, preferred_element_type=F32)
        kt = jnp.dot(wuk_ref[...], ckvn, preferred_element_type=F32)
        vt = jnp.dot(wuv_ref[...], ckvn, preferred_element_type=F32)
        ta = vt.shape[1]
        ones = jnp.ones((V_ROWS - V_HEAD, ta), BF16)
        for hh in range(N_HEADS):
            v_ref[si, hh, :V_HEAD, :] = vt[hh * V_HEAD:(hh + 1) * V_HEAD].astype(BF16)
            v_ref[si, hh, V_HEAD:, :] = ones

        cos, sin = cos_ref[si], sin_ref[si]
        inv_dh = 1.0 / QK_HEAD
        r1, r2 = QK_NOPE, QK_NOPE + HALF_ROPE
        gqh = gqh_ref[...]
        zeros_q = jnp.zeros((LANES - QK_HEAD, ta), BF16)
        for hh in range(N_HEADS):
            x = qt[hh * LANES:(hh + 1) * LANES]
            inv = lax.rsqrt(jnp.sum(x * x, axis=0, keepdims=True) * inv_dh + EPS)
            xn = x[:QK_HEAD] * gqh * inv
            x1, x2 = xn[r1:r2], xn[r2:]
            q_ref[si, hh, :r1, :] = xn[:r1].astype(BF16)
            q_ref[si, hh, r1:r2, :] = (x1 * cos - x2 * sin).astype(BF16)
            q_ref[si, hh, r2:QK_HEAD, :] = (x2 * cos + x1 * sin).astype(BF16)
            q_ref[si, hh, QK_HEAD:, :] = zeros_q

        kr_ss = jnp.sum(kr * kr, axis=0, keepdims=True)
        krg = kr * gkr_ref[...]
        k1, k2 = krg[:HALF_ROPE], krg[HALF_ROPE:]
        kr_rot = jnp.concatenate([k1 * cos - k2 * sin, k2 * cos + k1 * sin,
                                  jnp.zeros((LANES - QK_HEAD, ta), F32)], axis=0)
        gkn = gkn_ref[...]
        for hh in range(N_HEADS):
            x = kt[hh * QK_NOPE:(hh + 1) * QK_NOPE]
            inv = lax.rsqrt((jnp.sum(x * x, axis=0, keepdims=True) + kr_ss) * inv_dh + EPS)
            kh = jnp.concatenate([x * gkn, kr_rot], axis=0) * inv
            k_ref[si, hh] = kh.T.astype(BF16)


def _pre_mixer(h, cos_t, sin_t, lw, ta):
    b, tp, d = h.shape
    ns = 2 if b % 2 == 0 else 1
    tok_rows = lambda r: pl.BlockSpec((ns, r, ta), lambda bi, ti: (bi, 0, ti))
    head_t = lambda r: pl.BlockSpec((ns, N_HEADS, r, ta), lambda bi, ti: (bi, 0, 0, ti))
    rnn_rows = pl.BlockSpec((ns, ta, D_RNN), lambda bi, ti: (bi, ti, 0))
    return pl.pallas_call(
        _pre_kernel,
        grid=(b // ns, tp // ta),
        in_specs=[pl.BlockSpec((ns, ta, d), lambda bi, ti: (bi, ti, 0)),
                  tok_rows(HALF_ROPE), tok_rows(HALF_ROPE),
                  _full((1, d)), _full((d, 2 * D_RNN)), _full((D_LATENT, d)),
                  _full((Q_LORA, ta)), _full((KV_LORA, ta)),
                  _full((N_HEADS * LANES, Q_LORA)), _full((N_HEADS * QK_NOPE, KV_LORA)),
                  _full((N_HEADS * V_HEAD, KV_LORA)),
                  _full((QK_HEAD, ta)), _full((QK_NOPE, ta)), _full((QK_ROPE, ta))],
        out_specs=[head_t(LANES),
                   pl.BlockSpec((ns, N_HEADS, ta, LANES), lambda bi, ti: (bi, 0, ti, 0)),
                   head_t(V_ROWS), rnn_rows, rnn_rows],
        out_shape=[jax.ShapeDtypeStruct((b, N_HEADS, LANES, tp), BF16),
                   jax.ShapeDtypeStruct((b, N_HEADS, tp, LANES), BF16),
                   jax.ShapeDtypeStruct((b, N_HEADS, V_ROWS, tp), BF16),
                   jax.ShapeDtypeStruct((b, tp, D_RNN), BF16),
                   jax.ShapeDtypeStruct((b, tp, D_RNN), BF16)],
        compiler_params=pltpu.CompilerParams(
            dimension_semantics=("parallel", "parallel"), vmem_limit_bytes=VMEM_LIMIT),
        name="pre_mixer",
    )(h, cos_t, sin_t, lw["g_mix"], lw["w_rnn"], lw["w_lat"], lw["g_q"], lw["g_kv"], lw["w_uq"],
      lw["w_uk"], lw["w_uv"], lw["g_qh"], lw["g_kn"], lw["g_kr"])


def _attn_kernel(qi_ref, kj_ref, qt_ref, k_ref, vt_ref, o_ref, m_sc, acc_sc, *, tq):
    qi = qi_ref[pl.program_id(1)]
    kj = kj_ref[pl.program_id(1)]

    @pl.when(kj == 0)
    def _():
        m_sc[...] = jnp.full(m_sc.shape, NEG_INF, F32)
        acc_sc[...] = jnp.zeros(acc_sc.shape, F32)

    nseq = qt_ref.shape[0]

    def step(diagonal):
        if diagonal:
            keep = (lax.broadcasted_iota(jnp.int32, (tq, tq), 0)
                    <= lax.broadcasted_iota(jnp.int32, (tq, tq), 1))
        for si in range(nseq):
            sts = [jnp.dot(k_ref[si, hh], qt_ref[si, hh], preferred_element_type=F32)
                   for hh in range(N_HEADS)]
            for hh in range(N_HEADS):
                st = jnp.where(keep, sts[hh], NEG_INF) if diagonal else sts[hh]
                m_prev = m_sc[si, hh]
                m_new = jnp.maximum(m_prev, jnp.max(st, axis=0, keepdims=True))
                alpha = jnp.exp2(m_prev - m_new)
                p = jnp.exp2(st - m_new).astype(BF16)
                m_sc[si, hh] = m_new
                acc_sc[si, hh] = alpha * acc_sc[si, hh] + jnp.dot(vt_ref[si, hh], p,
                                                                  preferred_element_type=F32)

    @pl.when(kj < qi)
    def _():
        step(False)

    @pl.when(kj == qi)
    def _():
        step(True)
        for si in range(nseq):
            outs = []
            for hh in range(N_HEADS):
                acc = acc_sc[si, hh]
                outs.append(acc[:V_HEAD] / acc[V_HEAD:V_HEAD + 1])
            o_ref[si] = jnp.concatenate(outs, axis=0).T.astype(BF16)


def _attention(qt, k, vt, tq):
    b, nh, tp, _ = k.shape
    nt = tp // tq
    vrows = vt.shape[2]
    ns = SEQS_PER_STEP if b % SEQS_PER_STEP == 0 else 1
    pairs = [(qi, kj) for qi in range(nt) for kj in range(qi + 1)]
    qi_tab = jnp.asarray([p[0] for p in pairs], jnp.int32)
    kj_tab = jnp.asarray([p[1] for p in pairs], jnp.int32)
    grid_spec = pltpu.PrefetchScalarGridSpec(
        num_scalar_prefetch=2,
        grid=(b // ns, len(pairs)),
        in_specs=[pl.BlockSpec((ns, nh, LANES, tq), lambda bi, p, qt_, kt_: (bi, 0, 0, qt_[p])),
                  pl.BlockSpec((ns, nh, tq, LANES), lambda bi, p, qt_, kt_: (bi, 0, kt_[p], 0)),
                  pl.BlockSpec((ns, nh, vrows, tq), lambda bi, p, qt_, kt_: (bi, 0, 0, kt_[p]))],
        out_specs=pl.BlockSpec((ns, tq, D_ATTN), lambda bi, p, qt_, kt_: (bi, qt_[p], 0)),
        scratch_shapes=[pltpu.VMEM((ns, nh, 1, tq), F32), pltpu.VMEM((ns, nh, vrows, tq), F32)],
    )
    return pl.pallas_call(
        functools.partial(_attn_kernel, tq=tq),
        grid_spec=grid_spec,
        out_shape=jax.ShapeDtypeStruct((b, tp, D_ATTN), BF16),
        compiler_params=pltpu.CompilerParams(
            dimension_semantics=("parallel", "arbitrary"), vmem_limit_bytes=VMEM_LIMIT),
        name="attention",
    )(qi_tab, kj_tab, qt, k, vt)


def _sigmoid(x):
    return 0.5 * jnp.tanh(0.5 * x) + 0.5


def _rnn_kernel(xr_ref, gt_ref, perm_ref, permt_ref, cw_ref, cb_ref, wai_ref, bai_ref, lam_ref,
                o_ref, xe_sc, gt_sc, a_sc, b_sc, h_sc, *, tt, nb, rows):
    halo = (CONV_W - 1) * nb
    grp = MXU_WIDTH // nb
    blk = grp * nb

    @pl.when(pl.program_id(0) == 0)
    def _():
        h_sc[...] = jnp.zeros(h_sc.shape, F32)
        xe_sc[pl.ds(0, halo), :] = jnp.zeros((halo, xe_sc.shape[1]), F32)

    def gather_rows(ref, g):
        return jnp.concatenate([ref[bi, g * grp:(g + 1) * grp, :] for bi in range(nb)], axis=0)

    perm = perm_ref[...]
    for g in range(tt // grp):
        xe_sc[pl.ds(halo + g * blk, blk), :] = jnp.dot(perm, gather_rows(xr_ref, g),
                                                      preferred_element_type=F32)
        gt_sc[pl.ds(g * blk, blk), :] = jnp.dot(perm, gather_rows(gt_ref, g),
                                               preferred_element_type=F32)
    lam = lam_ref[...]
    log_a_unit = -LRU_C * (jnp.maximum(-lam, 0.0) + jnp.log1p(jnp.exp(-jnp.abs(lam))))
    cw = cw_ref[...]
    for r0 in range(0, tt * nb, rows):
        xc = cb_ref[...] + xe_sc[pl.ds(r0, rows), :] * cw[0:1]
        for j in range(1, CONV_W):
            xc = xc + xe_sc[pl.ds(r0 + j * nb, rows), :] * cw[j:j + 1]
        gates = jnp.dot(xc.astype(BF16), wai_ref[...], preferred_element_type=F32) + bai_ref[...]
        r = _sigmoid(gates[:, :D_RNN])
        ig = _sigmoid(gates[:, D_RNN:])
        a = jnp.exp(r * log_a_unit)
        a_sc[pl.ds(r0, rows), :] = a
        b_sc[pl.ds(r0, rows), :] = jnp.sqrt(1.0 - a * a) * ig * xc
    xe_sc[pl.ds(0, halo), :] = xe_sc[pl.ds(tt * nb, halo), :]

    def scan_step(t, h):
        sl = pl.ds(pl.multiple_of(t * nb, nb), nb)
        h = a_sc[sl, :] * h + b_sc[sl, :]
        b_sc[sl, :] = h
        return h

    h_sc[...] = lax.fori_loop(0, tt, scan_step, h_sc[...], unroll=8)

    permt = permt_ref[...]
    for g in range(tt // grp):
        sl = pl.ds(g * blk, blk)
        y = (b_sc[sl, :] * jax.nn.gelu(gt_sc[sl, :], approximate=True)).astype(BF16)
        y = jnp.dot(permt, y, preferred_element_type=F32).astype(BF16)
        for bi in range(nb):
            o_ref[bi, g * grp:(g + 1) * grp, :] = y[bi * grp:(bi + 1) * grp]


def _rglru(xr, gt, lw, tt):
    nb, tp, c = xr.shape
    grp = MXU_WIDTH // nb
    src = (jnp.arange(MXU_WIDTH) % nb) * grp + jnp.arange(MXU_WIDTH) // nb
    perm = (src[:, None] == jnp.arange(MXU_WIDTH)[None, :]).astype(BF16)
    blk = pl.BlockSpec((nb, tt, c), lambda ti: (0, ti, 0))
    rows = math.gcd(tt * nb, 512)
    return pl.pallas_call(
        functools.partial(_rnn_kernel, tt=tt, nb=nb, rows=rows),
        grid=(tp // tt,),
        in_specs=[blk, blk, _full((MXU_WIDTH, MXU_WIDTH)), _full((MXU_WIDTH, MXU_WIDTH)),
                  _full((CONV_W, c)), _full((1, c)), _full((c, 2 * c)),
                  _full((1, 2 * c)), _full((1, c))],
        out_specs=blk,
        out_shape=jax.ShapeDtypeStruct((nb, tp, c), BF16),
        scratch_shapes=[pltpu.VMEM(((tt + CONV_W - 1) * nb, c), F32),
                        pltpu.VMEM((tt * nb, c), F32),
                        pltpu.VMEM((tt * nb, c), F32), pltpu.VMEM((tt * nb, c), F32),
                        pltpu.VMEM((nb, c), F32)],
        compiler_params=pltpu.CompilerParams(
            dimension_semantics=("arbitrary",), vmem_limit_bytes=VMEM_LIMIT),
        name="rglru",
    )(xr, gt, perm, perm.T, lw["conv_w"], lw["conv_b"], lw["w_ai"], lw["b_ai"], lw["lam"])


def _mix_residual(att, rnn, h, ga_ref, gr_ref, wo_ref):
    an = (_rms(att.astype(F32)) * ga_ref[...]).astype(BF16)
    rn = (_rms(rnn.astype(F32)) * gr_ref[...]).astype(BF16)
    mix = jnp.concatenate([an, rn], axis=-1)
    return h + jnp.dot(mix, wo_ref[...], preferred_element_type=F32)


def _post_router_kernel(att_ref, rnn_ref, h_ref, ga_ref, gr_ref, wo_ref, gf_ref, wr_ref,
                        ho_ref, u_ref, lg_ref):
    hn = _mix_residual(att_ref[...], rnn_ref[...], h_ref[...], ga_ref, gr_ref, wo_ref)
    ho_ref[...] = hn
    u = _rms(hn) * gf_ref[...]
    u_ref[...] = u.astype(BF16)
    w = wr_ref[...]
    u_hi = u.astype(BF16)
    w_hi = w.astype(BF16)
    u_lo = (u - u_hi.astype(F32)).astype(BF16)
    w_lo = (w - w_hi.astype(F32)).astype(BF16)
    hh_hl = jnp.dot(u_hi, jnp.concatenate([w_hi, w_lo], axis=1), preferred_element_type=F32)
    lg_ref[...] = (hh_hl[:, :LANES] + hh_hl[:, LANES:]
                   + jnp.dot(u_lo, w_hi, preferred_element_type=F32))


def _post_router(att, rnn, h, lw, w_router, tm):
    n, d = h.shape
    row = lambda c: pl.BlockSpec((tm, c), lambda i: (i, 0))
    return pl.pallas_call(
        _post_router_kernel,
        grid=(n // tm,),
        in_specs=[row(D_ATTN), row(D_RNN), row(d), _full((1, D_ATTN)), _full((1, D_RNN)),
                  _full((d, d)), _full((1, d)), _full((d, LANES))],
        out_specs=[row(d), row(d), row(LANES)],
        out_shape=[jax.ShapeDtypeStruct((n, d), F32), jax.ShapeDtypeStruct((n, d), BF16),
                   jax.ShapeDtypeStruct((n, LANES), F32)],
        compiler_params=pltpu.CompilerParams(
            dimension_semantics=("parallel",), vmem_limit_bytes=VMEM_LIMIT),
        name="post_router",
    )(att, rnn, h, lw["g_att"], lw["g_rnn"], lw["w_out"], lw["g_ffn"], w_router)


def _swiglu_tile(u, wg_ref, wu_ref, wd_ref, act_sc):
    f = act_sc.shape[-1]
    full = f - f % MXU_WIDTH
    for c0 in range(0, full, MXU_WIDTH):
        g = jnp.dot(u, wg_ref[:, c0:c0 + MXU_WIDTH], preferred_element_type=F32)
        up = jnp.dot(u, wu_ref[:, c0:c0 + MXU_WIDTH], preferred_element_type=F32)
        act_sc[:, c0:c0 + MXU_WIDTH] = (jax.nn.silu(g) * up).astype(BF16)
    if full < f:
        w = jnp.concatenate([wg_ref[:, full:], wu_ref[:, full:]], axis=1)
        gu = jnp.dot(u, w, preferred_element_type=F32)
        act_sc[:, full:] = (jax.nn.silu(gu[:, :f - full]) * gu[:, f - full:]).astype(BF16)
    return jnp.dot(act_sc[...], wd_ref[...], preferred_element_type=F32)


def _post_ffn_kernel(att_ref, rnn_ref, h_ref, ga_ref, gr_ref, wo_ref, gf_ref, wg_ref, wu_ref,
                     wd_ref, o_ref, act_sc):
    hn = _mix_residual(att_ref[...], rnn_ref[...], h_ref[...], ga_ref, gr_ref, wo_ref)
    u = (_rms(hn) * gf_ref[...]).astype(BF16)
    o_ref[...] = hn + _swiglu_tile(u, wg_ref, wu_ref, wd_ref, act_sc)


def _post_dense_ffn(att, rnn, h, lw, wg, wu, wd, tm):
    n, d = h.shape
    f = wd.shape[0]
    row = lambda c: pl.BlockSpec((tm, c), lambda i: (i, 0))
    once = lambda r, c: pl.BlockSpec((r, c), lambda i: (0, 0), pipeline_mode=pl.Buffered(1))
    return pl.pallas_call(
        _post_ffn_kernel,
        grid=(n // tm,),
        in_specs=[row(D_ATTN), row(D_RNN), row(d), _full((1, D_ATTN)), _full((1, D_RNN)),
                  once(d, d), _full((1, d)), once(d, f), once(d, f), once(f, d)],
        out_specs=row(d),
        out_shape=jax.ShapeDtypeStruct((n, d), F32),
        scratch_shapes=[pltpu.VMEM((tm, f), BF16)],
        compiler_params=pltpu.CompilerParams(
            dimension_semantics=("parallel",), vmem_limit_bytes=VMEM_LIMIT),
        name="post_dense_ffn",
    )(att, rnn, h, lw["g_att"], lw["g_rnn"], lw["w_out"], lw["g_ffn"], wg, wu, wd)


def _moe_kernel(te_ref, nt_ref, x_ref, wg_ref, wu_ref, wd_ref, o_ref, wg_sc, wu_sc, wd_sc, act_sc):
    i = pl.program_id(0)

    @pl.when((i == 0) | (te_ref[i] != te_ref[jnp.maximum(i - 1, 0)]))
    def _():
        wg_sc[...] = wg_ref[0, 0].astype(BF16)
        wu_sc[...] = wu_ref[0, 0].astype(BF16)
        wd_sc[...] = wd_ref[0, 0].astype(BF16)

    @pl.when(i < nt_ref[0])
    def _():
        o_ref[...] = _swiglu_tile(x_ref[...], wg_sc, wu_sc, wd_sc, act_sc).astype(BF16)

    @pl.when(i >= nt_ref[0])
    def _():
        o_ref[...] = jnp.zeros(o_ref.shape, BF16)


def _grouped_ffn(tile_expert, n_tiles, xs, wg, wu, wd, layer, tm):
    a, d = xs.shape
    f = wd.shape[2]
    by_expert = lambda r, c: pl.BlockSpec((1, 1, r, c), lambda i, te, nt: (layer, te[i], 0, 0))
    grid_spec = pltpu.PrefetchScalarGridSpec(
        num_scalar_prefetch=2,
        grid=(a // tm,),
        in_specs=[pl.BlockSpec((tm, d), lambda i, te, nt: (i, 0)),
                  by_expert(d, f), by_expert(d, f), by_expert(f, d)],
        out_specs=pl.BlockSpec((tm, d), lambda i, te, nt: (i, 0)),
        scratch_shapes=[pltpu.VMEM((d, f), BF16), pltpu.VMEM((d, f), BF16),
                        pltpu.VMEM((f, d), BF16), pltpu.VMEM((tm, f), BF16)],
    )
    return pl.pallas_call(
        _moe_kernel,
        grid_spec=grid_spec,
        out_shape=jax.ShapeDtypeStruct((a, d), BF16),
        compiler_params=pltpu.CompilerParams(
            dimension_semantics=("arbitrary",), vmem_limit_bytes=VMEM_LIMIT),
        name="moe_ffn",
    )(tile_expert, n_tiles, xs, wg, wu, wd)


def _moe_layer(u, h, logits, wg, wu, wd, layer, tm, keep=None):
    n, d = h.shape
    n2 = TOP_K * n
    experts = jnp.arange(N_EXPERTS, dtype=jnp.int32)
    tokens = jnp.arange(n, dtype=jnp.int32)
    lt = logits[:, :N_EXPERTS].T
    eid = experts[:, None]
    m1 = jnp.max(lt, axis=0)
    i1 = jnp.min(jnp.where(lt == m1, eid, N_EXPERTS), axis=0)
    lt2 = jnp.where(eid == i1, -jnp.inf, lt)
    m2 = jnp.max(lt2, axis=0)
    i2 = jnp.min(jnp.where(lt2 == m2, eid, N_EXPERTS), axis=0)
    e2 = jnp.exp(m2 - m1)
    top_w = jnp.stack([1.0 / (1.0 + e2), e2 / (1.0 + e2)], axis=-1)
    oh1, oh2 = eid == i1, eid == i2
    cnt = oh1.astype(jnp.int32) + oh2.astype(jnp.int32)
    csum = jnp.cumsum(cnt, axis=1)
    before = csum - cnt
    counts = csum[:, -1]
    padded = ((counts + tm - 1) // tm) * tm
    ends = jnp.cumsum(padded)
    starts = ends - padded
    slot_of = lambda oh: jnp.sum(jnp.where(oh, before + starts[:, None], 0), axis=0)
    pos = jnp.stack([slot_of(oh1), slot_of(oh2)], axis=-1)
    order = jnp.sort(jnp.concatenate([i1 * n + tokens, i2 * n + tokens])) % n
    a_pad = (n2 // tm + N_EXPERTS) * tm
    slot = jnp.arange(a_pad, dtype=jnp.int32)
    slot_onehot = (jnp.sum((slot[:, None] >= ends[None, :]).astype(jnp.int32), axis=-1,
                           keepdims=True) == experts[None, :]).astype(jnp.int32)
    r = slot - jnp.sum(slot_onehot * starts[None, :], axis=-1)
    valid = r < jnp.sum(slot_onehot * counts[None, :], axis=-1)
    dense = jnp.sum(slot_onehot * (jnp.cumsum(counts) - counts)[None, :], axis=-1) + r
    slot_tok = jnp.where(valid, jnp.take(order, dense, mode="clip"), slot % n)
    tile_start = jnp.arange(a_pad // tm, dtype=jnp.int32) * tm
    tile_expert = jnp.minimum(jnp.sum((tile_start[:, None] >= ends[None, :]).astype(jnp.int32),
                                      axis=-1), N_EXPERTS - 1).astype(jnp.int32)
    n_tiles = (ends[-1] // tm).astype(jnp.int32).reshape(1)
    xs = jnp.take(u, slot_tok, axis=0, mode="clip")
    y = _grouped_ffn(tile_expert, n_tiles, xs, wg, wu, wd, layer, tm)
    if keep is not None:
        b, tp, lo, hi = keep
        rows = lambda v: v.reshape(b, tp, v.shape[-1])[:, lo:hi].reshape(b * (hi - lo), v.shape[-1])
        h, pos, top_w = rows(h), rows(pos), rows(top_w)
    return (h + top_w[:, 0:1] * jnp.take(y, pos[:, 0], axis=0).astype(F32)
            + top_w[:, 1:2] * jnp.take(y, pos[:, 1], axis=0).astype(F32))


def _block_diag(w):
    nb, c, _ = w.shape
    eye = jnp.eye(nb, dtype=w.dtype)
    return (eye[:, None, :, None] * w[:, :, None, :]).reshape(nb * c, nb * c)


def _lane_table(g, ta):
    return jnp.broadcast_to(g[:, None], (g.shape[0], ta))


def _layer_weights(l, p, ta):
    w_in = p["w_in"][l]
    w_uq = p["w_uq"][l].reshape(Q_LORA, N_HEADS, QK_HEAD)
    w_uq = jnp.pad(w_uq, ((0, 0), (0, 0), (0, LANES - QK_HEAD))).reshape(Q_LORA, N_HEADS * LANES)
    w_ukv = p["w_ukv"][l].reshape(KV_LORA, N_HEADS, QK_NOPE + V_HEAD)
    w_uk = w_ukv[:, :, :QK_NOPE].reshape(KV_LORA, N_HEADS * QK_NOPE)
    w_uv = w_ukv[:, :, QK_NOPE:].reshape(KV_LORA, N_HEADS * V_HEAD)
    q_gain = p["q_head_g"][l] * (QK_HEAD ** -0.5 * math.log2(math.e))
    return {
        "g_mix": p["norm_mix_g"][l].reshape(1, -1),
        "w_rnn": w_in[:, D_LATENT:].astype(BF16),
        "w_lat": w_in[:, :D_LATENT].T.astype(BF16),
        "g_q": _lane_table(p["q_norm_g"][l], ta),
        "g_kv": _lane_table(p["kv_norm_g"][l], ta),
        "w_uq": w_uq.T.astype(BF16),
        "w_uk": w_uk.T.astype(BF16),
        "w_uv": w_uv.T.astype(BF16),
        "g_qh": _lane_table(q_gain, ta),
        "g_kn": _lane_table(p["k_head_g"][l][:QK_NOPE], ta),
        "g_kr": _lane_table(p["k_head_g"][l][QK_NOPE:], ta),
        "conv_w": p["conv_w"][l],
        "conv_b": p["conv_b"][l].reshape(1, -1),
        "w_ai": jnp.concatenate([_block_diag(p["w_a"][l]), _block_diag(p["w_i"][l])],
                                axis=1).astype(BF16),
        "b_ai": jnp.concatenate([p["b_a"][l], p["b_i"][l]]).reshape(1, -1),
        "lam": p["lru_lambda"][l].reshape(1, -1),
        "g_att": p["attn_out_g"][l].reshape(1, -1),
        "g_rnn": p["rnn_out_g"][l].reshape(1, -1),
        "w_out": p["w_out"][l].astype(BF16),
        "g_ffn": p["norm_ffn_g"][l].reshape(1, -1),
    }


def _rope_tables(pos_full):
    inv_freq = ROPE_BASE ** (-jnp.arange(0, QK_ROPE, 2, dtype=F32) / QK_ROPE)
    ang = pos_full.astype(F32)[:, None, :] * inv_freq[None, :, None]
    return jnp.cos(ang), jnp.sin(ang)


def _largest_tile(total, cap, unit):
    best = unit
    for cand in range(unit, cap + 1, unit):
        if total % cand == 0:
            best = cand
    return best


def _pick_tiles(t, b):
    tq = 384 if t > 1024 else LANES
    tp = -(-t // tq) * tq
    tm_row = _largest_tile(b * tp, 512, 256)
    return tp, tq, tm_row


def kernel(x, positions, meta_tokens, norm_mix_g, w_in, q_norm_g, w_uq, kv_norm_g, w_ukv, q_head_g, k_head_g, conv_w, conv_b, w_a, b_a, w_i, b_i, lru_lambda, attn_out_g, rnn_out_g, w_out, norm_ffn_g, ffn_w_gate, ffn_w_up, ffn_w_down, router_w, moe_w_gate, moe_w_up, moe_w_down):
    p = dict(norm_mix_g=norm_mix_g, w_in=w_in, q_norm_g=q_norm_g, w_uq=w_uq, kv_norm_g=kv_norm_g,
             w_ukv=w_ukv, q_head_g=q_head_g, k_head_g=k_head_g, conv_w=conv_w, conv_b=conv_b,
             w_a=w_a, b_a=b_a, w_i=w_i, b_i=b_i, lru_lambda=lru_lambda, attn_out_g=attn_out_g,
             rnn_out_g=rnn_out_g, w_out=w_out, norm_ffn_g=norm_ffn_g)
    b, seq, d = x.shape
    depth = w_in.shape[0]
    t = N_META + seq
    tp, tq, tm_row = _pick_tiles(t, b)
    n = b * tp

    meta = jnp.broadcast_to(meta_tokens[None].astype(x.dtype), (b, N_META, d))
    h = jnp.concatenate([meta, x, jnp.zeros((b, tp - t, d), x.dtype)], axis=1)
    meta_pos = jnp.broadcast_to(jnp.arange(N_META, dtype=jnp.int32)[None, :], (b, N_META))
    pos_full = jnp.concatenate([meta_pos, positions.astype(jnp.int32) + N_META,
                                jnp.zeros((b, tp - t), jnp.int32)], axis=1)
    cos_t, sin_t = _rope_tables(pos_full)

    for l in range(depth):
        lw = _layer_weights(l, p, tq)
        qt, k, vt, xr, gt = _pre_mixer(h, cos_t, sin_t, lw, tq)
        att = _attention(qt, k, vt, tq)
        rnn = _rglru(xr, gt, lw, tq)
        j = l // 2
        att, rnn, h = att.reshape(n, D_ATTN), rnn.reshape(n, D_RNN), h.reshape(n, d)
        if l % 2 == 0:
            h = _post_dense_ffn(att, rnn, h, lw, ffn_w_gate[j].astype(BF16),
                                ffn_w_up[j].astype(BF16), ffn_w_down[j].astype(BF16), tm_row)
        else:
            w_r = jnp.pad(router_w[j], ((0, 0), (0, LANES - N_EXPERTS)))
            h, u, logits = _post_router(att, rnn, h, lw, w_r, _largest_tile(n, 1024, 256))
            last = l == depth - 1
            h = _moe_layer(u, h, logits, moe_w_gate, moe_w_up, moe_w_down, j, tm_row,
                           keep=(b, tp, N_META, t) if last else None)
            if last:
                return h.reshape(b, seq, d)
        h = h.reshape(b, tp, d)
    return h[:, N_META:t]
```

```python
import functools
import math

import jax
import jax.numpy as jnp
from jax import lax
from jax.experimental import pallas as pl
from jax.experimental.pallas import tpu as pltpu

F32 = jnp.float32
BF16 = jnp.bfloat16

D_MODEL = 1024
N_META = 16
D_ATTN = D_MODEL // 2
D_RNN = D_MODEL - D_ATTN
N_HEADS = 8
QK_NOPE = 64
QK_ROPE = 32
QK_HEAD = QK_NOPE + QK_ROPE
V_HEAD = D_ATTN // N_HEADS
Q_LORA = 384
KV_LORA = 256
ROPE_BASE = 10000.0
N_RNN_BLOCKS = 8
RNN_BLOCK = D_RNN // N_RNN_BLOCKS
CONV_W = 4
LRU_C = 8.0
N_EXPERTS = 8
TOP_K = 2
EPS = 1e-6

LANES = 128
SUBLANES = 8
MXU_WIDTH = 256
BF16_ROWS = 16
HALF_ROPE = QK_ROPE // 2
V_ROWS = V_HEAD + BF16_ROWS
D_LATENT = Q_LORA + KV_LORA + QK_ROPE
SEQS_PER_STEP = 8
VMEM_LIMIT = 56 * 1024 * 1024
NEG_INF = -1e30


def _rms(x):
    return x * lax.rsqrt(jnp.mean(x * x, axis=-1, keepdims=True) + EPS)


def _rms_rows(x):
    return x * lax.rsqrt(jnp.mean(x * x, axis=0, keepdims=True) + EPS)


def _full(shape):
    return pl.BlockSpec(shape, lambda *_: (0,) * len(shape))


def _pre_kernel(h_ref, cos_ref, sin_ref, gmix_ref, wrnn_ref, wlat_ref, gq_ref, gkv_ref, wuq_ref,
                wuk_ref, wuv_ref, gqh_ref, gkn_ref, gkr_ref, q_ref, k_ref, v_ref, xr_ref, gt_ref):
    for si in range(h_ref.shape[0]):
        u = (_rms(h_ref[si]) * gmix_ref[...]).astype(BF16)
        rnn_in = jnp.dot(u, wrnn_ref[...], preferred_element_type=F32)
        xr_ref[si] = rnn_in[:, :D_RNN].astype(BF16)
        gt_ref[si] = rnn_in[:, D_RNN:].astype(BF16)

        lat = lax.dot_general(wlat_ref[...], u, (((1,), (1,)), ((), ())),
                              preferred_element_type=F32)
        cqn = (_rms_rows(lat[:Q_LORA]) * gq_ref[...]).astype(BF16)
        ckvn = (_rms_rows(lat[Q_LORA:Q_LORA + KV_LORA]) * gkv_ref[...]).astype(BF16)
        kr = lat[Q_LORA + KV_LORA:]
        qt = jnp.dot(wuq_ref[...], cqn, preferred_element_type=F32)
        kt = jnp.dot(wuk_ref[...], ckvn, preferred_element_type=F32)
        vt = jnp.dot(wuv_ref[...], ckvn, preferred_element_type=F32)
        ta = vt.shape[1]
        ones = jnp.ones((V_ROWS - V_HEAD, ta), BF16)
        for hh in range(N_HEADS):
            v_ref[si, hh, :V_HEAD, :] = vt[hh * V_HEAD:(hh + 1) * V_HEAD].astype(BF16)
            v_ref[si, hh, V_HEAD:, :] = ones

        cos, sin = cos_ref[si], sin_ref[si]
        inv_dh = 1.0 / QK_HEAD
        r1, r2 = QK_NOPE, QK_NOPE + HALF_ROPE
        gqh = gqh_ref[...]
        zeros_q = jnp.zeros((LANES - QK_HEAD, ta), BF16)
        for hh in range(N_HEADS):
            x = qt[hh * LANES:(hh + 1) * LANES]
            inv = lax.rsqrt(jnp.sum(x * x, axis=0, keepdims=True) * inv_dh + EPS)
            xn = x[:QK_HEAD] * gqh * inv
            x1, x2 = xn[r1:r2], xn[r2:]
            q_ref[si, hh, :r1, :] = xn[:r1].astype(BF16)
            q_ref[si, hh, r1:r2, :] = (x1 * cos - x2 * sin).astype(BF16)
            q_ref[si, hh, r2:QK_HEAD, :] = (x2 * cos + x1 * sin).astype(BF16)
            q_ref[si, hh, QK_HEAD:, :] = zeros_q

        kr_ss = jnp.sum(kr * kr, axis=0, keepdims=True)
        krg = kr * gkr_ref[...]
        k1, k2 = krg[:HALF_ROPE], krg[HALF_ROPE:]
        kr_rot = jnp.concatenate([k1 * cos - k2 * sin, k2 * cos + k1 * sin,
                                  jnp.zeros((LANES - QK_HEAD, ta), F32)], axis=0)
        gkn = gkn_ref[...]
        for hh in range(N_HEADS):
            x = kt[hh * QK_NOPE:(hh + 1) * QK_NOPE]
            inv = lax.rsqrt((jnp.sum(x * x, axis=0, keepdims=True) + kr_ss) * inv_dh + EPS)
            kh = jnp.concatenate([x * gkn, kr_rot], axis=0) * inv
            k_ref[si, hh] = kh.T.astype(BF16)


def _pre_mixer(h, cos_t, sin_t, lw, ta):
    b, tp, d = h.shape
    ns = 2 if b % 2 == 0 else 1
    tok_rows = lambda r: pl.BlockSpec((ns, r, ta), lambda bi, ti: (bi, 0, ti))
    head_t = lambda r: pl.BlockSpec((ns, N_HEADS, r, ta), lambda bi, ti: (bi, 0, 0, ti))
    rnn_rows = pl.BlockSpec((ns, ta, D_RNN), lambda bi, ti: (bi, ti, 0))
    return pl.pallas_call(
        _pre_kernel,
        grid=(b // ns, tp // ta),
        in_specs=[pl.BlockSpec((ns, ta, d), lambda bi, ti: (bi, ti, 0)),
                  tok_rows(HALF_ROPE), tok_rows(HALF_ROPE),
                  _full((1, d)), _full((d, 2 * D_RNN)), _full((D_LATENT, d)),
                  _full((Q_LORA, ta)), _full((KV_LORA, ta)),
                  _full((N_HEADS * LANES, Q_LORA)), _full((N_HEADS * QK_NOPE, KV_LORA)),
                  _full((N_HEADS * V_HEAD, KV_LORA)),
                  _full((QK_HEAD, ta)), _full((QK_NOPE, ta)), _full((QK_ROPE, ta))],
        out_specs=[head_t(LANES),
                   pl.BlockSpec((ns, N_HEADS, ta, LANES), lambda bi, ti: (bi, 0, ti, 0)),
                   head_t(V_ROWS), rnn_rows, rnn_rows],
        out_shape=[jax.ShapeDtypeStruct((b, N_HEADS, LANES, tp), BF16),
                   jax.ShapeDtypeStruct((b, N_HEADS, tp, LANES), BF16),
                   jax.ShapeDtypeStruct((b, N_HEADS, V_ROWS, tp), BF16),
                   jax.ShapeDtypeStruct((b, tp, D_RNN), BF16),
                   jax.ShapeDtypeStruct((b, tp, D_RNN), BF16)],
        compiler_params=pltpu.CompilerParams(
            dimension_semantics=("parallel", "parallel"), vmem_limit_bytes=VMEM_LIMIT),
        name="pre_mixer",
    )(h, cos_t, sin_t, lw["g_mix"], lw["w_rnn"], lw["w_lat"], lw["g_q"], lw["g_kv"], lw["w_uq"],
      lw["w_uk"], lw["w_uv"], lw["g_qh"], lw["g_kn"], lw["g_kr"])


def _attn_kernel(qi_ref, kj_ref, qt_ref, k_ref, vt_ref, o_ref, m_sc, acc_sc, *, tq):
    qi = qi_ref[pl.program_id(1)]
    kj = kj_ref[pl.program_id(1)]

    @pl.when(kj == 0)
    def _():
        m_sc[...] = jnp.full(m_sc.shape, NEG_INF, F32)
        acc_sc[...] = jnp.zeros(acc_sc.shape, F32)

    nseq = qt_ref.shape[0]

    def step(diagonal):
        if diagonal:
            keep = (lax.broadcasted_iota(jnp.int32, (tq, tq), 0)
                    <= lax.broadcasted_iota(jnp.int32, (tq, tq), 1))
        for si in range(nseq):
            sts = [jnp.dot(k_ref[si, hh], qt_ref[si, hh], preferred_element_type=F32)
                   for hh in range(N_HEADS)]
            for hh in range(N_HEADS):
                st = jnp.where(keep, sts[hh], NEG_INF) if diagonal else sts[hh]
                m_prev = m_sc[si, hh]
                m_new = jnp.maximum(m_prev, jnp.max(st, axis=0, keepdims=True))
                alpha = jnp.exp2(m_prev - m_new)
                p = jnp.exp2(st - m_new).astype(BF16)
                m_sc[si, hh] = m_new
                acc_sc[si, hh] = alpha * acc_sc[si, hh] + jnp.dot(vt_ref[si, hh], p,
                                                                  preferred_element_type=F32)

    @pl.when(kj < qi)
    def _():
        step(False)

    @pl.when(kj == qi)
    def _():
        step(True)
        for si in range(nseq):
            outs = []
            for hh in range(N_HEADS):
                acc = acc_sc[si, hh]
                outs.append(acc[:V_HEAD] / acc[V_HEAD:V_HEAD + 1])
            o_ref[si] = jnp.concatenate(outs, axis=0).T.astype(BF16)


def _attention(qt, k, vt, tq):
    b, nh, tp, _ = k.shape
    nt = tp // tq
    vrows = vt.shape[2]
    ns = SEQS_PER_STEP if b % SEQS_PER_STEP == 0 else 1
    pairs = [(qi, kj) for qi in range(nt) for kj in range(qi + 1)]
    qi_tab = jnp.asarray([p[0] for p in pairs], jnp.int32)
    kj_tab = jnp.asarray([p[1] for p in pairs], jnp.int32)
    grid_spec = pltpu.PrefetchScalarGridSpec(
        num_scalar_prefetch=2,
        grid=(b // ns, len(pairs)),
        in_specs=[pl.BlockSpec((ns, nh, LANES, tq), lambda bi, p, qt_, kt_: (bi, 0, 0, qt_[p])),
                  pl.BlockSpec((ns, nh, tq, LANES), lambda bi, p, qt_, kt_: (bi, 0, kt_[p], 0)),
                  pl.BlockSpec((ns, nh, vrows, tq), lambda bi, p, qt_, kt_: (bi, 0, 0, kt_[p]))],
        out_specs=pl.BlockSpec((ns, tq, D_ATTN), lambda bi, p, qt_, kt_: (bi, qt_[p], 0)),
        scratch_shapes=[pltpu.VMEM((ns, nh, 1, tq), F32), pltpu.VMEM((ns, nh, vrows, tq), F32)],
    )
    return pl.pallas_call(
        functools.partial(_attn_kernel, tq=tq),
        grid_spec=grid_spec,
        out_shape=jax.ShapeDtypeStruct((b, tp, D_ATTN), BF16),
        compiler_params=pltpu.CompilerParams(
            dimension_semantics=("parallel", "arbitrary"), vmem_limit_bytes=VMEM_LIMIT),
        name="attention",
    )(qi_tab, kj_tab, qt, k, vt)


def _sigmoid(x):
    return 0.5 * jnp.tanh(0.5 * x) + 0.5


def _rnn_kernel(xr_ref, gt_ref, perm_ref, permt_ref, cw_ref, cb_ref, wai_ref, bai_ref, lam_ref,
                o_ref, xe_sc, gt_sc, a_sc, b_sc, h_sc, *, tt, nb, rows):
    halo = (CONV_W - 1) * nb
    grp = MXU_WIDTH // nb
    blk = grp * nb

    @pl.when(pl.program_id(0) == 0)
    def _():
        h_sc[...] = jnp.zeros(h_sc.shape, F32)
        xe_sc[pl.ds(0, halo), :] = jnp.zeros((halo, xe_sc.shape[1]), F32)

    def gather_rows(ref, g):
        return jnp.concatenate([ref[bi, g * grp:(g + 1) * grp, :] for bi in range(nb)], axis=0)

    perm = perm_ref[...]
    for g in range(tt // grp):
        xe_sc[pl.ds(halo + g * blk, blk), :] = jnp.dot(perm, gather_rows(xr_ref, g),
                                                      preferred_element_type=F32)
        gt_sc[pl.ds(g * blk, blk), :] = jnp.dot(perm, gather_rows(gt_ref, g),
                                               preferred_element_type=F32)
    lam = lam_ref[...]
    log_a_unit = -LRU_C * (jnp.maximum(-lam, 0.0) + jnp.log1p(jnp.exp(-jnp.abs(lam))))
    half_log2_a = (0.5 * math.log2(math.e)) * log_a_unit
    cw = cw_ref[...]
    for r0 in range(0, tt * nb, rows):
        xc = cb_ref[...] + xe_sc[pl.ds(r0, rows), :] * cw[0:1]
        for j in range(1, CONV_W):
            xc = xc + xe_sc[pl.ds(r0 + j * nb, rows), :] * cw[j:j + 1]
        gates = jnp.dot(xc.astype(BF16), wai_ref[...], preferred_element_type=F32) + bai_ref[...]
        ig = _sigmoid(gates[:, D_RNN:])
        a = jnp.exp2(jnp.tanh(0.5 * gates[:, :D_RNN]) * half_log2_a + half_log2_a)
        a_sc[pl.ds(r0, rows), :] = a
        b_sc[pl.ds(r0, rows), :] = jnp.sqrt(1.0 - a * a) * ig * xc
    xe_sc[pl.ds(0, halo), :] = xe_sc[pl.ds(tt * nb, halo), :]

    def scan_step(t, h):
        sl = pl.ds(pl.multiple_of(t * nb, nb), nb)
        h = a_sc[sl, :] * h + b_sc[sl, :]
        b_sc[sl, :] = h
        return h

    h_sc[...] = lax.fori_loop(0, tt, scan_step, h_sc[...], unroll=8)

    permt = permt_ref[...]
    for g in range(tt // grp):
        sl = pl.ds(g * blk, blk)
        y = (b_sc[sl, :] * jax.nn.gelu(gt_sc[sl, :], approximate=True)).astype(BF16)
        y = jnp.dot(permt, y, preferred_element_type=F32).astype(BF16)
        for bi in range(nb):
            o_ref[bi, g * grp:(g + 1) * grp, :] = y[bi * grp:(bi + 1) * grp]


def _rglru(xr, gt, lw, tt):
    nb, tp, c = xr.shape
    grp = MXU_WIDTH // nb
    src = (jnp.arange(MXU_WIDTH) % nb) * grp + jnp.arange(MXU_WIDTH) // nb
    perm = (src[:, None] == jnp.arange(MXU_WIDTH)[None, :]).astype(BF16)
    blk = pl.BlockSpec((nb, tt, c), lambda ti: (0, ti, 0))
    rows = math.gcd(tt * nb, 512)
    return pl.pallas_call(
        functools.partial(_rnn_kernel, tt=tt, nb=nb, rows=rows),
        grid=(tp // tt,),
        in_specs=[blk, blk, _full((MXU_WIDTH, MXU_WIDTH)), _full((MXU_WIDTH, MXU_WIDTH)),
                  _full((CONV_W, c)), _full((1, c)), _full((c, 2 * c)),
                  _full((1, 2 * c)), _full((1, c))],
        out_specs=blk,
        out_shape=jax.ShapeDtypeStruct((nb, tp, c), BF16),
        scratch_shapes=[pltpu.VMEM(((tt + CONV_W - 1) * nb, c), F32),
                        pltpu.VMEM((tt * nb, c), F32),
                        pltpu.VMEM((tt * nb, c), F32), pltpu.VMEM((tt * nb, c), F32),
                        pltpu.VMEM((nb, c), F32)],
        compiler_params=pltpu.CompilerParams(
            dimension_semantics=("arbitrary",), vmem_limit_bytes=VMEM_LIMIT),
        name="rglru",
    )(xr, gt, perm, perm.T, lw["conv_w"], lw["conv_b"], lw["w_ai"], lw["b_ai"], lw["lam"])


def _mix_residual(att, rnn, h, ga_ref, gr_ref, wo_ref):
    an = (_rms(att.astype(F32)) * ga_ref[...]).astype(BF16)
    rn = (_rms(rnn.astype(F32)) * gr_ref[...]).astype(BF16)
    mix = jnp.concatenate([an, rn], axis=-1)
    return h + jnp.dot(mix, wo_ref[...], preferred_element_type=F32)


def _post_router_kernel(att_ref, rnn_ref, h_ref, ga_ref, gr_ref, wo_ref, gf_ref, wr_ref,
                        ho_ref, u_ref, lg_ref):
    hn = _mix_residual(att_ref[...], rnn_ref[...], h_ref[...], ga_ref, gr_ref, wo_ref)
    ho_ref[...] = hn
    u = _rms(hn) * gf_ref[...]
    u_ref[...] = u.astype(BF16)
    w = wr_ref[...]
    u_hi = u.astype(BF16)
    w_hi = w.astype(BF16)
    u_lo = (u - u_hi.astype(F32)).astype(BF16)
    w_lo = (w - w_hi.astype(F32)).astype(BF16)
    hh_hl = jnp.dot(u_hi, jnp.concatenate([w_hi, w_lo], axis=1), preferred_element_type=F32)
    lg_ref[...] = (hh_hl[:, :LANES] + hh_hl[:, LANES:]
                   + jnp.dot(u_lo, w_hi, preferred_element_type=F32))


def _post_router(att, rnn, h, lw, w_router, tm):
    n, d = h.shape
    row = lambda c: pl.BlockSpec((tm, c), lambda i: (i, 0))
    return pl.pallas_call(
        _post_router_kernel,
        grid=(n // tm,),
        in_specs=[row(D_ATTN), row(D_RNN), row(d), _full((1, D_ATTN)), _full((1, D_RNN)),
                  _full((d, d)), _full((1, d)), _full((d, LANES))],
        out_specs=[row(d), row(d), row(LANES)],
        out_shape=[jax.ShapeDtypeStruct((n, d), F32), jax.ShapeDtypeStruct((n, d), BF16),
                   jax.ShapeDtypeStruct((n, LANES), F32)],
        compiler_params=pltpu.CompilerParams(
            dimension_semantics=("parallel",), vmem_limit_bytes=VMEM_LIMIT),
        name="post_router",
    )(att, rnn, h, lw["g_att"], lw["g_rnn"], lw["w_out"], lw["g_ffn"], w_router)


def _swiglu_tile(u, wg_ref, wu_ref, wd_ref, act_sc):
    f = act_sc.shape[-1]
    full = f - f % MXU_WIDTH
    for c0 in range(0, full, MXU_WIDTH):
        g = jnp.dot(u, wg_ref[:, c0:c0 + MXU_WIDTH], preferred_element_type=F32)
        up = jnp.dot(u, wu_ref[:, c0:c0 + MXU_WIDTH], preferred_element_type=F32)
        act_sc[:, c0:c0 + MXU_WIDTH] = (jax.nn.silu(g) * up).astype(BF16)
    if full < f:
        w = jnp.concatenate([wg_ref[:, full:], wu_ref[:, full:]], axis=1)
        gu = jnp.dot(u, w, preferred_element_type=F32)
        act_sc[:, full:] = (jax.nn.silu(gu[:, :f - full]) * gu[:, f - full:]).astype(BF16)
    return jnp.dot(act_sc[...], wd_ref[...], preferred_element_type=F32)


def _post_ffn_kernel(att_ref, rnn_ref, h_ref, ga_ref, gr_ref, wo_ref, gf_ref, wg_ref, wu_ref,
                     wd_ref, o_ref, act_sc):
    hn = _mix_residual(att_ref[...], rnn_ref[...], h_ref[...], ga_ref, gr_ref, wo_ref)
    u = (_rms(hn) * gf_ref[...]).astype(BF16)
    o_ref[...] = hn + _swiglu_tile(u, wg_ref, wu_ref, wd_ref, act_sc)


def _post_dense_ffn(att, rnn, h, lw, wg, wu, wd, tm):
    n, d = h.shape
    f = wd.shape[0]
    row = lambda c: pl.BlockSpec((tm, c), lambda i: (i, 0))
    once = lambda r, c: pl.BlockSpec((r, c), lambda i: (0, 0), pipeline_mode=pl.Buffered(1))
    return pl.pallas_call(
        _post_ffn_kernel,
        grid=(n // tm,),
        in_specs=[row(D_ATTN), row(D_RNN), row(d), _full((1, D_ATTN)), _full((1, D_RNN)),
                  once(d, d), _full((1, d)), once(d, f), once(d, f), once(f, d)],
        out_specs=row(d),
        out_shape=jax.ShapeDtypeStruct((n, d), F32),
        scratch_shapes=[pltpu.VMEM((tm, f), BF16)],
        compiler_params=pltpu.CompilerParams(
            dimension_semantics=("parallel",), vmem_limit_bytes=VMEM_LIMIT),
        name="post_dense_ffn",
    )(att, rnn, h, lw["g_att"], lw["g_rnn"], lw["w_out"], lw["g_ffn"], wg, wu, wd)


def _moe_kernel(te_ref, nt_ref, x_ref, wg_ref, wu_ref, wd_ref, o_ref, wg_sc, wu_sc, wd_sc, act_sc):
    i = pl.program_id(0)

    @pl.when((i == 0) | (te_ref[i] != te_ref[jnp.maximum(i - 1, 0)]))
    def _():
        wg_sc[...] = wg_ref[0, 0].astype(BF16)
        wu_sc[...] = wu_ref[0, 0].astype(BF16)
        wd_sc[...] = wd_ref[0, 0].astype(BF16)

    @pl.when(i < nt_ref[0])
    def _():
        o_ref[...] = _swiglu_tile(x_ref[...], wg_sc, wu_sc, wd_sc, act_sc).astype(BF16)

    @pl.when(i >= nt_ref[0])
    def _():
        o_ref[...] = jnp.zeros(o_ref.shape, BF16)


def _grouped_ffn(tile_expert, n_tiles, xs, wg, wu, wd, layer, tm):
    a, d = xs.shape
    f = wd.shape[2]
    by_expert = lambda r, c: pl.BlockSpec((1, 1, r, c), lambda i, te, nt: (layer, te[i], 0, 0))
    grid_spec = pltpu.PrefetchScalarGridSpec(
        num_scalar_prefetch=2,
        grid=(a // tm,),
        in_specs=[pl.BlockSpec((tm, d), lambda i, te, nt: (i, 0)),
                  by_expert(d, f), by_expert(d, f), by_expert(f, d)],
        out_specs=pl.BlockSpec((tm, d), lambda i, te, nt: (i, 0)),
        scratch_shapes=[pltpu.VMEM((d, f), BF16), pltpu.VMEM((d, f), BF16),
                        pltpu.VMEM((f, d), BF16), pltpu.VMEM((tm, f), BF16)],
    )
    return pl.pallas_call(
        _moe_kernel,
        grid_spec=grid_spec,
        out_shape=jax.ShapeDtypeStruct((a, d), BF16),
        compiler_params=pltpu.CompilerParams(
            dimension_semantics=("arbitrary",), vmem_limit_bytes=VMEM_LIMIT),
        name="moe_ffn",
    )(tile_expert, n_tiles, xs, wg, wu, wd)


def _moe_layer(u, h, logits, wg, wu, wd, layer, tm, keep=None):
    n, d = h.shape
    n2 = TOP_K * n
    experts = jnp.arange(N_EXPERTS, dtype=jnp.int32)
    tokens = jnp.arange(n, dtype=jnp.int32)
    lt = logits[:, :N_EXPERTS].T
    eid = experts[:, None]
    m1 = jnp.max(lt, axis=0)
    i1 = jnp.min(jnp.where(lt == m1, eid, N_EXPERTS), axis=0)
    lt2 = jnp.where(eid == i1, -jnp.inf, lt)
    m2 = jnp.max(lt2, axis=0)
    i2 = jnp.min(jnp.where(lt2 == m2, eid, N_EXPERTS), axis=0)
    e2 = jnp.exp(m2 - m1)
    top_w = jnp.stack([1.0 / (1.0 + e2), e2 / (1.0 + e2)], axis=-1)
    oh1, oh2 = eid == i1, eid == i2
    cnt = oh1.astype(jnp.int32) + oh2.astype(jnp.int32)
    csum = jnp.cumsum(cnt, axis=1)
    before = csum - cnt
    counts = csum[:, -1]
    padded = ((counts + tm - 1) // tm) * tm
    ends = jnp.cumsum(padded)
    starts = ends - padded
    slot_of = lambda oh: jnp.sum(jnp.where(oh, before + starts[:, None], 0), axis=0)
    pos = jnp.stack([slot_of(oh1), slot_of(oh2)], axis=-1)
    order = jnp.sort(jnp.concatenate([i1 * n + tokens, i2 * n + tokens])) % n
    a_pad = (n2 // tm + N_EXPERTS) * tm
    slot = jnp.arange(a_pad, dtype=jnp.int32)
    slot_onehot = (jnp.sum((slot[:, None] >= ends[None, :]).astype(jnp.int32), axis=-1,
                           keepdims=True) == experts[None, :]).astype(jnp.int32)
    r = slot - jnp.sum(slot_onehot * starts[None, :], axis=-1)
    valid = r < jnp.sum(slot_onehot * counts[None, :], axis=-1)
    dense = jnp.sum(slot_onehot * (jnp.cumsum(counts) - counts)[None, :], axis=-1) + r
    slot_tok = jnp.where(valid, jnp.take(order, dense, mode="clip"), slot % n)
    tile_start = jnp.arange(a_pad // tm, dtype=jnp.int32) * tm
    tile_expert = jnp.minimum(jnp.sum((tile_start[:, None] >= ends[None, :]).astype(jnp.int32),
                                      axis=-1), N_EXPERTS - 1).astype(jnp.int32)
    n_tiles = (ends[-1] // tm).astype(jnp.int32).reshape(1)
    xs = jnp.take(u, slot_tok, axis=0, mode="clip")
    y = _grouped_ffn(tile_expert, n_tiles, xs, wg, wu, wd, layer, tm)
    if keep is not None:
        b, tp, lo, hi = keep
        rows = lambda v: v.reshape(b, tp, v.shape[-1])[:, lo:hi].reshape(b * (hi - lo), v.shape[-1])
        h, pos, top_w = rows(h), rows(pos), rows(top_w)
    return (h + top_w[:, 0:1] * jnp.take(y, pos[:, 0], axis=0).astype(F32)
            + top_w[:, 1:2] * jnp.take(y, pos[:, 1], axis=0).astype(F32))


def _block_diag(w):
    nb, c, _ = w.shape
    eye = jnp.eye(nb, dtype=w.dtype)
    return (eye[:, None, :, None] * w[:, :, None, :]).reshape(nb * c, nb * c)


def _lane_table(g, ta):
    return jnp.broadcast_to(g[:, None], (g.shape[0], ta))


def _layer_weights(l, p, ta):
    w_in = p["w_in"][l]
    w_uq = p["w_uq"][l].reshape(Q_LORA, N_HEADS, QK_HEAD)
    w_uq = jnp.pad(w_uq, ((0, 0), (0, 0), (0, LANES - QK_HEAD))).reshape(Q_LORA, N_HEADS * LANES)
    w_ukv = p["w_ukv"][l].reshape(KV_LORA, N_HEADS, QK_NOPE + V_HEAD)
    w_uk = w_ukv[:, :, :QK_NOPE].reshape(KV_LORA, N_HEADS * QK_NOPE)
    w_uv = w_ukv[:, :, QK_NOPE:].reshape(KV_LORA, N_HEADS * V_HEAD)
    q_gain = p["q_head_g"][l] * (QK_HEAD ** -0.5 * math.log2(math.e))
    return {
        "g_mix": p["norm_mix_g"][l].reshape(1, -1),
        "w_rnn": w_in[:, D_LATENT:].astype(BF16),
        "w_lat": w_in[:, :D_LATENT].T.astype(BF16),
        "g_q": _lane_table(p["q_norm_g"][l], ta),
        "g_kv": _lane_table(p["kv_norm_g"][l], ta),
        "w_uq": w_uq.T.astype(BF16),
        "w_uk": w_uk.T.astype(BF16),
        "w_uv": w_uv.T.astype(BF16),
        "g_qh": _lane_table(q_gain, ta),
        "g_kn": _lane_table(p["k_head_g"][l][:QK_NOPE], ta),
        "g_kr": _lane_table(p["k_head_g"][l][QK_NOPE:], ta),
        "conv_w": p["conv_w"][l],
        "conv_b": p["conv_b"][l].reshape(1, -1),
        "w_ai": jnp.concatenate([_block_diag(p["w_a"][l]), _block_diag(p["w_i"][l])],
                                axis=1).astype(BF16),
        "b_ai": jnp.concatenate([p["b_a"][l], p["b_i"][l]]).reshape(1, -1),
        "lam": p["lru_lambda"][l].reshape(1, -1),
        "g_att": p["attn_out_g"][l].reshape(1, -1),
        "g_rnn": p["rnn_out_g"][l].reshape(1, -1),
        "w_out": p["w_out"][l].astype(BF16),
        "g_ffn": p["norm_ffn_g"][l].reshape(1, -1),
    }


def _rope_tables(pos_full):
    inv_freq = ROPE_BASE ** (-jnp.arange(0, QK_ROPE, 2, dtype=F32) / QK_ROPE)
    ang = pos_full.astype(F32)[:, None, :] * inv_freq[None, :, None]
    return jnp.cos(ang), jnp.sin(ang)


def _largest_tile(total, cap, unit):
    best = unit
    for cand in range(unit, cap + 1, unit):
        if total % cand == 0:
            best = cand
    return best


def _pick_tiles(t, b):
    tq = 384 if t > 1024 else LANES
    tp = -(-t // tq) * tq
    tm_row = _largest_tile(b * tp, 512, 256)
    return tp, tq, tm_row


def kernel(x, positions, meta_tokens, norm_mix_g, w_in, q_norm_g, w_uq, kv_norm_g, w_ukv, q_head_g, k_head_g, conv_w, conv_b, w_a, b_a, w_i, b_i, lru_lambda, attn_out_g, rnn_out_g, w_out, norm_ffn_g, ffn_w_gate, ffn_w_up, ffn_w_down, router_w, moe_w_gate, moe_w_up, moe_w_down):
    p = dict(norm_mix_g=norm_mix_g, w_in=w_in, q_norm_g=q_norm_g, w_uq=w_uq, kv_norm_g=kv_norm_g,
             w_ukv=w_ukv, q_head_g=q_head_g, k_head_g=k_head_g, conv_w=conv_w, conv_b=conv_b,
             w_a=w_a, b_a=b_a, w_i=w_i, b_i=b_i, lru_lambda=lru_lambda, attn_out_g=attn_out_g,
             rnn_out_g=rnn_out_g, w_out=w_out, norm_ffn_g=norm_ffn_g)
    b, seq, d = x.shape
    depth = w_in.shape[0]
    t = N_META + seq
    tp, tq, tm_row = _pick_tiles(t, b)
    n = b * tp

    meta = jnp.broadcast_to(meta_tokens[None].astype(x.dtype), (b, N_META, d))
    h = jnp.concatenate([meta, x, jnp.zeros((b, tp - t, d), x.dtype)], axis=1)
    meta_pos = jnp.broadcast_to(jnp.arange(N_META, dtype=jnp.int32)[None, :], (b, N_META))
    pos_full = jnp.concatenate([meta_pos, positions.astype(jnp.int32) + N_META,
                                jnp.zeros((b, tp - t), jnp.int32)], axis=1)
    cos_t, sin_t = _rope_tables(pos_full)

    for l in range(depth):
        lw = _layer_weights(l, p, tq)
        qt, k, vt, xr, gt = _pre_mixer(h, cos_t, sin_t, lw, tq)
        att = _attention(qt, k, vt, tq)
        rnn = _rglru(xr, gt, lw, tq)
        j = l // 2
        att, rnn, h = att.reshape(n, D_ATTN), rnn.reshape(n, D_RNN), h.reshape(n, d)
        if l % 2 == 0:
            h = _post_dense_ffn(att, rnn, h, lw, ffn_w_gate[j].astype(BF16),
                                ffn_w_up[j].astype(BF16), ffn_w_down[j].astype(BF16), tm_row)
        else:
            w_r = jnp.pad(router_w[j], ((0, 0), (0, LANES - N_EXPERTS)))
            h, u, logits = _post_router(att, rnn, h, lw, w_r, _largest_tile(n, 1024, 256))
            last = l == depth - 1
            h = _moe_layer(u, h, logits, moe_w_gate, moe_w_up, moe_w_down, j, tm_row,
                           keep=(b, tp, N_META, t) if last else None)
            if last:
                return h.reshape(b, seq, d)
        h = h.reshape(b, tp, d)
    return h[:, N_META:t]
```

```python
import functools
import math

import jax
import jax.numpy as jnp
from jax import lax
from jax.experimental import pallas as pl
from jax.experimental.pallas import tpu as pltpu

F32 = jnp.float32
BF16 = jnp.bfloat16

D_MODEL = 1024
N_META = 16
D_ATTN = D_MODEL // 2
D_RNN = D_MODEL - D_ATTN
N_HEADS = 8
QK_NOPE = 64
QK_ROPE = 32
QK_HEAD = QK_NOPE + QK_ROPE
V_HEAD = D_ATTN // N_HEADS
Q_LORA = 384
KV_LORA = 256
ROPE_BASE = 10000.0
N_RNN_BLOCKS = 8
RNN_BLOCK = D_RNN // N_RNN_BLOCKS
CONV_W = 4
LRU_C = 8.0
N_EXPERTS = 8
TOP_K = 2
EPS = 1e-6

LANES = 128
SUBLANES = 8
MXU_WIDTH = 256
BF16_ROWS = 16
HALF_ROPE = QK_ROPE // 2
V_ROWS = V_HEAD + BF16_ROWS
D_LATENT = Q_LORA + KV_LORA + QK_ROPE
SEQS_PER_STEP = 8
VMEM_LIMIT = 56 * 1024 * 1024
NEG_INF = -1e30


def _rms(x):
    return x * lax.rsqrt(jnp.mean(x * x, axis=-1, keepdims=True) + EPS)


def _rms_rows(x):
    return x * lax.rsqrt(jnp.mean(x * x, axis=0, keepdims=True) + EPS)


def _full(shape):
    return pl.BlockSpec(shape, lambda *_: (0,) * len(shape))


def _pre_kernel(h_ref, cos_ref, sin_ref, gmix_ref, wrnn_ref, wlat_ref, gq_ref, gkv_ref, wuq_ref,
                wuk_ref, wuv_ref, gqh_ref, gkn_ref, gkr_ref, q_ref, k_ref, v_ref, xr_ref, gt_ref):
    for si in range(h_ref.shape[0]):
        u = (_rms(h_ref[si]) * gmix_ref[...]).astype(BF16)
        rnn_in = jnp.dot(u, wrnn_ref[...], preferred_element_type=F32)
        xr_ref[si] = rnn_in[:, :D_RNN].astype(BF16)
        gt_ref[si] = rnn_in[:, D_RNN:].astype(BF16)

        lat = lax.dot_general(wlat_ref[...], u, (((1,), (1,)), ((), ())),
                              preferred_element_type=F32)
        cqn = (_rms_rows(lat[:Q_LORA]) * gq_ref[...]).astype(BF16)
        ckvn = (_rms_rows(lat[Q_LORA:Q_LORA + KV_LORA]) * gkv_ref[...]).astype(BF16)
        kr = lat[Q_LORA + KV_LORA:]
        qt = jnp.dot(wuq_ref[...], cqn, preferred_element_type=F32)
        kt = jnp.dot(wuk_ref[...], ckvn, preferred_element_type=F32)
        vt = jnp.dot(wuv_ref[...], ckvn, preferred_element_type=F32)
        ta = vt.shape[1]
        ones = jnp.ones((V_ROWS - V_HEAD, ta), BF16)
        for hh in range(N_HEADS):
            v_ref[si, hh, :V_HEAD, :] = vt[hh * V_HEAD:(hh + 1) * V_HEAD].astype(BF16)
            v_ref[si, hh, V_HEAD:, :] = ones

        cos, sin = cos_ref[si], sin_ref[si]
        inv_dh = 1.0 / QK_HEAD
        r1, r2 = QK_NOPE, QK_NOPE + HALF_ROPE
        gqh = gqh_ref[...]
        zeros_q = jnp.zeros((LANES - QK_HEAD, ta), BF16)
        for hh in range(N_HEADS):
            x = qt[hh * LANES:(hh + 1) * LANES]
            inv = lax.rsqrt(jnp.sum(x * x, axis=0, keepdims=True) * inv_dh + EPS)
            xn = x[:QK_HEAD] * gqh * inv
            x1, x2 = xn[r1:r2], xn[r2:]
            q_ref[si, hh, :r1, :] = xn[:r1].astype(BF16)
            q_ref[si, hh, r1:r2, :] = (x1 * cos - x2 * sin).astype(BF16)
            q_ref[si, hh, r2:QK_HEAD, :] = (x2 * cos + x1 * sin).astype(BF16)
            q_ref[si, hh, QK_HEAD:, :] = zeros_q

        kr_ss = jnp.sum(kr * kr, axis=0, keepdims=True)
        krg = kr * gkr_ref[...]
        k1, k2 = krg[:HALF_ROPE], krg[HALF_ROPE:]
        kr_rot = jnp.concatenate([k1 * cos - k2 * sin, k2 * cos + k1 * sin,
                                  jnp.zeros((LANES - QK_HEAD, ta), F32)], axis=0)
        gkn = gkn_ref[...]
        for hh in range(N_HEADS):
            x = kt[hh * QK_NOPE:(hh + 1) * QK_NOPE]
            inv = lax.rsqrt((jnp.sum(x * x, axis=0, keepdims=True) + kr_ss) * inv_dh + EPS)
            kh = jnp.concatenate([x * gkn, kr_rot], axis=0) * inv
            k_ref[si, hh] = kh.T.astype(BF16)


def _pre_mixer(h, cos_t, sin_t, lw, ta):
    b, tp, d = h.shape
    ns = math.gcd(b, 4)
    tok_rows = lambda r: pl.BlockSpec((ns, r, ta), lambda bi, ti: (bi, 0, ti))
    head_t = lambda r: pl.BlockSpec((ns, N_HEADS, r, ta), lambda bi, ti: (bi, 0, 0, ti))
    rnn_rows = pl.BlockSpec((ns, ta, D_RNN), lambda bi, ti: (bi, ti, 0))
    return pl.pallas_call(
        _pre_kernel,
        grid=(b // ns, tp // ta),
        in_specs=[pl.BlockSpec((ns, ta, d), lambda bi, ti: (bi, ti, 0)),
                  tok_rows(HALF_ROPE), tok_rows(HALF_ROPE),
                  _full((1, d)), _full((d, 2 * D_RNN)), _full((D_LATENT, d)),
                  _full((Q_LORA, ta)), _full((KV_LORA, ta)),
                  _full((N_HEADS * LANES, Q_LORA)), _full((N_HEADS * QK_NOPE, KV_LORA)),
                  _full((N_HEADS * V_HEAD, KV_LORA)),
                  _full((QK_HEAD, ta)), _full((QK_NOPE, ta)), _full((QK_ROPE, ta))],
        out_specs=[head_t(LANES),
                   pl.BlockSpec((ns, N_HEADS, ta, LANES), lambda bi, ti: (bi, 0, ti, 0)),
                   head_t(V_ROWS), rnn_rows, rnn_rows],
        out_shape=[jax.ShapeDtypeStruct((b, N_HEADS, LANES, tp), BF16),
                   jax.ShapeDtypeStruct((b, N_HEADS, tp, LANES), BF16),
                   jax.ShapeDtypeStruct((b, N_HEADS, V_ROWS, tp), BF16),
                   jax.ShapeDtypeStruct((b, tp, D_RNN), BF16),
                   jax.ShapeDtypeStruct((b, tp, D_RNN), BF16)],
        compiler_params=pltpu.CompilerParams(
            dimension_semantics=("parallel", "parallel"), vmem_limit_bytes=VMEM_LIMIT),
        name="pre_mixer",
    )(h, cos_t, sin_t, lw["g_mix"], lw["w_rnn"], lw["w_lat"], lw["g_q"], lw["g_kv"], lw["w_uq"],
      lw["w_uk"], lw["w_uv"], lw["g_qh"], lw["g_kn"], lw["g_kr"])


def _attn_kernel(qi_ref, kj_ref, qt_ref, k_ref, vt_ref, o_ref, m_sc, acc_sc, *, tq):
    qi = qi_ref[pl.program_id(1)]
    kj = kj_ref[pl.program_id(1)]

    @pl.when(kj == 0)
    def _():
        m_sc[...] = jnp.full(m_sc.shape, NEG_INF, F32)
        acc_sc[...] = jnp.zeros(acc_sc.shape, F32)

    nseq = qt_ref.shape[0]

    def step(diagonal):
        if diagonal:
            keep = (lax.broadcasted_iota(jnp.int32, (tq, tq), 0)
                    <= lax.broadcasted_iota(jnp.int32, (tq, tq), 1))
        for si in range(nseq):
            sts = [jnp.dot(k_ref[si, hh], qt_ref[si, hh], preferred_element_type=F32)
                   for hh in range(N_HEADS)]
            for hh in range(N_HEADS):
                st = jnp.where(keep, sts[hh], NEG_INF) if diagonal else sts[hh]
                m_prev = m_sc[si, hh]
                m_new = jnp.maximum(m_prev, jnp.max(st, axis=0, keepdims=True))
                alpha = jnp.exp2(m_prev - m_new)
                p = jnp.exp2(st - m_new).astype(BF16)
                m_sc[si, hh] = m_new
                acc_sc[si, hh] = alpha * acc_sc[si, hh] + jnp.dot(vt_ref[si, hh], p,
                                                                  preferred_element_type=F32)

    @pl.when(kj < qi)
    def _():
        step(False)

    @pl.when(kj == qi)
    def _():
        step(True)
        for si in range(nseq):
            outs = []
            for hh in range(N_HEADS):
                acc = acc_sc[si, hh]
                outs.append(acc[:V_HEAD] / acc[V_HEAD:V_HEAD + 1])
            o_ref[si] = jnp.concatenate(outs, axis=0).T.astype(BF16)


def _attention(qt, k, vt, tq):
    b, nh, tp, _ = k.shape
    nt = tp // tq
    vrows = vt.shape[2]
    ns = SEQS_PER_STEP if b % SEQS_PER_STEP == 0 else 1
    pairs = [(qi, kj) for qi in range(nt) for kj in range(qi + 1)]
    qi_tab = jnp.asarray([p[0] for p in pairs], jnp.int32)
    kj_tab = jnp.asarray([p[1] for p in pairs], jnp.int32)
    grid_spec = pltpu.PrefetchScalarGridSpec(
        num_scalar_prefetch=2,
        grid=(b // ns, len(pairs)),
        in_specs=[pl.BlockSpec((ns, nh, LANES, tq), lambda bi, p, qt_, kt_: (bi, 0, 0, qt_[p])),
                  pl.BlockSpec((ns, nh, tq, LANES), lambda bi, p, qt_, kt_: (bi, 0, kt_[p], 0)),
                  pl.BlockSpec((ns, nh, vrows, tq), lambda bi, p, qt_, kt_: (bi, 0, 0, kt_[p]))],
        out_specs=pl.BlockSpec((ns, tq, D_ATTN), lambda bi, p, qt_, kt_: (bi, qt_[p], 0)),
        scratch_shapes=[pltpu.VMEM((ns, nh, 1, tq), F32), pltpu.VMEM((ns, nh, vrows, tq), F32)],
    )
    return pl.pallas_call(
        functools.partial(_attn_kernel, tq=tq),
        grid_spec=grid_spec,
        out_shape=jax.ShapeDtypeStruct((b, tp, D_ATTN), BF16),
        compiler_params=pltpu.CompilerParams(
            dimension_semantics=("parallel", "arbitrary"), vmem_limit_bytes=VMEM_LIMIT),
        name="attention",
    )(qi_tab, kj_tab, qt, k, vt)


def _sigmoid(x):
    return 0.5 * jnp.tanh(0.5 * x) + 0.5


def _gelu_tanh(x):
    k = math.sqrt(2.0 / math.pi)
    return x * (0.5 + 0.5 * jnp.tanh(x * (k + (k * 0.044715) * (x * x))))


def _rnn_kernel(xr_ref, gt_ref, perm_ref, permt_ref, cw_ref, cb_ref, wai_ref, bai_ref, lam_ref,
                o_ref, xe_sc, gt_sc, a_sc, b_sc, h_sc, *, tt, nb, rows):
    halo = (CONV_W - 1) * nb
    grp = MXU_WIDTH // nb
    blk = grp * nb

    @pl.when(pl.program_id(0) == 0)
    def _():
        h_sc[...] = jnp.zeros(h_sc.shape, F32)
        xe_sc[pl.ds(0, halo), :] = jnp.zeros((halo, xe_sc.shape[1]), F32)

    def gather_rows(ref, g):
        return jnp.concatenate([ref[bi, g * grp:(g + 1) * grp, :] for bi in range(nb)], axis=0)

    perm = perm_ref[...]
    for g in range(tt // grp):
        xe_sc[pl.ds(halo + g * blk, blk), :] = jnp.dot(perm, gather_rows(xr_ref, g),
                                                      preferred_element_type=F32)
        gt_sc[pl.ds(g * blk, blk), :] = jnp.dot(perm, gather_rows(gt_ref, g),
                                               preferred_element_type=F32)
    lam = lam_ref[...]
    log_a_unit = -LRU_C * (jnp.maximum(-lam, 0.0) + jnp.log1p(jnp.exp(-jnp.abs(lam))))
    half_log2_a = (0.5 * math.log2(math.e)) * log_a_unit
    cw = cw_ref[...]
    for r0 in range(0, tt * nb, rows):
        xc = cb_ref[...] + xe_sc[pl.ds(r0, rows), :] * cw[0:1]
        for j in range(1, CONV_W):
            xc = xc + xe_sc[pl.ds(r0 + j * nb, rows), :] * cw[j:j + 1]
        gates = jnp.dot(xc.astype(BF16), wai_ref[...], preferred_element_type=F32) + bai_ref[...]
        ig = _sigmoid(gates[:, D_RNN:])
        a = jnp.exp2(jnp.tanh(0.5 * gates[:, :D_RNN]) * half_log2_a + half_log2_a)
        a_sc[pl.ds(r0, rows), :] = a
        b_sc[pl.ds(r0, rows), :] = jnp.sqrt(1.0 - a * a) * ig * xc
    xe_sc[pl.ds(0, halo), :] = xe_sc[pl.ds(tt * nb, halo), :]

    def scan_step(t, h):
        sl = pl.ds(pl.multiple_of(t * nb, nb), nb)
        h = a_sc[sl, :] * h + b_sc[sl, :]
        b_sc[sl, :] = h
        return h

    h_sc[...] = lax.fori_loop(0, tt, scan_step, h_sc[...], unroll=8)

    permt = permt_ref[...]
    for g in range(tt // grp):
        sl = pl.ds(g * blk, blk)
        y = (b_sc[sl, :] * _gelu_tanh(gt_sc[sl, :])).astype(BF16)
        y = jnp.dot(permt, y, preferred_element_type=F32).astype(BF16)
        for bi in range(nb):
            o_ref[bi, g * grp:(g + 1) * grp, :] = y[bi * grp:(bi + 1) * grp]


def _rglru(xr, gt, lw, tt):
    nb, tp, c = xr.shape
    grp = MXU_WIDTH // nb
    src = (jnp.arange(MXU_WIDTH) % nb) * grp + jnp.arange(MXU_WIDTH) // nb
    perm = (src[:, None] == jnp.arange(MXU_WIDTH)[None, :]).astype(BF16)
    blk = pl.BlockSpec((nb, tt, c), lambda ti: (0, ti, 0))
    rows = math.gcd(tt * nb, 512)
    return pl.pallas_call(
        functools.partial(_rnn_kernel, tt=tt, nb=nb, rows=rows),
        grid=(tp // tt,),
        in_specs=[blk, blk, _full((MXU_WIDTH, MXU_WIDTH)), _full((MXU_WIDTH, MXU_WIDTH)),
                  _full((CONV_W, c)), _full((1, c)), _full((c, 2 * c)),
                  _full((1, 2 * c)), _full((1, c))],
        out_specs=blk,
        out_shape=jax.ShapeDtypeStruct((nb, tp, c), BF16),
        scratch_shapes=[pltpu.VMEM(((tt + CONV_W - 1) * nb, c), F32),
                        pltpu.VMEM((tt * nb, c), F32),
                        pltpu.VMEM((tt * nb, c), F32), pltpu.VMEM((tt * nb, c), F32),
                        pltpu.VMEM((nb, c), F32)],
        compiler_params=pltpu.CompilerParams(
            dimension_semantics=("arbitrary",), vmem_limit_bytes=VMEM_LIMIT),
        name="rglru",
    )(xr, gt, perm, perm.T, lw["conv_w"], lw["conv_b"], lw["w_ai"], lw["b_ai"], lw["lam"])


def _mix_residual(att, rnn, h, ga_ref, gr_ref, wo_ref):
    an = (_rms(att.astype(F32)) * ga_ref[...]).astype(BF16)
    rn = (_rms(rnn.astype(F32)) * gr_ref[...]).astype(BF16)
    mix = jnp.concatenate([an, rn], axis=-1)
    return h + jnp.dot(mix, wo_ref[...], preferred_element_type=F32)


def _post_router_kernel(att_ref, rnn_ref, h_ref, ga_ref, gr_ref, wo_ref, gf_ref, wr_ref,
                        ho_ref, u_ref, lg_ref):
    hn = _mix_residual(att_ref[...], rnn_ref[...], h_ref[...], ga_ref, gr_ref, wo_ref)
    ho_ref[...] = hn
    u = _rms(hn) * gf_ref[...]
    u_ref[...] = u.astype(BF16)
    w = wr_ref[...]
    u_hi = u.astype(BF16)
    w_hi = w.astype(BF16)
    u_lo = (u - u_hi.astype(F32)).astype(BF16)
    w_lo = (w - w_hi.astype(F32)).astype(BF16)
    hh_hl = jnp.dot(u_hi, jnp.concatenate([w_hi, w_lo], axis=1), preferred_element_type=F32)
    lg_ref[...] = (hh_hl[:, :LANES] + hh_hl[:, LANES:]
                   + jnp.dot(u_lo, w_hi, preferred_element_type=F32))


def _post_router(att, rnn, h, lw, w_router, tm):
    n, d = h.shape
    row = lambda c: pl.BlockSpec((tm, c), lambda i: (i, 0))
    return pl.pallas_call(
        _post_router_kernel,
        grid=(n // tm,),
        in_specs=[row(D_ATTN), row(D_RNN), row(d), _full((1, D_ATTN)), _full((1, D_RNN)),
                  _full((d, d)), _full((1, d)), _full((d, LANES))],
        out_specs=[row(d), row(d), row(LANES)],
        out_shape=[jax.ShapeDtypeStruct((n, d), F32), jax.ShapeDtypeStruct((n, d), BF16),
                   jax.ShapeDtypeStruct((n, LANES), F32)],
        compiler_params=pltpu.CompilerParams(
            dimension_semantics=("parallel",), vmem_limit_bytes=VMEM_LIMIT),
        name="post_router",
    )(att, rnn, h, lw["g_att"], lw["g_rnn"], lw["w_out"], lw["g_ffn"], w_router)


def _swiglu_tile(u, wg_ref, wu_ref, wd_ref, act_sc):
    f = act_sc.shape[-1]
    full = f - f % MXU_WIDTH
    for c0 in range(0, full, MXU_WIDTH):
        g = jnp.dot(u, wg_ref[:, c0:c0 + MXU_WIDTH], preferred_element_type=F32)
        up = jnp.dot(u, wu_ref[:, c0:c0 + MXU_WIDTH], preferred_element_type=F32)
        act_sc[:, c0:c0 + MXU_WIDTH] = (jax.nn.silu(g) * up).astype(BF16)
    if full < f:
        w = jnp.concatenate([wg_ref[:, full:], wu_ref[:, full:]], axis=1)
        gu = jnp.dot(u, w, preferred_element_type=F32)
        act_sc[:, full:] = (jax.nn.silu(gu[:, :f - full]) * gu[:, f - full:]).astype(BF16)
    return jnp.dot(act_sc[...], wd_ref[...], preferred_element_type=F32)


def _post_ffn_kernel(att_ref, rnn_ref, h_ref, ga_ref, gr_ref, wo_ref, gf_ref, wg_ref, wu_ref,
                     wd_ref, o_ref, act_sc):
    hn = _mix_residual(att_ref[...], rnn_ref[...], h_ref[...], ga_ref, gr_ref, wo_ref)
    u = (_rms(hn) * gf_ref[...]).astype(BF16)
    o_ref[...] = hn + _swiglu_tile(u, wg_ref, wu_ref, wd_ref, act_sc)


def _post_dense_ffn(att, rnn, h, lw, wg, wu, wd, tm):
    n, d = h.shape
    f = wd.shape[0]
    row = lambda c: pl.BlockSpec((tm, c), lambda i: (i, 0))
    once = lambda r, c: pl.BlockSpec((r, c), lambda i: (0, 0), pipeline_mode=pl.Buffered(1))
    return pl.pallas_call(
        _post_ffn_kernel,
        grid=(n // tm,),
        in_specs=[row(D_ATTN), row(D_RNN), row(d), _full((1, D_ATTN)), _full((1, D_RNN)),
                  once(d, d), _full((1, d)), once(d, f), once(d, f), once(f, d)],
        out_specs=row(d),
        out_shape=jax.ShapeDtypeStruct((n, d), F32),
        scratch_shapes=[pltpu.VMEM((tm, f), BF16)],
        compiler_params=pltpu.CompilerParams(
            dimension_semantics=("parallel",), vmem_limit_bytes=VMEM_LIMIT),
        name="post_dense_ffn",
    )(att, rnn, h, lw["g_att"], lw["g_rnn"], lw["w_out"], lw["g_ffn"], wg, wu, wd)


def _moe_kernel(te_ref, nt_ref, x_ref, wg_ref, wu_ref, wd_ref, o_ref, wg_sc, wu_sc, wd_sc, act_sc):
    i = pl.program_id(0)

    @pl.when((i == 0) | (te_ref[i] != te_ref[jnp.maximum(i - 1, 0)]))
    def _():
        wg_sc[...] = wg_ref[0, 0].astype(BF16)
        wu_sc[...] = wu_ref[0, 0].astype(BF16)
        wd_sc[...] = wd_ref[0, 0].astype(BF16)

    @pl.when(i < nt_ref[0])
    def _():
        o_ref[...] = _swiglu_tile(x_ref[...], wg_sc, wu_sc, wd_sc, act_sc).astype(BF16)

    @pl.when(i >= nt_ref[0])
    def _():
        o_ref[...] = jnp.zeros(o_ref.shape, BF16)


def _grouped_ffn(tile_expert, n_tiles, xs, wg, wu, wd, layer, tm):
    a, d = xs.shape
    f = wd.shape[2]
    by_expert = lambda r, c: pl.BlockSpec((1, 1, r, c), lambda i, te, nt: (layer, te[i], 0, 0))
    grid_spec = pltpu.PrefetchScalarGridSpec(
        num_scalar_prefetch=2,
        grid=(a // tm,),
        in_specs=[pl.BlockSpec((tm, d), lambda i, te, nt: (i, 0)),
                  by_expert(d, f), by_expert(d, f), by_expert(f, d)],
        out_specs=pl.BlockSpec((tm, d), lambda i, te, nt: (i, 0)),
        scratch_shapes=[pltpu.VMEM((d, f), BF16), pltpu.VMEM((d, f), BF16),
                        pltpu.VMEM((f, d), BF16), pltpu.VMEM((tm, f), BF16)],
    )
    return pl.pallas_call(
        _moe_kernel,
        grid_spec=grid_spec,
        out_shape=jax.ShapeDtypeStruct((a, d), BF16),
        compiler_params=pltpu.CompilerParams(
            dimension_semantics=("arbitrary",), vmem_limit_bytes=VMEM_LIMIT),
        name="moe_ffn",
    )(tile_expert, n_tiles, xs, wg, wu, wd)


def _moe_layer(u, h, logits, wg, wu, wd, layer, tm, keep=None):
    n, d = h.shape
    n2 = TOP_K * n
    experts = jnp.arange(N_EXPERTS, dtype=jnp.int32)
    tokens = jnp.arange(n, dtype=jnp.int32)
    lt = logits[:, :N_EXPERTS].T
    eid = experts[:, None]
    m1 = jnp.max(lt, axis=0)
    i1 = jnp.min(jnp.where(lt == m1, eid, N_EXPERTS), axis=0)
    lt2 = jnp.where(eid == i1, -jnp.inf, lt)
    m2 = jnp.max(lt2, axis=0)
    i2 = jnp.min(jnp.where(lt2 == m2, eid, N_EXPERTS), axis=0)
    e2 = jnp.exp(m2 - m1)
    top_w = jnp.stack([1.0 / (1.0 + e2), e2 / (1.0 + e2)], axis=-1)
    oh1, oh2 = eid == i1, eid == i2
    cnt = oh1.astype(jnp.int32) + oh2.astype(jnp.int32)
    csum = jnp.cumsum(cnt, axis=1)
    before = csum - cnt
    counts = csum[:, -1]
    padded = ((counts + tm - 1) // tm) * tm
    ends = jnp.cumsum(padded)
    starts = ends - padded
    slot_of = lambda oh: jnp.sum(jnp.where(oh, before + starts[:, None], 0), axis=0)
    pos = jnp.stack([slot_of(oh1), slot_of(oh2)], axis=-1)
    order = jnp.sort(jnp.concatenate([i1 * n + tokens, i2 * n + tokens])) % n
    a_pad = (n2 // tm + N_EXPERTS) * tm
    slot = jnp.arange(a_pad, dtype=jnp.int32)
    slot_onehot = (jnp.sum((slot[:, None] >= ends[None, :]).astype(jnp.int32), axis=-1,
                           keepdims=True) == experts[None, :]).astype(jnp.int32)
    r = slot - jnp.sum(slot_onehot * starts[None, :], axis=-1)
    valid = r < jnp.sum(slot_onehot * counts[None, :], axis=-1)
    dense = jnp.sum(slot_onehot * (jnp.cumsum(counts) - counts)[None, :], axis=-1) + r
    slot_tok = jnp.where(valid, jnp.take(order, dense, mode="clip"), slot % n)
    tile_start = jnp.arange(a_pad // tm, dtype=jnp.int32) * tm
    tile_expert = jnp.minimum(jnp.sum((tile_start[:, None] >= ends[None, :]).astype(jnp.int32),
                                      axis=-1), N_EXPERTS - 1).astype(jnp.int32)
    n_tiles = (ends[-1] // tm).astype(jnp.int32).reshape(1)
    xs = jnp.take(u, slot_tok, axis=0, mode="clip")
    y = _grouped_ffn(tile_expert, n_tiles, xs, wg, wu, wd, layer, tm)
    if keep is not None:
        b, tp, lo, hi = keep
        rows = lambda v: v.reshape(b, tp, v.shape[-1])[:, lo:hi].reshape(b * (hi - lo), v.shape[-1])
        h, pos, top_w = rows(h), rows(pos), rows(top_w)
    return (h + top_w[:, 0:1] * jnp.take(y, pos[:, 0], axis=0).astype(F32)
            + top_w[:, 1:2] * jnp.take(y, pos[:, 1], axis=0).astype(F32))


def _block_diag(w):
    nb, c, _ = w.shape
    eye = jnp.eye(nb, dtype=w.dtype)
    return (eye[:, None, :, None] * w[:, :, None, :]).reshape(nb * c, nb * c)


def _lane_table(g, ta):
    return jnp.broadcast_to(g[:, None], (g.shape[0], ta))


def _layer_weights(l, p, ta):
    w_in = p["w_in"][l]
    w_uq = p["w_uq"][l].reshape(Q_LORA, N_HEADS, QK_HEAD)
    w_uq = jnp.pad(w_uq, ((0, 0), (0, 0), (0, LANES - QK_HEAD))).reshape(Q_LORA, N_HEADS * LANES)
    w_ukv = p["w_ukv"][l].reshape(KV_LORA, N_HEADS, QK_NOPE + V_HEAD)
    w_uk = w_ukv[:, :, :QK_NOPE].reshape(KV_LORA, N_HEADS * QK_NOPE)
    w_uv = w_ukv[:, :, QK_NOPE:].reshape(KV_LORA, N_HEADS * V_HEAD)
    q_gain = p["q_head_g"][l] * (QK_HEAD ** -0.5 * math.log2(math.e))
    return {
        "g_mix": p["norm_mix_g"][l].reshape(1, -1),
        "w_rnn": w_in[:, D_LATENT:].astype(BF16),
        "w_lat": w_in[:, :D_LATENT].T.astype(BF16),
        "g_q": _lane_table(p["q_norm_g"][l], ta),
        "g_kv": _lane_table(p["kv_norm_g"][l], ta),
        "w_uq": w_uq.T.astype(BF16),
        "w_uk": w_uk.T.astype(BF16),
        "w_uv": w_uv.T.astype(BF16),
        "g_qh": _lane_table(q_gain, ta),
        "g_kn": _lane_table(p["k_head_g"][l][:QK_NOPE], ta),
        "g_kr": _lane_table(p["k_head_g"][l][QK_NOPE:], ta),
        "conv_w": p["conv_w"][l],
        "conv_b": p["conv_b"][l].reshape(1, -1),
        "w_ai": jnp.concatenate([_block_diag(p["w_a"][l]), _block_diag(p["w_i"][l])],
                                axis=1).astype(BF16),
        "b_ai": jnp.concatenate([p["b_a"][l], p["b_i"][l]]).reshape(1, -1),
        "lam": p["lru_lambda"][l].reshape(1, -1),
        "g_att": p["attn_out_g"][l].reshape(1, -1),
        "g_rnn": p["rnn_out_g"][l].reshape(1, -1),
        "w_out": p["w_out"][l].astype(BF16),
        "g_ffn": p["norm_ffn_g"][l].reshape(1, -1),
    }


def _rope_tables(pos_full):
    inv_freq = ROPE_BASE ** (-jnp.arange(0, QK_ROPE, 2, dtype=F32) / QK_ROPE)
    ang = pos_full.astype(F32)[:, None, :] * inv_freq[None, :, None]
    return jnp.cos(ang), jnp.sin(ang)


def _largest_tile(total, cap, unit):
    best = unit
    for cand in range(unit, cap + 1, unit):
        if total % cand == 0:
            best = cand
    return best


def _pick_tiles(t, b):
    tq = 384 if t > 1024 else LANES
    tp = -(-t // tq) * tq
    tm_row = _largest_tile(b * tp, 512, 256)
    return tp, tq, tm_row


def kernel(x, positions, meta_tokens, norm_mix_g, w_in, q_norm_g, w_uq, kv_norm_g, w_ukv, q_head_g, k_head_g, conv_w, conv_b, w_a, b_a, w_i, b_i, lru_lambda, attn_out_g, rnn_out_g, w_out, norm_ffn_g, ffn_w_gate, ffn_w_up, ffn_w_down, router_w, moe_w_gate, moe_w_up, moe_w_down):
    p = dict(norm_mix_g=norm_mix_g, w_in=w_in, q_norm_g=q_norm_g, w_uq=w_uq, kv_norm_g=kv_norm_g,
             w_ukv=w_ukv, q_head_g=q_head_g, k_head_g=k_head_g, conv_w=conv_w, conv_b=conv_b,
             w_a=w_a, b_a=b_a, w_i=w_i, b_i=b_i, lru_lambda=lru_lambda, attn_out_g=attn_out_g,
             rnn_out_g=rnn_out_g, w_out=w_out, norm_ffn_g=norm_ffn_g)
    b, seq, d = x.shape
    depth = w_in.shape[0]
    t = N_META + seq
    tp, tq, tm_row = _pick_tiles(t, b)
    n = b * tp

    meta = jnp.broadcast_to(meta_tokens[None].astype(x.dtype), (b, N_META, d))
    h = jnp.concatenate([meta, x, jnp.zeros((b, tp - t, d), x.dtype)], axis=1)
    meta_pos = jnp.broadcast_to(jnp.arange(N_META, dtype=jnp.int32)[None, :], (b, N_META))
    pos_full = jnp.concatenate([meta_pos, positions.astype(jnp.int32) + N_META,
                                jnp.zeros((b, tp - t), jnp.int32)], axis=1)
    cos_t, sin_t = _rope_tables(pos_full)

    for l in range(depth):
        lw = _layer_weights(l, p, tq)
        qt, k, vt, xr, gt = _pre_mixer(h, cos_t, sin_t, lw, tq)
        att = _attention(qt, k, vt, tq)
        rnn = _rglru(xr, gt, lw, tq)
        j = l // 2
        att, rnn, h = att.reshape(n, D_ATTN), rnn.reshape(n, D_RNN), h.reshape(n, d)
        if l % 2 == 0:
            h = _post_dense_ffn(att, rnn, h, lw, ffn_w_gate[j].astype(BF16),
                                ffn_w_up[j].astype(BF16), ffn_w_down[j].astype(BF16), tm_row)
        else:
            w_r = jnp.pad(router_w[j], ((0, 0), (0, LANES - N_EXPERTS)))
            h, u, logits = _post_router(att, rnn, h, lw, w_r, _largest_tile(n, 1024, 256))
            last = l == depth - 1
            h = _moe_layer(u, h, logits, moe_w_gate, moe_w_up, moe_w_down, j, tm_row,
                           keep=(b, tp, N_META, t) if last else None)
            if last:
                return h.reshape(b, seq, d)
        h = h.reshape(b, tp, d)
    return h[:, N_META:t]
```

```python
import functools
import math

import jax
import jax.numpy as jnp
from jax import lax
from jax.experimental import pallas as pl
from jax.experimental.pallas import tpu as pltpu

F32 = jnp.float32
BF16 = jnp.bfloat16

D_MODEL = 1024
N_META = 16
D_ATTN = D_MODEL // 2
D_RNN = D_MODEL - D_ATTN
N_HEADS = 8
QK_NOPE = 64
QK_ROPE = 32
QK_HEAD = QK_NOPE + QK_ROPE
V_HEAD = D_ATTN // N_HEADS
Q_LORA = 384
KV_LORA = 256
ROPE_BASE = 10000.0
CONV_W = 4
LRU_C = 8.0
N_EXPERTS = 8
TOP_K = 2
EPS = 1e-6

LANES = 128
MXU_WIDTH = 256
BF16_ROWS = 16
HALF_ROPE = QK_ROPE // 2
V_ROWS = V_HEAD + BF16_ROWS
D_LATENT = Q_LORA + KV_LORA + QK_ROPE
SEQS_PER_STEP = 8
VMEM_LIMIT = 56 * 1024 * 1024
NEG_INF = -1e30


def _rms(x):
    return x * lax.rsqrt(jnp.mean(x * x, axis=-1, keepdims=True) + EPS)


def _rms_rows(x):
    return x * lax.rsqrt(jnp.mean(x * x, axis=0, keepdims=True) + EPS)


def _full(shape):
    return pl.BlockSpec(shape, lambda *_: (0,) * len(shape))


def _pre_kernel(h_ref, cos_ref, sin_ref, gmix_ref, wrnn_ref, wlat_ref, gq_ref, gkv_ref, wuq_ref,
                wuk_ref, wuv_ref, gqh_ref, gkn_ref, gkr_ref, q_ref, k_ref, v_ref, xr_ref, gt_ref):
    for si in range(h_ref.shape[0]):
        u = (_rms(h_ref[si]) * gmix_ref[...]).astype(BF16)
        rnn_in = jnp.dot(u, wrnn_ref[...], preferred_element_type=F32)
        xr_ref[si] = rnn_in[:, :D_RNN].astype(BF16)
        gt_ref[si] = rnn_in[:, D_RNN:].astype(BF16)

        lat = lax.dot_general(wlat_ref[...], u, (((1,), (1,)), ((), ())),
                              preferred_element_type=F32)
        cqn = (_rms_rows(lat[:Q_LORA]) * gq_ref[...]).astype(BF16)
        ckvn = (_rms_rows(lat[Q_LORA:Q_LORA + KV_LORA]) * gkv_ref[...]).astype(BF16)
        kr = lat[Q_LORA + KV_LORA:]
        qt = jnp.dot(wuq_ref[...], cqn, preferred_element_type=F32)
        kt = jnp.dot(wuk_ref[...], ckvn, preferred_element_type=F32)
        vt = jnp.dot(wuv_ref[...], ckvn, preferred_element_type=F32)
        ta = vt.shape[1]
        ones = jnp.ones((V_ROWS - V_HEAD, ta), BF16)
        for hh in range(N_HEADS):
            v_ref[si, hh, :V_HEAD, :] = vt[hh * V_HEAD:(hh + 1) * V_HEAD].astype(BF16)
            v_ref[si, hh, V_HEAD:, :] = ones

        cos, sin = cos_ref[si], sin_ref[si]
        inv_dh = 1.0 / QK_HEAD
        r1, r2 = QK_NOPE, QK_NOPE + HALF_ROPE
        gqh = gqh_ref[...]
        zeros_q = jnp.zeros((LANES - QK_HEAD, ta), BF16)
        for hh in range(N_HEADS):
            x = qt[hh * LANES:(hh + 1) * LANES]
            inv = lax.rsqrt(jnp.sum(x * x, axis=0, keepdims=True) * inv_dh + EPS)
            xn = x[:QK_HEAD] * gqh * inv
            x1, x2 = xn[r1:r2], xn[r2:]
            q_ref[si, hh, :r1, :] = xn[:r1].astype(BF16)
            q_ref[si, hh, r1:r2, :] = (x1 * cos - x2 * sin).astype(BF16)
            q_ref[si, hh, r2:QK_HEAD, :] = (x2 * cos + x1 * sin).astype(BF16)
            q_ref[si, hh, QK_HEAD:, :] = zeros_q

        kr_ss = jnp.sum(kr * kr, axis=0, keepdims=True)
        krg = kr * gkr_ref[...]
        k1, k2 = krg[:HALF_ROPE], krg[HALF_ROPE:]
        kr_rot = jnp.concatenate([k1 * cos - k2 * sin, k2 * cos + k1 * sin,
                                  jnp.zeros((LANES - QK_HEAD, ta), F32)], axis=0)
        gkn = gkn_ref[...]
        for hh in range(N_HEADS):
            x = kt[hh * QK_NOPE:(hh + 1) * QK_NOPE]
            inv = lax.rsqrt((jnp.sum(x * x, axis=0, keepdims=True) + kr_ss) * inv_dh + EPS)
            kh = jnp.concatenate([x * gkn, kr_rot], axis=0) * inv
            k_ref[si, hh] = kh.T.astype(BF16)


def _pre_mixer(h, cos_t, sin_t, lw, ta):
    b, tp, d = h.shape
    ns = math.gcd(b, 4)
    tok_rows = lambda r: pl.BlockSpec((ns, r, ta), lambda bi, ti: (bi, 0, ti))
    head_t = lambda r: pl.BlockSpec((ns, N_HEADS, r, ta), lambda bi, ti: (bi, 0, 0, ti))
    rnn_rows = pl.BlockSpec((ns, ta, D_RNN), lambda bi, ti: (bi, ti, 0))
    return pl.pallas_call(
        _pre_kernel,
        grid=(b // ns, tp // ta),
        in_specs=[pl.BlockSpec((ns, ta, d), lambda bi, ti: (bi, ti, 0)),
                  tok_rows(HALF_ROPE), tok_rows(HALF_ROPE),
                  _full((1, d)), _full((d, 2 * D_RNN)), _full((D_LATENT, d)),
                  _full((Q_LORA, ta)), _full((KV_LORA, ta)),
                  _full((N_HEADS * LANES, Q_LORA)), _full((N_HEADS * QK_NOPE, KV_LORA)),
                  _full((N_HEADS * V_HEAD, KV_LORA)),
                  _full((QK_HEAD, ta)), _full((QK_NOPE, ta)), _full((QK_ROPE, ta))],
        out_specs=[head_t(LANES),
                   pl.BlockSpec((ns, N_HEADS, ta, LANES), lambda bi, ti: (bi, 0, ti, 0)),
                   head_t(V_ROWS), rnn_rows, rnn_rows],
        out_shape=[jax.ShapeDtypeStruct((b, N_HEADS, LANES, tp), BF16),
                   jax.ShapeDtypeStruct((b, N_HEADS, tp, LANES), BF16),
                   jax.ShapeDtypeStruct((b, N_HEADS, V_ROWS, tp), BF16),
                   jax.ShapeDtypeStruct((b, tp, D_RNN), BF16),
                   jax.ShapeDtypeStruct((b, tp, D_RNN), BF16)],
        compiler_params=pltpu.CompilerParams(
            dimension_semantics=("parallel", "parallel"), vmem_limit_bytes=VMEM_LIMIT),
        name="pre_mixer",
    )(h, cos_t, sin_t, lw["g_mix"], lw["w_rnn"], lw["w_lat"], lw["g_q"], lw["g_kv"], lw["w_uq"],
      lw["w_uk"], lw["w_uv"], lw["g_qh"], lw["g_kn"], lw["g_kr"])


def _attn_kernel(qi_ref, kj_ref, qt_ref, k_ref, vt_ref, o_ref, m_sc, acc_sc, *, tq):
    qi = qi_ref[pl.program_id(1)]
    kj = kj_ref[pl.program_id(1)]

    @pl.when(kj == 0)
    def _():
        m_sc[...] = jnp.full(m_sc.shape, NEG_INF, F32)
        acc_sc[...] = jnp.zeros(acc_sc.shape, F32)

    nseq = qt_ref.shape[0]

    def step(diagonal):
        if diagonal:
            keep = (lax.broadcasted_iota(jnp.int32, (tq, tq), 0)
                    <= lax.broadcasted_iota(jnp.int32, (tq, tq), 1))
        for si in range(nseq):
            sts = [jnp.dot(k_ref[si, hh], qt_ref[si, hh], preferred_element_type=F32)
                   for hh in range(N_HEADS)]
            for hh in range(N_HEADS):
                st = jnp.where(keep, sts[hh], NEG_INF) if diagonal else sts[hh]
                m_prev = m_sc[si, hh]
                m_new = jnp.maximum(m_prev, jnp.max(st, axis=0, keepdims=True))
                alpha = jnp.exp2(m_prev - m_new)
                p = jnp.exp2(st - m_new).astype(BF16)
                m_sc[si, hh] = m_new
                acc_sc[si, hh] = alpha * acc_sc[si, hh] + jnp.dot(vt_ref[si, hh], p,
                                                                  preferred_element_type=F32)

    @pl.when(kj < qi)
    def _():
        step(False)

    @pl.when(kj == qi)
    def _():
        step(True)
        for si in range(nseq):
            outs = []
            for hh in range(N_HEADS):
                acc = acc_sc[si, hh]
                outs.append(acc[:V_HEAD] / acc[V_HEAD:V_HEAD + 1])
            o_ref[si] = jnp.concatenate(outs, axis=0).T.astype(BF16)


def _attention(qt, k, vt, tq):
    b, nh, tp, _ = k.shape
    nt = tp // tq
    vrows = vt.shape[2]
    ns = SEQS_PER_STEP if b % SEQS_PER_STEP == 0 else 1
    pairs = [(qi, kj) for qi in range(nt) for kj in range(qi + 1)]
    qi_tab = jnp.asarray([p[0] for p in pairs], jnp.int32)
    kj_tab = jnp.asarray([p[1] for p in pairs], jnp.int32)
    grid_spec = pltpu.PrefetchScalarGridSpec(
        num_scalar_prefetch=2,
        grid=(b // ns, len(pairs)),
        in_specs=[pl.BlockSpec((ns, nh, LANES, tq), lambda bi, p, qt_, kt_: (bi, 0, 0, qt_[p])),
                  pl.BlockSpec((ns, nh, tq, LANES), lambda bi, p, qt_, kt_: (bi, 0, kt_[p], 0)),
                  pl.BlockSpec((ns, nh, vrows, tq), lambda bi, p, qt_, kt_: (bi, 0, 0, kt_[p]))],
        out_specs=pl.BlockSpec((ns, tq, D_ATTN), lambda bi, p, qt_, kt_: (bi, qt_[p], 0)),
        scratch_shapes=[pltpu.VMEM((ns, nh, 1, tq), F32), pltpu.VMEM((ns, nh, vrows, tq), F32)],
    )
    return pl.pallas_call(
        functools.partial(_attn_kernel, tq=tq),
        grid_spec=grid_spec,
        out_shape=jax.ShapeDtypeStruct((b, tp, D_ATTN), BF16),
        compiler_params=pltpu.CompilerParams(
            dimension_semantics=("parallel", "arbitrary"), vmem_limit_bytes=VMEM_LIMIT),
        name="attention",
    )(qi_tab, kj_tab, qt, k, vt)


def _sigmoid(x):
    return 0.5 * jnp.tanh(0.5 * x) + 0.5


def _gelu_tanh(x):
    k = math.sqrt(2.0 / math.pi)
    return x * (0.5 + 0.5 * jnp.tanh(x * (k + (k * 0.044715) * (x * x))))


def _rnn_kernel(xr_ref, gt_ref, perm_ref, permt_ref, cw_ref, cb_ref, wai_ref, bai_ref, lam_ref,
                o_ref, xe_sc, gt_sc, a_sc, b_sc, h_sc, *, tt, nb, rows):
    halo = (CONV_W - 1) * nb
    grp = MXU_WIDTH // nb
    blk = grp * nb

    @pl.when(pl.program_id(0) == 0)
    def _():
        h_sc[...] = jnp.zeros(h_sc.shape, F32)
        xe_sc[pl.ds(0, halo), :] = jnp.zeros((halo, xe_sc.shape[1]), F32)

    def gather_rows(ref, g):
        return jnp.concatenate([ref[bi, g * grp:(g + 1) * grp, :] for bi in range(nb)], axis=0)

    perm = perm_ref[...]
    for g in range(tt // grp):
        xe_sc[pl.ds(halo + g * blk, blk), :] = jnp.dot(perm, gather_rows(xr_ref, g),
                                                      preferred_element_type=F32)
        gt_sc[pl.ds(g * blk, blk), :] = jnp.dot(perm, gather_rows(gt_ref, g),
                                               preferred_element_type=F32)
    lam = lam_ref[...]
    log_a_unit = -LRU_C * (jnp.maximum(-lam, 0.0) + jnp.log1p(jnp.exp(-jnp.abs(lam))))
    half_log2_a = (0.5 * math.log2(math.e)) * log_a_unit
    cw = cw_ref[...]
    for r0 in range(0, tt * nb, rows):
        xc = cb_ref[...] + xe_sc[pl.ds(r0, rows), :] * cw[0:1]
        for j in range(1, CONV_W):
            xc = xc + xe_sc[pl.ds(r0 + j * nb, rows), :] * cw[j:j + 1]
        gates = jnp.dot(xc.astype(BF16), wai_ref[...], preferred_element_type=F32) + bai_ref[...]
        ig = _sigmoid(gates[:, D_RNN:])
        a = jnp.exp2(jnp.tanh(0.5 * gates[:, :D_RNN]) * half_log2_a + half_log2_a)
        a_sc[pl.ds(r0, rows), :] = a
        b_sc[pl.ds(r0, rows), :] = jnp.sqrt(1.0 - a * a) * ig * xc
    xe_sc[pl.ds(0, halo), :] = xe_sc[pl.ds(tt * nb, halo), :]

    def scan_step(t, h):
        sl = pl.ds(pl.multiple_of(t * nb, nb), nb)
        h = a_sc[sl, :] * h + b_sc[sl, :]
        b_sc[sl, :] = h
        return h

    h_sc[...] = lax.fori_loop(0, tt, scan_step, h_sc[...], unroll=8)

    permt = permt_ref[...]
    for g in range(tt // grp):
        sl = pl.ds(g * blk, blk)
        y = (b_sc[sl, :] * _gelu_tanh(gt_sc[sl, :])).astype(BF16)
        y = jnp.dot(permt, y, preferred_element_type=F32).astype(BF16)
        for bi in range(nb):
            o_ref[bi, g * grp:(g + 1) * grp, :] = y[bi * grp:(bi + 1) * grp]


def _rglru(xr, gt, lw, tt):
    nb, tp, c = xr.shape
    grp = MXU_WIDTH // nb
    src = (jnp.arange(MXU_WIDTH) % nb) * grp + jnp.arange(MXU_WIDTH) // nb
    perm = (src[:, None] == jnp.arange(MXU_WIDTH)[None, :]).astype(BF16)
    blk = pl.BlockSpec((nb, tt, c), lambda ti: (0, ti, 0))
    rows = math.gcd(tt * nb, 512)
    return pl.pallas_call(
        functools.partial(_rnn_kernel, tt=tt, nb=nb, rows=rows),
        grid=(tp // tt,),
        in_specs=[blk, blk, _full((MXU_WIDTH, MXU_WIDTH)), _full((MXU_WIDTH, MXU_WIDTH)),
                  _full((CONV_W, c)), _full((1, c)), _full((c, 2 * c)),
                  _full((1, 2 * c)), _full((1, c))],
        out_specs=blk,
        out_shape=jax.ShapeDtypeStruct((nb, tp, c), BF16),
        scratch_shapes=[pltpu.VMEM(((tt + CONV_W - 1) * nb, c), F32),
                        pltpu.VMEM((tt * nb, c), F32),
                        pltpu.VMEM((tt * nb, c), F32), pltpu.VMEM((tt * nb, c), F32),
                        pltpu.VMEM((nb, c), F32)],
        compiler_params=pltpu.CompilerParams(
            dimension_semantics=("arbitrary",), vmem_limit_bytes=VMEM_LIMIT),
        name="rglru",
    )(xr, gt, perm, perm.T, lw["conv_w"], lw["conv_b"], lw["w_ai"], lw["b_ai"], lw["lam"])


def _mix_residual(att, rnn, h, ga_ref, gr_ref, wo_ref):
    an = (_rms(att.astype(F32)) * ga_ref[...]).astype(BF16)
    rn = (_rms(rnn.astype(F32)) * gr_ref[...]).astype(BF16)
    mix = jnp.concatenate([an, rn], axis=-1)
    return h + jnp.dot(mix, wo_ref[...], preferred_element_type=F32)


def _post_router_kernel(att_ref, rnn_ref, h_ref, ga_ref, gr_ref, wo_ref, gf_ref, wr_ref,
                        ho_ref, u_ref, lg_ref):
    hn = _mix_residual(att_ref[...], rnn_ref[...], h_ref[...], ga_ref, gr_ref, wo_ref)
    ho_ref[...] = hn
    u = _rms(hn) * gf_ref[...]
    u_ref[...] = u.astype(BF16)
    w = wr_ref[...]
    u_hi = u.astype(BF16)
    w_hi = w.astype(BF16)
    u_lo = (u - u_hi.astype(F32)).astype(BF16)
    w_lo = (w - w_hi.astype(F32)).astype(BF16)
    hh_hl = jnp.dot(u_hi, jnp.concatenate([w_hi, w_lo], axis=1), preferred_element_type=F32)
    lg_ref[...] = (hh_hl[:, :LANES] + hh_hl[:, LANES:]
                   + jnp.dot(u_lo, w_hi, preferred_element_type=F32))


def _post_router(att, rnn, h, lw, w_router, tm):
    n, d = h.shape
    row = lambda c: pl.BlockSpec((tm, c), lambda i: (i, 0))
    return pl.pallas_call(
        _post_router_kernel,
        grid=(n // tm,),
        in_specs=[row(D_ATTN), row(D_RNN), row(d), _full((1, D_ATTN)), _full((1, D_RNN)),
                  _full((d, d)), _full((1, d)), _full((d, LANES))],
        out_specs=[row(d), row(d), row(LANES)],
        out_shape=[jax.ShapeDtypeStruct((n, d), F32), jax.ShapeDtypeStruct((n, d), BF16),
                   jax.ShapeDtypeStruct((n, LANES), F32)],
        compiler_params=pltpu.CompilerParams(
            dimension_semantics=("parallel",), vmem_limit_bytes=VMEM_LIMIT),
        name="post_router",
    )(att, rnn, h, lw["g_att"], lw["g_rnn"], lw["w_out"], lw["g_ffn"], w_router)


def _swiglu_tile(u, wg_ref, wu_ref, wd_ref, act_sc):
    f = act_sc.shape[-1]
    full = f - f % MXU_WIDTH
    for c0 in range(0, full, MXU_WIDTH):
        g = jnp.dot(u, wg_ref[:, c0:c0 + MXU_WIDTH], preferred_element_type=F32)
        up = jnp.dot(u, wu_ref[:, c0:c0 + MXU_WIDTH], preferred_element_type=F32)
        act_sc[:, c0:c0 + MXU_WIDTH] = (jax.nn.silu(g) * up).astype(BF16)
    if full < f:
        w = jnp.concatenate([wg_ref[:, full:], wu_ref[:, full:]], axis=1)
        gu = jnp.dot(u, w, preferred_element_type=F32)
        act_sc[:, full:] = (jax.nn.silu(gu[:, :f - full]) * gu[:, f - full:]).astype(BF16)
    return jnp.dot(act_sc[...], wd_ref[...], preferred_element_type=F32)


def _post_ffn_kernel(att_ref, rnn_ref, h_ref, ga_ref, gr_ref, wo_ref, gf_ref, wg_ref, wu_ref,
                     wd_ref, o_ref, act_sc):
    hn = _mix_residual(att_ref[...], rnn_ref[...], h_ref[...], ga_ref, gr_ref, wo_ref)
    u = (_rms(hn) * gf_ref[...]).astype(BF16)
    o_ref[...] = hn + _swiglu_tile(u, wg_ref, wu_ref, wd_ref, act_sc)


def _post_dense_ffn(att, rnn, h, lw, wg, wu, wd, tm):
    n, d = h.shape
    f = wd.shape[0]
    row = lambda c: pl.BlockSpec((tm, c), lambda i: (i, 0))
    once = lambda r, c: pl.BlockSpec((r, c), lambda i: (0, 0), pipeline_mode=pl.Buffered(1))
    return pl.pallas_call(
        _post_ffn_kernel,
        grid=(n // tm,),
        in_specs=[row(D_ATTN), row(D_RNN), row(d), _full((1, D_ATTN)), _full((1, D_RNN)),
                  once(d, d), _full((1, d)), once(d, f), once(d, f), once(f, d)],
        out_specs=row(d),
        out_shape=jax.ShapeDtypeStruct((n, d), F32),
        scratch_shapes=[pltpu.VMEM((tm, f), BF16)],
        compiler_params=pltpu.CompilerParams(
            dimension_semantics=("parallel",), vmem_limit_bytes=VMEM_LIMIT),
        name="post_dense_ffn",
    )(att, rnn, h, lw["g_att"], lw["g_rnn"], lw["w_out"], lw["g_ffn"], wg, wu, wd)


def _moe_kernel(te_ref, nt_ref, x_ref, wg_ref, wu_ref, wd_ref, o_ref, wg_sc, wu_sc, wd_sc, act_sc):
    i = pl.program_id(0)

    @pl.when((i == 0) | (te_ref[i] != te_ref[jnp.maximum(i - 1, 0)]))
    def _():
        wg_sc[...] = wg_ref[0, 0].astype(BF16)
        wu_sc[...] = wu_ref[0, 0].astype(BF16)
        wd_sc[...] = wd_ref[0, 0].astype(BF16)

    @pl.when(i < nt_ref[0])
    def _():
        o_ref[...] = _swiglu_tile(x_ref[...], wg_sc, wu_sc, wd_sc, act_sc).astype(BF16)

    @pl.when(i >= nt_ref[0])
    def _():
        o_ref[...] = jnp.zeros(o_ref.shape, BF16)


def _grouped_ffn(tile_expert, n_tiles, xs, wg, wu, wd, layer, tm):
    a, d = xs.shape
    f = wd.shape[2]
    by_expert = lambda r, c: pl.BlockSpec((1, 1, r, c), lambda i, te, nt: (layer, te[i], 0, 0))
    grid_spec = pltpu.PrefetchScalarGridSpec(
        num_scalar_prefetch=2,
        grid=(a // tm,),
        in_specs=[pl.BlockSpec((tm, d), lambda i, te, nt: (i, 0)),
                  by_expert(d, f), by_expert(d, f), by_expert(f, d)],
        out_specs=pl.BlockSpec((tm, d), lambda i, te, nt: (i, 0)),
        scratch_shapes=[pltpu.VMEM((d, f), BF16), pltpu.VMEM((d, f), BF16),
                        pltpu.VMEM((f, d), BF16), pltpu.VMEM((tm, f), BF16)],
    )
    return pl.pallas_call(
        _moe_kernel,
        grid_spec=grid_spec,
        out_shape=jax.ShapeDtypeStruct((a, d), BF16),
        compiler_params=pltpu.CompilerParams(
            dimension_semantics=("arbitrary",), vmem_limit_bytes=VMEM_LIMIT),
        name="moe_ffn",
    )(tile_expert, n_tiles, xs, wg, wu, wd)


def _moe_layer(u, h, logits, wg, wu, wd, layer, tm, keep=None):
    n, d = h.shape
    n2 = TOP_K * n
    experts = jnp.arange(N_EXPERTS, dtype=jnp.int32)
    tokens = jnp.arange(n, dtype=jnp.int32)
    lt = logits[:, :N_EXPERTS].T
    eid = experts[:, None]
    m1 = jnp.max(lt, axis=0)
    i1 = jnp.min(jnp.where(lt == m1, eid, N_EXPERTS), axis=0)
    lt2 = jnp.where(eid == i1, -jnp.inf, lt)
    m2 = jnp.max(lt2, axis=0)
    i2 = jnp.min(jnp.where(lt2 == m2, eid, N_EXPERTS), axis=0)
    e2 = jnp.exp(m2 - m1)
    top_w = jnp.stack([1.0 / (1.0 + e2), e2 / (1.0 + e2)], axis=-1)
    oh1, oh2 = eid == i1, eid == i2
    cnt = oh1.astype(jnp.int32) + oh2.astype(jnp.int32)
    csum = jnp.cumsum(cnt, axis=1)
    before = csum - cnt
    counts = csum[:, -1]
    padded = ((counts + tm - 1) // tm) * tm
    ends = jnp.cumsum(padded)
    starts = ends - padded
    slot_of = lambda oh: jnp.sum(jnp.where(oh, before + starts[:, None], 0), axis=0)
    pos = jnp.stack([slot_of(oh1), slot_of(oh2)], axis=-1)
    order = jnp.sort(jnp.concatenate([i1 * n + tokens, i2 * n + tokens])) % n
    a_pad = (n2 // tm + N_EXPERTS) * tm
    slot = jnp.arange(a_pad, dtype=jnp.int32)
    slot_onehot = (jnp.sum((slot[:, None] >= ends[None, :]).astype(jnp.int32), axis=-1,
                           keepdims=True) == experts[None, :]).astype(jnp.int32)
    r = slot - jnp.sum(slot_onehot * starts[None, :], axis=-1)
    valid = r < jnp.sum(slot_onehot * counts[None, :], axis=-1)
    dense = jnp.sum(slot_onehot * (jnp.cumsum(counts) - counts)[None, :], axis=-1) + r
    slot_tok = jnp.where(valid, jnp.take(order, dense, mode="clip"), slot % n)
    tile_start = jnp.arange(a_pad // tm, dtype=jnp.int32) * tm
    tile_expert = jnp.minimum(jnp.sum((tile_start[:, None] >= ends[None, :]).astype(jnp.int32),
                                      axis=-1), N_EXPERTS - 1).astype(jnp.int32)
    n_tiles = (ends[-1] // tm).astype(jnp.int32).reshape(1)
    xs = jnp.take(u, slot_tok, axis=0, mode="clip")
    y = _grouped_ffn(tile_expert, n_tiles, xs, wg, wu, wd, layer, tm)
    if keep is not None:
        b, tp, lo, hi = keep
        rows = lambda v: v.reshape(b, tp, v.shape[-1])[:, lo:hi].reshape(b * (hi - lo), v.shape[-1])
        h, pos, top_w = rows(h), rows(pos), rows(top_w)
    return (h + top_w[:, 0:1] * jnp.take(y, pos[:, 0], axis=0).astype(F32)
            + top_w[:, 1:2] * jnp.take(y, pos[:, 1], axis=0).astype(F32))


def _block_diag(w):
    nb, c, _ = w.shape
    eye = jnp.eye(nb, dtype=w.dtype)
    return (eye[:, None, :, None] * w[:, :, None, :]).reshape(nb * c, nb * c)


def _lane_table(g, ta):
    return jnp.broadcast_to(g[:, None], (g.shape[0], ta))


def _layer_weights(l, p, ta):
    w_in = p["w_in"][l]
    w_uq = p["w_uq"][l].reshape(Q_LORA, N_HEADS, QK_HEAD)
    w_uq = jnp.pad(w_uq, ((0, 0), (0, 0), (0, LANES - QK_HEAD))).reshape(Q_LORA, N_HEADS * LANES)
    w_ukv = p["w_ukv"][l].reshape(KV_LORA, N_HEADS, QK_NOPE + V_HEAD)
    w_uk = w_ukv[:, :, :QK_NOPE].reshape(KV_LORA, N_HEADS * QK_NOPE)
    w_uv = w_ukv[:, :, QK_NOPE:].reshape(KV_LORA, N_HEADS * V_HEAD)
    q_gain = p["q_head_g"][l] * (QK_HEAD ** -0.5 * math.log2(math.e))
    return {
        "g_mix": p["norm_mix_g"][l].reshape(1, -1),
        "w_rnn": w_in[:, D_LATENT:].astype(BF16),
        "w_lat": w_in[:, :D_LATENT].T.astype(BF16),
        "g_q": _lane_table(p["q_norm_g"][l], ta),
        "g_kv": _lane_table(p["kv_norm_g"][l], ta),
        "w_uq": w_uq.T.astype(BF16),
        "w_uk": w_uk.T.astype(BF16),
        "w_uv": w_uv.T.astype(BF16),
        "g_qh": _lane_table(q_gain, ta),
        "g_kn": _lane_table(p["k_head_g"][l][:QK_NOPE], ta),
        "g_kr": _lane_table(p["k_head_g"][l][QK_NOPE:], ta),
        "conv_w": p["conv_w"][l],
        "conv_b": p["conv_b"][l].reshape(1, -1),
        "w_ai": jnp.concatenate([_block_diag(p["w_a"][l]), _block_diag(p["w_i"][l])],
                                axis=1).astype(BF16),
        "b_ai": jnp.concatenate([p["b_a"][l], p["b_i"][l]]).reshape(1, -1),
        "lam": p["lru_lambda"][l].reshape(1, -1),
        "g_att": p["attn_out_g"][l].reshape(1, -1),
        "g_rnn": p["rnn_out_g"][l].reshape(1, -1),
        "w_out": p["w_out"][l].astype(BF16),
        "g_ffn": p["norm_ffn_g"][l].reshape(1, -1),
    }


def _rope_tables(pos_full):
    inv_freq = ROPE_BASE ** (-jnp.arange(0, QK_ROPE, 2, dtype=F32) / QK_ROPE)
    ang = pos_full.astype(F32)[:, None, :] * inv_freq[None, :, None]
    return jnp.cos(ang), jnp.sin(ang)


def _largest_tile(total, cap, unit):
    best = unit
    for cand in range(unit, cap + 1, unit):
        if total % cand == 0:
            best = cand
    return best


def _pick_tiles(t, b):
    tq = 384 if t > 1024 else LANES
    tp = -(-t // tq) * tq
    tm_row = _largest_tile(b * tp, 512, 256)
    return tp, tq, tm_row


def kernel(x, positions, meta_tokens, norm_mix_g, w_in, q_norm_g, w_uq, kv_norm_g, w_ukv, q_head_g, k_head_g, conv_w, conv_b, w_a, b_a, w_i, b_i, lru_lambda, attn_out_g, rnn_out_g, w_out, norm_ffn_g, ffn_w_gate, ffn_w_up, ffn_w_down, router_w, moe_w_gate, moe_w_up, moe_w_down):
    p = dict(norm_mix_g=norm_mix_g, w_in=w_in, q_norm_g=q_norm_g, w_uq=w_uq, kv_norm_g=kv_norm_g,
             w_ukv=w_ukv, q_head_g=q_head_g, k_head_g=k_head_g, conv_w=conv_w, conv_b=conv_b,
             w_a=w_a, b_a=b_a, w_i=w_i, b_i=b_i, lru_lambda=lru_lambda, attn_out_g=attn_out_g,
             rnn_out_g=rnn_out_g, w_out=w_out, norm_ffn_g=norm_ffn_g)
    b, seq, d = x.shape
    depth = w_in.shape[0]
    t = N_META + seq
    tp, tq, tm_row = _pick_tiles(t, b)
    n = b * tp

    meta = jnp.broadcast_to(meta_tokens[None].astype(x.dtype), (b, N_META, d))
    h = jnp.concatenate([meta, x, jnp.zeros((b, tp - t, d), x.dtype)], axis=1)
    meta_pos = jnp.broadcast_to(jnp.arange(N_META, dtype=jnp.int32)[None, :], (b, N_META))
    pos_full = jnp.concatenate([meta_pos, positions.astype(jnp.int32) + N_META,
                                jnp.zeros((b, tp - t), jnp.int32)], axis=1)
    cos_t, sin_t = _rope_tables(pos_full)

    for l in range(depth):
        lw = _layer_weights(l, p, tq)
        qt, k, vt, xr, gt = _pre_mixer(h, cos_t, sin_t, lw, tq)
        att = _attention(qt, k, vt, tq)
        rnn = _rglru(xr, gt, lw, tq)
        j = l // 2
        att, rnn, h = att.reshape(n, D_ATTN), rnn.reshape(n, D_RNN), h.reshape(n, d)
        if l % 2 == 0:
            h = _post_dense_ffn(att, rnn, h, lw, ffn_w_gate[j].astype(BF16),
                                ffn_w_up[j].astype(BF16), ffn_w_down[j].astype(BF16), tm_row)
        else:
            w_r = jnp.pad(router_w[j], ((0, 0), (0, LANES - N_EXPERTS)))
            h, u, logits = _post_router(att, rnn, h, lw, w_r, _largest_tile(n, 1024, 256))
            last = l == depth - 1
            h = _moe_layer(u, h, logits, moe_w_gate, moe_w_up, moe_w_down, j, tm_row,
                           keep=(b, tp, N_META, t) if last else None)
            if last:
                return h.reshape(b, seq, d)
        h = h.reshape(b, tp, d)
    return h[:, N_META:t]
```
